```python
import math
import jax
import jax.numpy as jnp
from jax import lax
import numpy as np

D_MODEL = 1024
BATCH = 2
SEQ = 8192
DEPTH = 2
DEC_BATCH = 32
DEC_SEQ = 1
PAST_LEN = 8192
PAGE_SIZE = 128

N_EVEN = (DEPTH + 1) // 2
N_ODD = DEPTH // 2
GROUP_WIDTH = D_MODEL // 2

LRU_HEADS = 8
LRU_WIDTH = GROUP_WIDTH
LRU_BLOCK = LRU_WIDTH // LRU_HEADS
LRU_CONV = 4
LRU_C = 8.0

MOBA_HEADS = 8
MOBA_HEAD_DIM = GROUP_WIDTH // MOBA_HEADS
MOBA_BLOCK = 256
MOBA_TOPK = 3
MOBA_Q_CHUNK = 32

RWKV_HEADS = 8
RWKV_HEAD_DIM = GROUP_WIDTH // RWKV_HEADS
RWKV_DECAY_LORA = 64
RWKV_AAA_LORA = 64
RWKV_GATE_LORA = 128
RWKV_COLS = 3 * GROUP_WIDTH + RWKV_DECAY_LORA + RWKV_AAA_LORA + RWKV_GATE_LORA
RWKV_LN_EPS = 64e-5

DIFF_HEADS = 4
DIFF_QK_DIM = 64
DIFF_V_DIM = GROUP_WIDTH // DIFF_HEADS
DIFF_Q_BLOCK = 128

N_BUCKETS = 32
T5_MAX_EXACT = 16
T5_MAX_DISTANCE = 128
T5_HEADS = 8

D_FF = 4 * D_MODEL
FFN_CONV = 3
PLE_DIM = 256

EVEN_IN = 2 * LRU_WIDTH + 3 * MOBA_HEADS * MOBA_HEAD_DIM
ODD_IN = RWKV_COLS + 2 * DIFF_HEADS * 2 * DIFF_QK_DIM + DIFF_HEADS * DIFF_V_DIM

kernel_name = 'hybrid_rglru_moba_rwkv7_diffattn_step'


def rmsnorm(x, g, eps=1e-6):
    xf = x.astype(jnp.float32)
    y = xf * lax.rsqrt(jnp.mean(xf * xf, axis=-1, keepdims=True) + eps)
    return (y * g.astype(jnp.float32)).astype(x.dtype)


def causal_depthwise_conv(buf, u, w, b):
    width = w.shape[0]
    t = u.shape[1]
    full = jnp.concatenate([buf.astype(u.dtype), u], axis=1)
    out = b + sum(w[j] * full[:, j:j + t] for j in range(width))
    return out, full[:, t:]


def t5_bucket(rel):
    n = jnp.maximum(rel, 0)
    nf = jnp.maximum(n, 1).astype(jnp.float32)
    large = T5_MAX_EXACT + (jnp.log(nf / T5_MAX_EXACT) / math.log(T5_MAX_DISTANCE / T5_MAX_EXACT)
                            * (N_BUCKETS - T5_MAX_EXACT)).astype(jnp.int32)
    large = jnp.minimum(large, N_BUCKETS - 1)
    return jnp.where(n < T5_MAX_EXACT, n, large)


def gather_pages(pool, page_table):
    g = pool[page_table]
    return g.reshape((page_table.shape[0], -1) + pool.shape[2:])


def rglru_branch(gate_in, x_in, conv_buf, h0, conv_w, conv_b, w_a, b_a, w_x, b_x, lam):
    b, t, _ = x_in.shape
    xc, new_buf = causal_depthwise_conv(conv_buf, x_in, conv_w, conv_b)
    xh = xc.reshape(b, t, LRU_HEADS, LRU_BLOCK)
    r = jax.nn.sigmoid(jnp.einsum('bthi,hij->bthj', xh, w_a).reshape(b, t, LRU_WIDTH) + b_a)
    i = jax.nn.sigmoid(jnp.einsum('bthi,hij->bthj', xh, w_x).reshape(b, t, LRU_WIDTH) + b_x)
    log_a = -LRU_C * jax.nn.softplus(-lam.astype(jnp.float32)) * r.astype(jnp.float32)
    a = jnp.exp(log_a)
    u = jnp.sqrt(-jnp.expm1(2.0 * log_a)) * (i * xc).astype(jnp.float32)

    def step(h, au):
        a_t, u_t = au
        h = a_t * h + u_t
        return h, h

    h_last, hs = lax.scan(step, h0.astype(jnp.float32), (jnp.swapaxes(a, 0, 1), jnp.swapaxes(u, 0, 1)))
    hs = jnp.swapaxes(hs, 0, 1).astype(x_in.dtype)
    return hs * jax.nn.gelu(gate_in, approximate=True), new_buf, h_last


def moba_attention(q, k, v, pos_q, t5_bias):
    b, tq, nh, dh = q.shape
    tk = k.shape[1]
    nb = -(-tk // MOBA_BLOCK)
    pad = nb * MOBA_BLOCK - tk
    kb = jnp.pad(k, ((0, 0), (0, pad), (0, 0), (0, 0))).reshape(b, nb, MOBA_BLOCK, nh, dh).transpose(0, 3, 1, 2, 4)
    vb = jnp.pad(v, ((0, 0), (0, pad), (0, 0), (0, 0))).reshape(b, nb, MOBA_BLOCK, nh, dh).transpose(0, 3, 1, 2, 4)
    k_mean = jnp.mean(kb.astype(jnp.float32), axis=3)
    bias_tab = t5_bias.T[:MOBA_HEADS]
    scale = dh ** -0.5
    n_top = min(MOBA_TOPK, nb)
    q_chunk = MOBA_Q_CHUNK if tq % MOBA_Q_CHUNK == 0 else tq
    n_chunks = tq // q_chunk
    b_ix = jnp.arange(b)[:, None, None, None]
    h_ix = jnp.arange(nh)[None, :, None, None]
    blk_ix = jnp.arange(nb)

    def chunk(args):
        qc, pc = args
        gate = jnp.einsum('bqhd,bhnd->bhqn', qc.astype(jnp.float32), k_mean)
        own = pc // MOBA_BLOCK
        gate = jnp.where(blk_ix[None, :] < own[:, None], gate, -jnp.inf)
        top_s, top_i = lax.top_k(gate, n_top)
        own_b = jnp.broadcast_to(own[None, None, :, None], (b, nh, q_chunk, 1))
        idx = jnp.concatenate([top_i, own_b], axis=-1)
        valid = jnp.concatenate([jnp.isfinite(top_s), jnp.ones_like(own_b, dtype=bool)], axis=-1)
        k_sel = kb[b_ix, h_ix, idx]
        v_sel = vb[b_ix, h_ix, idx]
        logits = jnp.einsum('bqhd,bhqsjd->bhqsj', qc, k_sel, preferred_element_type=jnp.float32) * scale
        pos_k = idx[..., None] * MOBA_BLOCK + jnp.arange(MOBA_BLOCK)
        rel = pc[None, None, :, None, None] - pos_k
        bias = bias_tab[h_ix[..., None], t5_bucket(rel)]
        mask = valid[..., None] & (rel >= 0)
        logits = jnp.where(mask, logits + bias, -jnp.inf)
        p = jax.nn.softmax(logits.reshape(b, nh, q_chunk, -1), axis=-1)
        return jnp.einsum('bhqn,bhqnd->bqhd', p.astype(v.dtype), v_sel.reshape(b, nh, q_chunk, -1, dh))

    qs = q.reshape(b, n_chunks, q_chunk, nh, dh).transpose(1, 0, 2, 3, 4)
    ps = pos_q.reshape(n_chunks, q_chunk)
    outs = lax.map(chunk, (qs, ps))
    return outs.transpose(1, 0, 2, 3, 4).reshape(b, tq, nh, dh)


def diff_attention(q, k, v, pos_q, lam, t5_bias):
    b, tq, nh, _, dqk = q.shape
    tk = k.shape[1]
    dv = v.shape[-1]
    pos_k = jnp.arange(tk, dtype=jnp.int32)
    bias_tab = t5_bias.T[:2 * DIFF_HEADS].reshape(DIFF_HEADS, 2, N_BUCKETS)
    scale = dqk ** -0.5
    q_block = DIFF_Q_BLOCK if tq % DIFF_Q_BLOCK == 0 else tq
    n_blocks = tq // q_block

    def block(args):
        qb, pb = args
        logits = jnp.einsum('bqhmd,bkhmd->bhmqk', qb, k, preferred_element_type=jnp.float32) * scale
        rel = pb[:, None] - pos_k[None, :]
        bias = bias_tab[:, :, t5_bucket(rel)]
        logits = jnp.where(rel >= 0, logits + bias, -jnp.inf)
        a = jax.nn.softmax(logits, axis=-1)
        attn = a[:, :, 0] - lam * a[:, :, 1]
        return jnp.einsum('bhqk,bkhd->bqhd', attn.astype(v.dtype), v)

    qs = q.reshape(b, n_blocks, q_block, nh, 2, dqk).transpose(1, 0, 2, 3, 4, 5)
    ps = pos_q.reshape(n_blocks, q_block)
    outs = lax.map(block, (qs, ps))
    return outs.transpose(1, 0, 2, 3, 4).reshape(b, tq, nh, dv)


def rwkv7_branch(z, shift_buf, s0, mu, w0, w2, a0, a2, g2, k_k, k_a, r_k, ln_w, ln_b):
    b, t, _ = z.shape
    g = GROUP_WIDTH
    z_prev = jnp.concatenate([shift_buf[:, None].astype(z.dtype), z[:, :-1]], axis=1)
    zs = z + mu * (z_prev - z)
    new_shift = z[:, -1]
    o = 3 * g
    r, k, v = zs[..., :g], zs[..., g:2 * g], zs[..., 2 * g:o]
    wd = zs[..., o:o + RWKV_DECAY_LORA]
    ad = zs[..., o + RWKV_DECAY_LORA:o + RWKV_DECAY_LORA + RWKV_AAA_LORA]
    gd = zs[..., o + RWKV_DECAY_LORA + RWKV_AAA_LORA:]
    w = -jax.nn.softplus(-(w0 + jnp.tanh(wd) @ w2).astype(jnp.float32)) - 0.5
    decay = jnp.exp(-jnp.exp(w))
    a = jax.nn.sigmoid(a0 + ad @ a2)
    gate = jax.nn.sigmoid(gd) @ g2
    heads = lambda u: u.reshape(b, t, RWKV_HEADS, RWKV_HEAD_DIM).astype(jnp.float32)
    kk = heads(k * k_k)
    kk = kk / jnp.maximum(jnp.sqrt(jnp.sum(kk * kk, axis=-1, keepdims=True)), 1e-12)
    k = k * (1 + (a - 1) * k_a)
    rh, kh, vh, ah, dh = heads(r), heads(k), heads(v), heads(a), heads(decay)

    def step(s, inp):
        r_t, w_t, k_t, v_t, kk_t, a_t = inp
        sa = jnp.einsum('bhvk,bhk->bhv', s, -kk_t)
        s = s * w_t[:, :, None, :] + sa[..., None] * (kk_t * a_t)[:, :, None, :] + v_t[..., None] * k_t[:, :, None, :]
        return s, jnp.einsum('bhvk,bhk->bhv', s, r_t)

    xs = tuple(jnp.swapaxes(u, 0, 1) for u in (rh, dh, kh, vh, kk, ah))
    s_last, ys = lax.scan(step, s0.astype(jnp.float32), xs)
    y = jnp.swapaxes(ys, 0, 1)
    mean = jnp.mean(y, axis=-1, keepdims=True)
    var = jnp.mean(jnp.square(y - mean), axis=-1, keepdims=True)
    y = ((y - mean) * lax.rsqrt(var + RWKV_LN_EPS)).reshape(b, t, g) * ln_w + ln_b
    bonus = jnp.sum(rh * kh * r_k, axis=-1, keepdims=True) * vh
    y = y + bonus.reshape(b, t, g)
    return (y * gate).astype(z.dtype), new_shift, s_last


def even_mixer(h, k_past, v_past, conv_buf, h0, t5_bias, w_in, w_out, conv_w, conv_b, w_a, b_a, w_x, b_x, lam):
    b, t, _ = h.shape
    g = GROUP_WIDTH
    proj = h @ w_in
    lru_y, new_buf, h_last = rglru_branch(proj[..., :g], proj[..., g:2 * g], conv_buf, h0,
                                          conv_w, conv_b, w_a, b_a, w_x, b_x, lam)
    heads = lambda u: u.reshape(b, t, MOBA_HEADS, MOBA_HEAD_DIM)
    q = heads(proj[..., 2 * g:3 * g])
    k = heads(proj[..., 3 * g:4 * g])
    v = heads(proj[..., 4 * g:5 * g])
    k_all = jnp.concatenate([k_past.astype(k.dtype), k], axis=1)
    v_all = jnp.concatenate([v_past.astype(v.dtype), v], axis=1)
    pos_q = k_past.shape[1] + jnp.arange(t, dtype=jnp.int32)
    att = moba_attention(q, k_all, v_all, pos_q, t5_bias)
    out = jnp.concatenate([lru_y, att.reshape(b, t, g)], axis=-1) @ w_out
    return out, k, v, new_buf, h_last


def odd_mixer(h, k_past, v_past, shift_buf, s0, layer_idx, t5_bias, w_in, w_out, mu, w0, w2, a0, a2, g2,
              k_k, k_a, r_k, ln_w, ln_b, diff_lambda, subln_w):
    b, t, _ = h.shape
    g = GROUP_WIDTH
    proj = h @ w_in
    rw_y, new_shift, s_last = rwkv7_branch(proj[..., :RWKV_COLS], shift_buf, s0, mu, w0, w2, a0, a2, g2,
                                           k_k, k_a, r_k, ln_w, ln_b)
    o = RWKV_COLS
    nq = DIFF_HEADS * 2 * DIFF_QK_DIM
    q = proj[..., o:o + nq].reshape(b, t, DIFF_HEADS, 2, DIFF_QK_DIM)
    k = proj[..., o + nq:o + 2 * nq].reshape(b, t, DIFF_HEADS, 2, DIFF_QK_DIM)
    v = proj[..., o + 2 * nq:].reshape(b, t, DIFF_HEADS, DIFF_V_DIM)
    k_all = jnp.concatenate([k_past.astype(k.dtype), k], axis=1)
    v_all = jnp.concatenate([v_past.astype(v.dtype), v], axis=1)
    pos_q = k_past.shape[1] + jnp.arange(t, dtype=jnp.int32)
    lam_init = 0.8 - 0.6 * math.exp(-0.3 * layer_idx)
    lf = diff_lambda.astype(jnp.float32)
    lam = jnp.exp(jnp.sum(lf[0] * lf[1])) - jnp.exp(jnp.sum(lf[2] * lf[3])) + lam_init
    att = diff_attention(q, k_all, v_all, pos_q, lam, t5_bias)
    att = rmsnorm(att, subln_w, 1e-5) * (1.0 - lam_init)
    out = jnp.concatenate([rw_y, att.reshape(b, t, g)], axis=-1) @ w_out
    return out, k, v, new_shift, s_last


def conv_ffn(h, buf, w_up, conv_w, conv_b, w_down):
    u = h @ w_up
    c, new_buf = causal_depthwise_conv(buf, u, conv_w, conv_b)
    gate, val = jnp.split(c, 2, axis=-1)
    return (jax.nn.gelu(gate, approximate=True) * val) @ w_down, new_buf


def trunk(x, p, moba_past, lru_conv0, lru_h0, diff_past, rwkv_shift0, rwkv_s0, ffn_conv0, W):
    moba_k, moba_v, lru_conv, lru_h = [], [], [], []
    diff_k, diff_v, rwkv_shift, rwkv_s, ffn_conv = [], [], [], [], []
    for i in range(DEPTH):
        j = i // 2
        h = rmsnorm(x, W['norm_mix_pre'][i])
        if i % 2 == 0:
            out, k, v, cbuf, h_last = even_mixer(
                h, moba_past[j][0], moba_past[j][1], lru_conv0[j], lru_h0[j], W['t5_bias'],
                W['even_w_in'][j], W['even_w_out'][j], W['lru_conv_w'][j], W['lru_conv_b'][j],
                W['lru_w_a'][j], W['lru_b_a'][j], W['lru_w_x'][j], W['lru_b_x'][j], W['lru_lambda'][j])
            moba_k.append(k)
            moba_v.append(v)
            lru_conv.append(cbuf)
            lru_h.append(h_last)
        else:
            out, k, v, sbuf, s_last = odd_mixer(
                h, diff_past[j][0], diff_past[j][1], rwkv_shift0[j], rwkv_s0[j], i, W['t5_bias'],
                W['odd_w_in'][j], W['odd_w_out'][j], W['rwkv_mu'][j], W['rwkv_w0'][j], W['rwkv_w2'][j],
                W['rwkv_a0'][j], W['rwkv_a2'][j], W['rwkv_g2'][j], W['rwkv_k_k'][j], W['rwkv_k_a'][j],
                W['rwkv_r_k'][j], W['rwkv_ln_w'][j], W['rwkv_ln_b'][j], W['diff_lambda'][j], W['diff_subln_w'][j])
            diff_k.append(k)
            diff_v.append(v)
            rwkv_shift.append(sbuf)
            rwkv_s.append(s_last)
        x = x + rmsnorm(out, W['norm_mix_post'][i])
        h = rmsnorm(x, W['norm_ffn_pre'][i])
        f, fbuf = conv_ffn(h, ffn_conv0[i], W['ffn_w_up'][i], W['ffn_conv_w'][i], W['ffn_conv_b'][i], W['ffn_w_down'][i])
        ffn_conv.append(fbuf)
        x = x + rmsnorm(f, W['norm_ffn_post'][i])
        x = x + jax.nn.sigmoid(x @ W['ple_w_gate'][i]) * (p[i] @ W['ple_w_proj'][i])
    st = jnp.stack
    return (x, st(moba_k), st(moba_v), st(lru_conv), st(lru_h), st(diff_k), st(diff_v),
            st(rwkv_shift), st(rwkv_s), st(ffn_conv))


def setup_inputs(seed: int = 0) -> dict:
    key = jax.random.key(seed)
    ks = iter(jax.random.split(key, 64))
    f32 = jnp.float32

    def nrm(shape, scale):
        return scale * jax.random.normal(next(ks), shape, f32)

    def gain(shape):
        return 1.0 + nrm(shape, 0.05)

    n_pages = PAST_LEN // PAGE_SIZE
    n_used = DEC_BATCH * n_pages
    n_pool = n_used + (n_used + 3) // 4

    x_prompt = nrm((BATCH, SEQ, D_MODEL), 1.0)
    x_sample = nrm((DEC_BATCH, DEC_SEQ, D_MODEL), 1.0)
    cache_moba_k = nrm((N_EVEN, n_pool, PAGE_SIZE, MOBA_HEADS, MOBA_HEAD_DIM), 1.0)
    cache_moba_v = nrm((N_EVEN, n_pool, PAGE_SIZE, MOBA_HEADS, MOBA_HEAD_DIM), 1.0)
    state_lru_conv = nrm((N_EVEN, DEC_BATCH, LRU_CONV - 1, LRU_WIDTH), 1.0)
    state_lru_h = nrm((N_EVEN, DEC_BATCH, LRU_WIDTH), 0.5)
    cache_diff_k = nrm((N_ODD, n_pool, PAGE_SIZE, DIFF_HEADS, 2, DIFF_QK_DIM), 1.0)
    cache_diff_v = nrm((N_ODD, n_pool, PAGE_SIZE, DIFF_HEADS, DIFF_V_DIM), 1.0)
    state_rwkv_shift = nrm((N_ODD, DEC_BATCH, RWKV_COLS), 1.0)
    state_rwkv = nrm((N_ODD, DEC_BATCH, RWKV_HEADS, RWKV_HEAD_DIM, RWKV_HEAD_DIM), 0.5)
    state_ffn_conv = nrm((DEPTH, DEC_BATCH, FFN_CONV - 1, 2 * D_FF), 1.0)
    page_table = jax.random.permutation(next(ks), n_pool)[:n_used].reshape(DEC_BATCH, n_pages).astype(jnp.int32)
    p_prompt = nrm((DEPTH, BATCH, SEQ, PLE_DIM), 1.0)
    p_sample = nrm((DEPTH, DEC_BATCH, DEC_SEQ, PLE_DIM), 1.0)

    t5_bias = nrm((N_BUCKETS, T5_HEADS), 0.5)
    norm_mix_pre = gain((DEPTH, D_MODEL))
    norm_mix_post = gain((DEPTH, D_MODEL))
    norm_ffn_pre = gain((DEPTH, D_MODEL))
    norm_ffn_post = gain((DEPTH, D_MODEL))
    even_w_in = nrm((N_EVEN, D_MODEL, EVEN_IN), D_MODEL ** -0.5)
    even_w_out = nrm((N_EVEN, 2 * GROUP_WIDTH, D_MODEL), (2 * GROUP_WIDTH) ** -0.5)
    lru_conv_w = nrm((N_EVEN, LRU_CONV, LRU_WIDTH), LRU_CONV ** -0.5)
    lru_conv_b = nrm((N_EVEN, LRU_WIDTH), 0.01)
    lru_w_a = nrm((N_EVEN, LRU_HEADS, LRU_BLOCK, LRU_BLOCK), LRU_BLOCK ** -0.5)
    lru_b_a = nrm((N_EVEN, LRU_WIDTH), 0.01)
    lru_w_x = nrm((N_EVEN, LRU_HEADS, LRU_BLOCK, LRU_BLOCK), LRU_BLOCK ** -0.5)
    lru_b_x = nrm((N_EVEN, LRU_WIDTH), 0.01)
    a_c = jax.random.uniform(next(ks), (N_EVEN, LRU_WIDTH), f32, 0.9, 0.999)
    s = a_c ** (1.0 / LRU_C)
    lru_lambda = jnp.log(s) - jnp.log1p(-s)
    odd_w_in = nrm((N_ODD, D_MODEL, ODD_IN), D_MODEL ** -0.5)
    odd_w_out = nrm((N_ODD, 2 * GROUP_WIDTH, D_MODEL), (2 * GROUP_WIDTH) ** -0.5)
    rwkv_mu = jax.random.uniform(next(ks), (N_ODD, RWKV_COLS), f32)
    rwkv_w0 = jax.random.uniform(next(ks), (N_ODD, GROUP_WIDTH), f32, -6.0, -1.0)
    rwkv_w2 = nrm((N_ODD, RWKV_DECAY_LORA, GROUP_WIDTH), 0.1)
    rwkv_a0 = nrm((N_ODD, GROUP_WIDTH), 0.1)
    rwkv_a2 = nrm((N_ODD, RWKV_AAA_LORA, GROUP_WIDTH), 0.1)
    rwkv_g2 = nrm((N_ODD, RWKV_GATE_LORA, GROUP_WIDTH), RWKV_GATE_LORA ** -0.5)
    rwkv_k_k = 0.85 + nrm((N_ODD, GROUP_WIDTH), 0.05)
    rwkv_k_a = gain((N_ODD, GROUP_WIDTH))
    rwkv_r_k = nrm((N_ODD, RWKV_HEADS, RWKV_HEAD_DIM), 0.1)
    rwkv_ln_w = gain((N_ODD, GROUP_WIDTH))
    rwkv_ln_b = nrm((N_ODD, GROUP_WIDTH), 0.01)
    diff_lambda = nrm((N_ODD, 4, DIFF_QK_DIM), 0.1)
    diff_subln_w = gain((N_ODD, DIFF_V_DIM))
    ffn_w_up = nrm((DEPTH, D_MODEL, 2 * D_FF), D_MODEL ** -0.5)
    ffn_conv_w = nrm((DEPTH, FFN_CONV, 2 * D_FF), FFN_CONV ** -0.5)
    ffn_conv_b = nrm((DEPTH, 2 * D_FF), 0.01)
    ffn_w_down = nrm((DEPTH, D_FF, D_MODEL), D_FF ** -0.5)
    ple_w_proj = nrm((DEPTH, PLE_DIM, D_MODEL), PLE_DIM ** -0.5)
    ple_w_gate = nrm((DEPTH, D_MODEL, D_MODEL), D_MODEL ** -0.5)
    return {
        'x_prompt': x_prompt, 'x_sample': x_sample,
        'cache_moba_k': cache_moba_k, 'cache_moba_v': cache_moba_v,
        'state_lru_conv': state_lru_conv, 'state_lru_h': state_lru_h,
        'cache_diff_k': cache_diff_k, 'cache_diff_v': cache_diff_v,
        'state_rwkv_shift': state_rwkv_shift, 'state_rwkv': state_rwkv,
        'state_ffn_conv': state_ffn_conv, 'page_table': page_table,
        'p_prompt': p_prompt, 'p_sample': p_sample,
        't5_bias': t5_bias,
        'norm_mix_pre': norm_mix_pre, 'norm_mix_post': norm_mix_post,
        'norm_ffn_pre': norm_ffn_pre, 'norm_ffn_post': norm_ffn_post,
        'even_w_in': even_w_in, 'even_w_out': even_w_out,
        'lru_conv_w': lru_conv_w, 'lru_conv_b': lru_conv_b,
        'lru_w_a': lru_w_a, 'lru_b_a': lru_b_a, 'lru_w_x': lru_w_x, 'lru_b_x': lru_b_x,
        'lru_lambda': lru_lambda,
        'odd_w_in': odd_w_in, 'odd_w_out': odd_w_out,
        'rwkv_mu': rwkv_mu, 'rwkv_w0': rwkv_w0, 'rwkv_w2': rwkv_w2, 'rwkv_a0': rwkv_a0,
        'rwkv_a2': rwkv_a2, 'rwkv_g2': rwkv_g2, 'rwkv_k_k': rwkv_k_k, 'rwkv_k_a': rwkv_k_a,
        'rwkv_r_k': rwkv_r_k, 'rwkv_ln_w': rwkv_ln_w, 'rwkv_ln_b': rwkv_ln_b,
        'diff_lambda': diff_lambda, 'diff_subln_w': diff_subln_w,
        'ffn_w_up': ffn_w_up, 'ffn_conv_w': ffn_conv_w, 'ffn_conv_b': ffn_conv_b, 'ffn_w_down': ffn_w_down,
        'ple_w_proj': ple_w_proj, 'ple_w_gate': ple_w_gate,
    }


def reference(x_prompt, x_sample, cache_moba_k, cache_moba_v, state_lru_conv, state_lru_h,
              cache_diff_k, cache_diff_v, state_rwkv_shift, state_rwkv, state_ffn_conv, page_table,
              p_prompt, p_sample, t5_bias, norm_mix_pre, norm_mix_post, norm_ffn_pre, norm_ffn_post,
              even_w_in, even_w_out, lru_conv_w, lru_conv_b, lru_w_a, lru_b_a, lru_w_x, lru_b_x, lru_lambda,
              odd_w_in, odd_w_out, rwkv_mu, rwkv_w0, rwkv_w2, rwkv_a0, rwkv_a2, rwkv_g2, rwkv_k_k, rwkv_k_a,
              rwkv_r_k, rwkv_ln_w, rwkv_ln_b, diff_lambda, diff_subln_w,
              ffn_w_up, ffn_conv_w, ffn_conv_b, ffn_w_down, ple_w_proj, ple_w_gate):
    W = dict(t5_bias=t5_bias, norm_mix_pre=norm_mix_pre, norm_mix_post=norm_mix_post,
             norm_ffn_pre=norm_ffn_pre, norm_ffn_post=norm_ffn_post,
             even_w_in=even_w_in, even_w_out=even_w_out, lru_conv_w=lru_conv_w, lru_conv_b=lru_conv_b,
             lru_w_a=lru_w_a, lru_b_a=lru_b_a, lru_w_x=lru_w_x, lru_b_x=lru_b_x, lru_lambda=lru_lambda,
             odd_w_in=odd_w_in, odd_w_out=odd_w_out, rwkv_mu=rwkv_mu, rwkv_w0=rwkv_w0, rwkv_w2=rwkv_w2,
             rwkv_a0=rwkv_a0, rwkv_a2=rwkv_a2, rwkv_g2=rwkv_g2, rwkv_k_k=rwkv_k_k, rwkv_k_a=rwkv_k_a,
             rwkv_r_k=rwkv_r_k, rwkv_ln_w=rwkv_ln_w, rwkv_ln_b=rwkv_ln_b,
             diff_lambda=diff_lambda, diff_subln_w=diff_subln_w,
             ffn_w_up=ffn_w_up, ffn_conv_w=ffn_conv_w, ffn_conv_b=ffn_conv_b, ffn_w_down=ffn_w_down,
             ple_w_proj=ple_w_proj, ple_w_gate=ple_w_gate)
    bp, tp = x_prompt.shape[:2]
    dt = x_prompt.dtype

    empty_moba = jnp.zeros((bp, 0, MOBA_HEADS, MOBA_HEAD_DIM), dt)
    empty_dk = jnp.zeros((bp, 0, DIFF_HEADS, 2, DIFF_QK_DIM), dt)
    empty_dv = jnp.zeros((bp, 0, DIFF_HEADS, DIFF_V_DIM), dt)
    (y_prompt, mk_p, mv_p, lru_conv_prompt, lru_h_prompt, dk_p, dv_p,
     rwkv_shift_prompt, rwkv_S_prompt, ffn_conv_prompt) = trunk(
        x_prompt, p_prompt,
        [(empty_moba, empty_moba) for _ in range(N_EVEN)],
        jnp.zeros((N_EVEN, bp, LRU_CONV - 1, LRU_WIDTH), dt),
        jnp.zeros((N_EVEN, bp, LRU_WIDTH), jnp.float32),
        [(empty_dk, empty_dv) for _ in range(N_ODD)],
        jnp.zeros((N_ODD, bp, RWKV_COLS), dt),
        jnp.zeros((N_ODD, bp, RWKV_HEADS, RWKV_HEAD_DIM, RWKV_HEAD_DIM), jnp.float32),
        jnp.zeros((DEPTH, bp, FFN_CONV - 1, 2 * D_FF), dt),
        W)

    (y_sample, moba_k_sample, moba_v_sample, lru_conv_sample, lru_h_sample, diff_k_sample, diff_v_sample,
     rwkv_shift_sample, rwkv_S_sample, ffn_conv_sample) = trunk(
        x_sample, p_sample,
        [(gather_pages(cache_moba_k[j], page_table), gather_pages(cache_moba_v[j], page_table)) for j in range(N_EVEN)],
        state_lru_conv, state_lru_h,
        [(gather_pages(cache_diff_k[j], page_table), gather_pages(cache_diff_v[j], page_table)) for j in range(N_ODD)],
        state_rwkv_shift, state_rwkv, state_ffn_conv,
        W)

    n_pp = tp // PAGE_SIZE
    moba_k_prompt = mk_p.reshape(N_EVEN, bp, n_pp, PAGE_SIZE, MOBA_HEADS, MOBA_HEAD_DIM)
    moba_v_prompt = mv_p.reshape(N_EVEN, bp, n_pp, PAGE_SIZE, MOBA_HEADS, MOBA_HEAD_DIM)
    diff_k_prompt = dk_p.reshape(N_ODD, bp, n_pp, PAGE_SIZE, DIFF_HEADS, 2, DIFF_QK_DIM)
    diff_v_prompt = dv_p.reshape(N_ODD, bp, n_pp, PAGE_SIZE, DIFF_HEADS, DIFF_V_DIM)
    return (y_prompt, y_sample,
            moba_k_prompt, moba_v_prompt, moba_k_sample, moba_v_sample,
            lru_conv_prompt, lru_conv_sample, lru_h_prompt, lru_h_sample,
            diff_k_prompt, diff_v_prompt, diff_k_sample, diff_v_sample,
            rwkv_shift_prompt, rwkv_shift_sample, rwkv_S_prompt, rwkv_S_sample,
            ffn_conv_prompt, ffn_conv_sample)
```

```python
import functools
import math

import numpy as np
import jax
import jax.numpy as jnp
from jax import lax
from jax.experimental import pallas as pl
from jax.experimental.pallas import tpu as pltpu

F32 = jnp.float32
BF16 = jnp.bfloat16
HI = lax.Precision.HIGHEST

D_MODEL = 1024
GROUP = 512
PAGE = 128
VMEM_LIMIT = 56 * 1024 * 1024


def _cp(*sem):
    return pltpu.CompilerParams(dimension_semantics=sem, vmem_limit_bytes=VMEM_LIMIT)


def _rms(x, g, eps):
    return x * lax.rsqrt(jnp.mean(x * x, axis=-1, keepdims=True) + eps) * g


def _gelu(x):
    return 0.5 * x * (1.0 + jnp.tanh(math.sqrt(2.0 / math.pi) * (x + 0.044715 * (x * x * x))))


def _sigmoid(x):
    return 1.0 / (1.0 + jnp.exp(-x))


def _dot(a, b):
    return jnp.dot(a, b, preferred_element_type=F32)


def _shift_rows(x, d, fill):
    r = pltpu.roll(x, d, 0)
    row = lax.broadcasted_iota(jnp.int32, x.shape, 0)
    for i in range(d):
        r = jnp.where(row == i, fill[i], r)
    return r


def _norm_linear_kernel(x_ref, g_ref, w_ref, *out_refs, splits):
    h = _rms(x_ref[...], g_ref[...], 1e-6).astype(BF16)
    off = 0
    for o_ref, n in zip(out_refs, splits):
        o_ref[...] = _dot(h, w_ref[:, off:off + n])
        off += n


def norm_linear(x, g, w, splits):
    n, d = x.shape
    m = w.shape[1]
    tm = min(n, 512)
    return pl.pallas_call(
        functools.partial(_norm_linear_kernel, splits=splits),
        grid=(n // tm,),
        in_specs=[pl.BlockSpec((tm, d), lambda i: (i, 0)),
                  pl.BlockSpec((1, d), lambda i: (0, 0)),
                  pl.BlockSpec((d, m), lambda i: (0, 0))],
        out_specs=[pl.BlockSpec((tm, s), lambda i: (i, 0)) for s in splits],
        out_shape=[jax.ShapeDtypeStruct((n, s), F32) for s in splits],
        compiler_params=_cp("arbitrary"),
        name="norm_linear",
    )(x, g.reshape(1, d), w.astype(BF16))


def _out_proj_kernel(a_ref, b_ref, w_ref, x_ref, g_ref, o_ref):
    y = _dot(a_ref[...].astype(BF16), w_ref[:GROUP, :]) + _dot(b_ref[...].astype(BF16), w_ref[GROUP:, :])
    o_ref[...] = x_ref[...] + _rms(y, g_ref[...], 1e-6)


def out_proj(a, b, w, x, g):
    n, d = x.shape
    tm = min(n, 512)
    return pl.pallas_call(
        _out_proj_kernel,
        grid=(n // tm,),
        in_specs=[pl.BlockSpec((tm, GROUP), lambda i: (i, 0)),
                  pl.BlockSpec((tm, GROUP), lambda i: (i, 0)),
                  pl.BlockSpec((2 * GROUP, d), lambda i: (0, 0)),
                  pl.BlockSpec((tm, d), lambda i: (i, 0)),
                  pl.BlockSpec((1, d), lambda i: (0, 0))],
        out_specs=pl.BlockSpec((tm, d), lambda i: (i, 0)),
        out_shape=jax.ShapeDtypeStruct((n, d), F32),
        compiler_params=_cp("arbitrary"),
        name="out_proj",
    )(a, b, w.astype(BF16), x, g.reshape(1, d))


def _ffn_tail(acc, x, gpost, p, wg_ref, wp_ref):
    x1 = x + _rms(acc, gpost, 1e-6)
    gate = _sigmoid(_dot(x1.astype(BF16), wg_ref[...]))
    return x1 + gate * _dot(p.astype(BF16), wp_ref[...])


def _ffn_kernel(x_ref, p_ref, gpre_ref, wug_ref, wuv_ref, cwg_ref, cwv_ref, cbg_ref, cbv_ref, wd_ref,
                gpost_ref, wg_ref, wp_ref, o_ref, st_ref, hn, acc, carry, *, tm, f, nf):
    t = pl.program_id(1)
    j = pl.program_id(2)

    @pl.when(j == 0)
    def _():
        hn[...] = _rms(x_ref[...], gpre_ref[...], 1e-6).astype(BF16)
        acc[...] = jnp.zeros_like(acc)

    @pl.when(t == 0)
    def _():
        carry[j] = jnp.zeros((8, 2 * f), F32)

    h = hn[...]
    ug = _dot(h, wug_ref[...])
    uv = _dot(h, wuv_ref[...])
    prev = carry[j]

    def conv(u, pv, cw_ref, cb_ref):
        cw = cw_ref[...]
        u1 = _shift_rows(u, 1, [pv[7:8]])
        u2 = _shift_rows(u, 2, [pv[6:7], pv[7:8]])
        return cb_ref[...] + cw[0:1] * u2 + cw[1:2] * u1 + cw[2:3] * u

    cg = conv(ug, prev[:, :f], cwg_ref, cbg_ref)
    cv = conv(uv, prev[:, f:], cwv_ref, cbv_ref)
    carry[j] = jnp.concatenate([ug[tm - 8:, :], uv[tm - 8:, :]], axis=1)
    st_ref[0, 0, 0, 0:1, :] = ug[tm - 2:tm - 1, :]
    st_ref[0, 0, 0, 1:2, :] = uv[tm - 2:tm - 1, :]
    st_ref[0, 0, 1, 0:1, :] = ug[tm - 1:tm, :]
    st_ref[0, 0, 1, 1:2, :] = uv[tm - 1:tm, :]
    act = (_gelu(cg) * cv).astype(BF16)
    acc[...] += _dot(act, wd_ref[...])

    @pl.when(j == nf - 1)
    def _():
        o_ref[...] = _ffn_tail(acc[...], x_ref[...], gpost_ref[...], p_ref[...], wg_ref, wp_ref)


def ffn_prompt(x, p, gpre, w_up, conv_w, conv_b, w_down, gpost, w_gate, w_proj, batch, tm=512, f=512):
    n, d = x.shape
    dff = w_down.shape[0]
    nf = dff // f
    nt = n // batch // tm
    pd = p.shape[1]
    w_up = w_up.astype(BF16)
    conv_b = conv_b.reshape(1, 2 * dff)
    row = lambda b, t, j: (b * nt + t, 0)
    const = lambda b, t, j: (0, 0)
    out, st = pl.pallas_call(
        functools.partial(_ffn_kernel, tm=tm, f=f, nf=nf),
        grid=(batch, nt, nf),
        in_specs=[pl.BlockSpec((tm, d), row),
                  pl.BlockSpec((tm, pd), row),
                  pl.BlockSpec((1, d), const),
                  pl.BlockSpec((d, f), lambda b, t, j: (0, j)),
                  pl.BlockSpec((d, f), lambda b, t, j: (0, nf + j)),
                  pl.BlockSpec((3, f), lambda b, t, j: (0, j)),
                  pl.BlockSpec((3, f), lambda b, t, j: (0, nf + j)),
                  pl.BlockSpec((1, f), lambda b, t, j: (0, j)),
                  pl.BlockSpec((1, f), lambda b, t, j: (0, nf + j)),
                  pl.BlockSpec((f, d), lambda b, t, j: (j, 0)),
                  pl.BlockSpec((1, d), const),
                  pl.BlockSpec((d, d), const),
                  pl.BlockSpec((pd, d), const)],
        out_specs=[pl.BlockSpec((tm, d), row),
                   pl.BlockSpec((1, 1, 2, 2, f), lambda b, t, j: (b, t, 0, 0, j))],
        out_shape=[jax.ShapeDtypeStruct((n, d), F32),
                   jax.ShapeDtypeStruct((batch, nt, 2, 2, dff), F32)],
        scratch_shapes=[pltpu.VMEM((tm, d), BF16), pltpu.VMEM((tm, d), F32), pltpu.VMEM((nf, 8, 2 * f), F32)],
        compiler_params=_cp("arbitrary", "arbitrary", "arbitrary"),
        name="ffn_prompt",
    )(x, p, gpre.reshape(1, d), w_up, w_up, conv_w, conv_w, conv_b, conv_b, w_down.astype(BF16),
      gpost.reshape(1, d), w_gate.astype(BF16), w_proj.astype(BF16))
    return out, st[:, nt - 1].reshape(batch, 2, 2 * dff)


def _ffn_step_kernel(x_ref, p_ref, gpre_ref, wug_ref, wuv_ref, cwg_ref, cwv_ref, cbg_ref, cbv_ref, wd_ref,
                     gpost_ref, wg_ref, wp_ref, s0g_ref, s0v_ref, s1g_ref, s1v_ref,
                     o_ref, ug_ref, uv_ref, acc, *, nf):
    j = pl.program_id(0)

    @pl.when(j == 0)
    def _():
        acc[...] = jnp.zeros_like(acc)

    h = _rms(x_ref[...], gpre_ref[...], 1e-6).astype(BF16)
    ug = _dot(h, wug_ref[...])
    uv = _dot(h, wuv_ref[...])
    ug_ref[...] = ug
    uv_ref[...] = uv
    cwg = cwg_ref[...]
    cwv = cwv_ref[...]
    cg = cbg_ref[...] + cwg[0:1] * s0g_ref[...] + cwg[1:2] * s1g_ref[...] + cwg[2:3] * ug
    cv = cbv_ref[...] + cwv[0:1] * s0v_ref[...] + cwv[1:2] * s1v_ref[...] + cwv[2:3] * uv
    acc[...] += _dot((_gelu(cg) * cv).astype(BF16), wd_ref[...])

    @pl.when(j == nf - 1)
    def _():
        o_ref[...] = _ffn_tail(acc[...], x_ref[...], gpost_ref[...], p_ref[...], wg_ref, wp_ref)


def ffn_step(x, p, gpre, w_up, conv_w, conv_b, w_down, gpost, w_gate, w_proj, state, f=512):
    n, d = x.shape
    dff = w_down.shape[0]
    nf = dff // f
    pd = p.shape[1]
    w_up = w_up.astype(BF16)
    conv_b = conv_b.reshape(1, 2 * dff)
    s0, s1 = state[:, 0, :], state[:, 1, :]
    const = lambda j: (0, 0)
    lo = lambda j: (0, j)
    hi = lambda j: (0, nf + j)
    out, ug, uv = pl.pallas_call(
        functools.partial(_ffn_step_kernel, nf=nf),
        grid=(nf,),
        in_specs=[pl.BlockSpec((n, d), const),
                  pl.BlockSpec((n, pd), const),
                  pl.BlockSpec((1, d), const),
                  pl.BlockSpec((d, f), lo), pl.BlockSpec((d, f), hi),
                  pl.BlockSpec((3, f), lo), pl.BlockSpec((3, f), hi),
                  pl.BlockSpec((1, f), lo), pl.BlockSpec((1, f), hi),
                  pl.BlockSpec((f, d), lambda j: (j, 0)),
                  pl.BlockSpec((1, d), const),
                  pl.BlockSpec((d, d), const),
                  pl.BlockSpec((pd, d), const),
                  pl.BlockSpec((n, f), lo), pl.BlockSpec((n, f), hi),
                  pl.BlockSpec((n, f), lo), pl.BlockSpec((n, f), hi)],
        out_specs=[pl.BlockSpec((n, d), const), pl.BlockSpec((n, f), lo), pl.BlockSpec((n, f), lo)],
        out_shape=[jax.ShapeDtypeStruct((n, d), F32), jax.ShapeDtypeStruct((n, dff), F32),
                   jax.ShapeDtypeStruct((n, dff), F32)],
        scratch_shapes=[pltpu.VMEM((n, d), F32)],
        compiler_params=_cp("arbitrary"),
        name="ffn_step",
    )(x, p, gpre.reshape(1, d), w_up, w_up, conv_w, conv_w, conv_b, conv_b, w_down.astype(BF16),
      gpost.reshape(1, d), w_gate.astype(BF16), w_proj.astype(BF16), s0, s0, s1, s1)
    new_state = jnp.stack([s1, jnp.concatenate([ug, uv], axis=1)], axis=1)
    return out, new_state


def _lru_gates(xc, wa_ref, ba_ref, wx_ref, bx_ref, lam_ref):
    xb = xc.astype(BF16)
    r = _sigmoid(_dot(xb, wa_ref[...]) + ba_ref[...])
    i = _sigmoid(_dot(xb, wx_ref[...]) + bx_ref[...])
    lam = lam_ref[...]
    softplus_neg = jnp.maximum(-lam, 0.0) + jnp.log1p(jnp.exp(-jnp.abs(lam)))
    log_a = -8.0 * softplus_neg * r
    a = jnp.exp(log_a)
    th = jnp.tanh(log_a)
    u = jnp.sqrt(-2.0 * th / (1.0 - th)) * (i * xc)
    return a, u


def _lru_kernel(gate_ref, x_ref, cw_ref, cb_ref, wa_ref, ba_ref, wx_ref, bx_ref, lam_ref,
                y_ref, conv_ref, hlast_ref, xcarry, hcarry, *, tl):
    t = pl.program_id(1)

    @pl.when(t == 0)
    def _():
        xcarry[...] = jnp.zeros_like(xcarry)
        hcarry[...] = jnp.zeros_like(hcarry)

    x = x_ref[...]
    c = xcarry[...]
    cw = cw_ref[...]
    xs1 = _shift_rows(x, 1, [c[7:8]])
    xs2 = _shift_rows(x, 2, [c[6:7], c[7:8]])
    xs3 = _shift_rows(x, 3, [c[5:6], c[6:7], c[7:8]])
    xc = cb_ref[...] + cw[0:1] * xs3 + cw[1:2] * xs2 + cw[2:3] * xs1 + cw[3:4] * x
    a, u = _lru_gates(xc, wa_ref, ba_ref, wx_ref, bx_ref, lam_ref)
    row = lax.broadcasted_iota(jnp.int32, a.shape, 0)
    d = 1
    while d < tl:
        a_s = jnp.where(row < d, 1.0, pltpu.roll(a, d, 0))
        u_s = jnp.where(row < d, 0.0, pltpu.roll(u, d, 0))
        u = u + a * u_s
        a = a * a_s
        d *= 2
    h = a * hcarry[...] + u
    y_ref[...] = h * _gelu(gate_ref[...])
    hcarry[...] = h[tl - 1:tl, :]
    hlast_ref[0] = h[tl - 1:tl, :]
    xcarry[...] = x[tl - 8:, :]
    conv_ref[0] = x[tl - 3:, :]


def _block_diag(w):
    h, a, b = w.shape
    eye = jnp.eye(h, dtype=w.dtype)
    return (eye[:, None, :, None] * w[:, :, None, :]).reshape(h * a, h * b)


def lru_prompt(gate, x, conv_w, conv_b, w_a, b_a, w_x, b_x, lam, batch, tl=256):
    n, c = x.shape
    nt = n // batch // tl
    row = lambda b, t: (b * nt + t, 0)
    const = lambda b, t: (0, 0)
    vec = pl.BlockSpec((1, c), const)
    y, conv, hlast = pl.pallas_call(
        functools.partial(_lru_kernel, tl=tl),
        grid=(batch, nt),
        in_specs=[pl.BlockSpec((tl, c), row), pl.BlockSpec((tl, c), row),
                  pl.BlockSpec((4, c), const), vec,
                  pl.BlockSpec((c, c), const), vec, pl.BlockSpec((c, c), const), vec, vec],
        out_specs=[pl.BlockSpec((tl, c), row),
                   pl.BlockSpec((1, 3, c), lambda b, t: (b, 0, 0)),
                   pl.BlockSpec((1, 1, c), lambda b, t: (b, 0, 0))],
        out_shape=[jax.ShapeDtypeStruct((n, c), F32), jax.ShapeDtypeStruct((batch, 3, c), F32),
                   jax.ShapeDtypeStruct((batch, 1, c), F32)],
        scratch_shapes=[pltpu.VMEM((8, c), F32), pltpu.VMEM((1, c), F32)],
        compiler_params=_cp("arbitrary", "arbitrary"),
        name="lru_prompt",
    )(gate, x, conv_w, conv_b.reshape(1, c), _block_diag(w_a).astype(BF16), b_a.reshape(1, c),
      _block_diag(w_x).astype(BF16), b_x.reshape(1, c), lam.reshape(1, c))
    return y, conv, hlast.reshape(batch, c)


def _lru_step_kernel(gate_ref, x_ref, s0_ref, s1_ref, s2_ref, h0_ref, cw_ref, cb_ref, wa_ref, ba_ref,
                     wx_ref, bx_ref, lam_ref, y_ref, h_ref):
    cw = cw_ref[...]
    x = x_ref[...]
    xc = cb_ref[...] + cw[0:1] * s0_ref[...] + cw[1:2] * s1_ref[...] + cw[2:3] * s2_ref[...] + cw[3:4] * x
    a, u = _lru_gates(xc, wa_ref, ba_ref, wx_ref, bx_ref, lam_ref)
    h = a * h0_ref[...] + u
    h_ref[...] = h
    y_ref[...] = h * _gelu(gate_ref[...])


def lru_step(gate, x, conv_state, h0, conv_w, conv_b, w_a, b_a, w_x, b_x, lam):
    n, c = x.shape
    y, h = pl.pallas_call(
        _lru_step_kernel,
        out_shape=[jax.ShapeDtypeStruct((n, c), F32), jax.ShapeDtypeStruct((n, c), F32)],
        name="lru_step",
    )(gate, x, conv_state[:, 0], conv_state[:, 1], conv_state[:, 2], h0, conv_w, conv_b.reshape(1, c),
      _block_diag(w_a).astype(BF16), b_a.reshape(1, c), _block_diag(w_x).astype(BF16), b_x.reshape(1, c),
      lam.reshape(1, c))
    new_conv = jnp.stack([conv_state[:, 1], conv_state[:, 2], x], axis=1)
    return y, new_conv, h


N_BUCKETS = 32
T5_MAX_EXACT = 16
T5_MAX_DISTANCE = 128
NEG = -1e30
ATT_BLOCK = 256
MOBA_TOPK = 3


def _t5_bucket_np(rel):
    n = np.maximum(rel, 0)
    nf = np.maximum(n, 1).astype(np.float32)
    large = T5_MAX_EXACT + (np.log(nf / np.float32(T5_MAX_EXACT)) / np.float32(math.log(T5_MAX_DISTANCE / T5_MAX_EXACT))
                            * np.float32(N_BUCKETS - T5_MAX_EXACT)).astype(np.int32)
    large = np.minimum(large, N_BUCKETS - 1)
    return np.where(n < T5_MAX_EXACT, n, large).astype(np.int32)


def _prompt_bucket_table():
    r = np.arange(ATT_BLOCK)[:, None]
    c = np.arange(ATT_BLOCK)[None, :]
    tabs = []
    for o in range(2):
        rel = o * ATT_BLOCK + r - c
        tabs.append(np.where(rel >= 0, _t5_bucket_np(rel), -1))
    return np.stack(tabs).astype(np.int32)


def _bias_from_buckets(bucket, t5_ref, col):
    out = jnp.full(bucket.shape, NEG, F32)
    for b in range(N_BUCKETS):
        out = jnp.where(bucket == b, t5_ref[b, col], out)
    return out


def _attn_kernel(*refs, mode, nq, lam_init):
    if mode == "moba":
        (t5_ref, q_ref, k_ref, v_ref, bk_ref, o_ref, kb, vb, bias, kmean) = refs
    else:
        (t5_ref, lam_ref, q_ref, k_ref, v_ref, bk_ref, sub_ref, o_ref, kb, vb, bias) = refs
    g = pl.program_id(1)
    qi = pl.program_id(2)
    blk = ATT_BLOCK

    @pl.when(qi == 0)
    def _():
        def cast(j, c):
            rows = pl.ds(pl.multiple_of(j * blk, blk), blk)
            kf = k_ref[rows, :]
            kb[rows, :] = kf.astype(BF16)
            vb[rows, :] = v_ref[rows, :].astype(BF16)
            if mode == "moba":
                kmean[pl.ds(j, 1), :] = jnp.sum(kf, axis=0, keepdims=True) * (1.0 / blk)
            return c
        lax.fori_loop(0, nq, cast, 0)
        for m in range(2):
            col = 2 * g + m
            bias[m, 0] = _bias_from_buckets(bk_ref[0], t5_ref, col)
            bias[m, 1] = _bias_from_buckets(bk_ref[1], t5_ref, col)
            bias[m, 2] = jnp.full((blk, blk), t5_ref[N_BUCKETS - 1, col], F32)

    q = q_ref[...]
    lane = lax.broadcasted_iota(jnp.int32, q.shape, 1)
    half = lane >= 64
    qm = [jnp.where(half, 0.0, q), jnp.where(half, q, 0.0)]

    if mode == "moba":
        nb = kmean.shape[0]
        bi = lax.broadcasted_iota(jnp.int32, (blk, nb), 1)
        sels = []
        for m in range(2):
            gate = lax.dot_general(qm[m], kmean[...], (((1,), (1,)), ((), ())), precision=HI,
                                   preferred_element_type=F32)
            gt = jnp.where(bi < qi, gate, -jnp.inf)
            sel = jnp.zeros((blk, nb), F32)
            for _ in range(MOBA_TOPK):
                mx = jnp.max(gt, axis=1, keepdims=True)
                cand = jnp.where((gt == mx) & (mx > -jnp.inf), bi, nb)
                first = jnp.min(cand, axis=1, keepdims=True)
                pick = bi == first
                sel = jnp.where(pick, 1.0, sel)
                gt = jnp.where(pick, -jnp.inf, gt)
            sels.append(sel)

    qs = [(x * 0.125).astype(BF16) for x in qm]

    def body(d, carry):
        j = qi - d
        rows = pl.ds(pl.multiple_of(j * blk, blk), blk)
        kblk = kb[rows, :]
        vblk = vb[rows, :]
        dd = jnp.minimum(d, 2)
        out = []
        for m in range(2):
            mi, li, acc = carry[m]
            s = lax.dot_general(qs[m], kblk, (((1,), (1,)), ((), ())), preferred_element_type=F32)
            s = s + bias[m, dd]
            if mode == "moba":
                hit = jnp.max(jnp.where(bi == j, sels[m], 0.0), axis=1, keepdims=True)
                s = jnp.where((hit > 0.0) | (d == 0), s, NEG)
            mn = jnp.maximum(mi, jnp.max(s, axis=1, keepdims=True))
            alpha = jnp.exp(mi - mn)
            p = jnp.exp(s - mn)
            ln = alpha * li + jnp.sum(p, axis=1, keepdims=True)
            acc = alpha * acc + _dot(p.astype(BF16), vblk)
            out.append((mn, ln, acc))
        return tuple(out)

    init = tuple((jnp.full((blk, 1), NEG, F32), jnp.zeros((blk, 1), F32), jnp.zeros((blk, 128), F32))
                 for _ in range(2))
    res = lax.fori_loop(0, qi + 1, body, init)
    o0 = res[0][2] / res[0][1]
    o1 = res[1][2] / res[1][1]
    if mode == "moba":
        o_ref[...] = jnp.where(half, o1, o0)
    else:
        att = o0 - lam_ref[0] * o1
        o_ref[...] = _rms(att, sub_ref[...], 1e-5) * (1.0 - lam_init)


def attn_prompt(q, k, v, t5_bias, batch, mode, lam=None, subln_w=None, lam_init=0.0):
    n, c = q.shape
    t = n // batch
    nq = t // ATT_BLOCK
    ng = c // 128
    blk = ATT_BLOCK
    smem = pl.BlockSpec(memory_space=pltpu.SMEM)
    qspec = pl.BlockSpec((blk, 128), lambda b, g, i: (b * nq + i, g))
    kvspec = pl.BlockSpec((t, 128), lambda b, g, i: (b, g))
    bkspec = pl.BlockSpec((2, blk, blk), lambda b, g, i: (0, 0, 0))
    scratch = [pltpu.VMEM((t, 128), BF16), pltpu.VMEM((t, 128), BF16), pltpu.VMEM((2, 3, blk, blk), F32)]
    bk = jnp.asarray(_prompt_bucket_table())
    if mode == "moba":
        in_specs = [smem, qspec, kvspec, kvspec, bkspec]
        args = (t5_bias, q, k, v, bk)
        scratch.append(pltpu.VMEM((nq, 128), F32))
    else:
        in_specs = [smem, smem, qspec, kvspec, kvspec, bkspec, pl.BlockSpec((1, 128), lambda b, g, i: (0, 0))]
        args = (t5_bias, lam.reshape(1), q, k, v, bk, subln_w.reshape(1, 128))
    return pl.pallas_call(
        functools.partial(_attn_kernel, mode=mode, nq=nq, lam_init=lam_init),
        grid=(batch, ng, nq),
        in_specs=in_specs,
        out_specs=qspec,
        out_shape=jax.ShapeDtypeStruct((n, c), F32),
        scratch_shapes=scratch,
        compiler_params=_cp("arbitrary", "arbitrary", "arbitrary"),
        name="attn_" + mode,
    )(*args)


RWKV_COLS = 1792
RWKV_LN_EPS = 64e-5


def _seg_ones(n):
    r = lax.broadcasted_iota(jnp.int32, (n, n), 0) // 64
    c = lax.broadcasted_iota(jnp.int32, (n, n), 1) // 64
    return jnp.where(r == c, 1.0, 0.0).astype(F32)


def _dot_hi(a, b):
    return jnp.dot(a, b, precision=HI, preferred_element_type=F32)


def _rwkv_prep_math(z, z_prev, mu_ref, w0_ref, w2_ref, a0_ref, a2_ref, g2_ref, kk_ref, ka_ref, outs):
    r_ref, e_ref, kkn_ref, ab_ref, k2_ref, v_ref, gate_ref = outs
    g = GROUP
    zs = z + mu_ref[...] * (z_prev - z)
    k = zs[:, g:2 * g]
    lora = zs[:, 3 * g:3 * g + 128]
    wlin = w0_ref[...] + _dot(jnp.tanh(lora).astype(BF16), w2_ref[...])
    softplus_neg = jnp.maximum(-wlin, 0.0) + jnp.log1p(jnp.exp(-jnp.abs(wlin)))
    a = _sigmoid(a0_ref[...] + _dot(lora.astype(BF16), a2_ref[...]))
    kk = k * kk_ref[...]
    norm = jnp.sqrt(_dot_hi(kk * kk, _seg_ones(g)))
    kk = kk / jnp.maximum(norm, 1e-12)
    r_ref[...] = zs[:, :g]
    e_ref[...] = jnp.exp(-softplus_neg - 0.5)
    kkn_ref[...] = kk
    ab_ref[...] = kk * a
    k2_ref[...] = k * (1.0 + (a - 1.0) * ka_ref[...])
    v_ref[...] = zs[:, 2 * g:3 * g]
    gate_ref[...] = _dot(_sigmoid(zs[:, 3 * g + 128:]).astype(BF16), g2_ref[...])


def _rwkv_prep_kernel(z_ref, mu_ref, w0_ref, w2_ref, a0_ref, a2_ref, g2_ref, kk_ref, ka_ref, *rest):
    outs, zcarry = rest[:7], rest[7]
    t = pl.program_id(1)

    @pl.when(t == 0)
    def _():
        zcarry[...] = jnp.zeros_like(zcarry)

    z = z_ref[...]
    z_prev = _shift_rows(z, 1, [zcarry[7:8, :]])
    zcarry[...] = z[z.shape[0] - 8:, :]
    _rwkv_prep_math(z, z_prev, mu_ref, w0_ref, w2_ref, a0_ref, a2_ref, g2_ref, kk_ref, ka_ref, outs)


def _rwkv_prep_step_kernel(z_ref, zp_ref, mu_ref, w0_ref, w2_ref, a0_ref, a2_ref, g2_ref, kk_ref, ka_ref, *outs):
    _rwkv_prep_math(z_ref[...], zp_ref[...], mu_ref, w0_ref, w2_ref, a0_ref, a2_ref, g2_ref, kk_ref, ka_ref, outs)


def _rwkv_prep_params(mu, w0, w2, a0, a2, g2, k_k, k_a):
    g = GROUP
    zero = jnp.zeros_like(w2)
    return (mu.reshape(1, RWKV_COLS), w0.reshape(1, g), jnp.concatenate([w2, zero], 0).astype(BF16),
            a0.reshape(1, g), jnp.concatenate([zero, a2], 0).astype(BF16), g2.astype(BF16),
            k_k.reshape(1, g), k_a.reshape(1, g))


def rwkv_prep(z, batch, params, shift=None, tl=256):
    n = z.shape[0]
    g = GROUP
    out_shape = [jax.ShapeDtypeStruct((n, g), F32)] * 7
    if shift is not None:
        return pl.pallas_call(_rwkv_prep_step_kernel, out_shape=out_shape, name="rwkv_prep_step")(z, shift, *params)
    nt = n // batch // tl
    row = lambda b, t: (b * nt + t, 0)
    const = lambda b, t: (0, 0)
    vec = pl.BlockSpec((1, g), const)
    lora = pl.BlockSpec((128, g), const)
    return pl.pallas_call(
        _rwkv_prep_kernel,
        grid=(batch, nt),
        in_specs=[pl.BlockSpec((tl, RWKV_COLS), row), pl.BlockSpec((1, RWKV_COLS), const),
                  vec, lora, vec, lora, lora, vec, vec],
        out_specs=[pl.BlockSpec((tl, g), row)] * 7,
        out_shape=out_shape,
        scratch_shapes=[pltpu.VMEM((8, RWKV_COLS), F32)],
        compiler_params=_cp("arbitrary", "arbitrary"),
        name="rwkv_prep",
    )(z, *params)


def _nt(a, b):
    return lax.dot_general(a, b, (((1,), (1,)), ((), ())), precision=HI, preferred_element_type=F32)


def _tn(a, b):
    return lax.dot_general(a, b, (((0,), (0,)), ((), ())), precision=HI, preferred_element_type=F32)


def _rwkv_chunk_pair(r, e, kk, ab, k2, v, c):
    shape = (c, 128)
    half = lax.broadcasted_iota(jnp.int32, shape, 1) >= 64
    row = lax.broadcasted_iota(jnp.int32, shape, 0)
    cum = e
    d = 1
    while d < c:
        cum = cum + jnp.where(row < d, 0.0, pltpu.roll(cum, d, 0))
        d *= 2
    g_inv = jnp.exp(cum)
    at = -kk * jnp.exp(e - cum)
    bt = ab * g_inv
    kt = k2 * g_inv
    rt = r * jnp.exp(-cum)
    g_end = jnp.exp(-cum[c - 1:c, :])
    ri = lax.broadcasted_iota(jnp.int32, (c, c), 0)
    ci = lax.broadcasted_iota(jnp.int32, (c, c), 1)
    per_head = []
    for h in range(2):
        mask = half if h else jnp.logical_not(half)
        ah = jnp.where(mask, at, 0.0)
        rh = jnp.where(mask, rt, 0.0)
        lab = jnp.where(ri > ci, _nt(ah, bt), 0.0)
        lak = jnp.where(ri > ci, _nt(ah, kt), 0.0)
        rb = jnp.where(ri >= ci, _nt(rh, bt), 0.0)
        rk = jnp.where(ri >= ci, _nt(rh, kt), 0.0)
        x = jnp.where(ri == ci, 1.0, 0.0) + lab
        p = lab
        n = 2
        while n < c:
            p = _dot_hi(p, p)
            x = x + _dot_hi(x, p)
            n *= 2
        pa = _dot_hi(x, at)
        q = _dot_hi(x, _dot_hi(lak, v))
        per_head.append((pa, q, _dot_hi(rb, pa), _dot_hi(rb, q) + _dot_hi(rk, v)))
    pa, q, y1, y0 = (jnp.where(half, per_head[1][i], per_head[0][i]) for i in range(4))
    y1 = rt + y1
    bg = bt * g_end
    kg = kt * g_end
    r2 = lax.broadcasted_iota(jnp.int32, (128, 128), 0)
    c2 = lax.broadcasted_iota(jnp.int32, (128, 128), 1)
    same_head = (r2 >= 64) == (c2 >= 64)
    m = jnp.where(r2 == c2, g_end, 0.0) + jnp.where(same_head, _tn(bg, pa), 0.0)
    nn = jnp.where(same_head, _tn(bg, q) + _tn(kg, v), 0.0)
    return m, nn, y1, y0


def _rwkv_chunk_kernel(r_ref, e_ref, kk_ref, ab_ref, k2_ref, v_ref, m_ref, n_ref, y1_ref, y0_ref, *, c):
    for g in range(GROUP // 128):
        ls = slice(g * 128, (g + 1) * 128)
        m, nn, y1, y0 = _rwkv_chunk_pair(r_ref[:, ls], e_ref[:, ls], kk_ref[:, ls], ab_ref[:, ls],
                                         k2_ref[:, ls], v_ref[:, ls], c)
        m_ref[0, g] = m
        n_ref[0, g] = nn
        y1_ref[:, ls] = y1
        y0_ref[:, ls] = y0


def _rwkv_scan_kernel(m_ref, n_ref, y1_ref, y0_ref, r_ref, k2_ref, v_ref, gate_ref, h0_ref,
                      lnw_ref, lnb_ref, rk_ref, y_ref, hout_ref, h):
    t = pl.program_id(1)

    @pl.when(t == 0)
    def _():
        h[...] = h0_ref[0]

    ones = _seg_ones(128)
    for g in range(GROUP // 128):
        ls = slice(g * 128, (g + 1) * 128)
        hg = h[g]
        y = _dot_hi(y1_ref[:, ls], hg) + y0_ref[:, ls]
        h[g] = _dot_hi(m_ref[0, g], hg) + n_ref[0, g]
        mean = _dot_hi(y, ones) * (1.0 / 64)
        yc = y - mean
        var = _dot_hi(yc * yc, ones) * (1.0 / 64)
        yn = yc * lax.rsqrt(var + RWKV_LN_EPS) * lnw_ref[:, ls] + lnb_ref[:, ls]
        v = v_ref[:, ls]
        bonus = _dot_hi(r_ref[:, ls] * k2_ref[:, ls] * rk_ref[:, ls], ones) * v
        y_ref[:, ls] = (yn + bonus) * gate_ref[:, ls]
    hout_ref[0] = h[...]


def rwkv_scan(r, e, kk, ab, k2, v, gate, s0, ln_w, ln_b, r_k, batch, c):
    n, g = r.shape
    nc = n // batch // c
    npair = g // 128
    row = lambda b, t: (b * nc + t, 0)
    blk = pl.BlockSpec((c, g), row)
    mat = pl.BlockSpec((1, npair, 128, 128), lambda b, t: (b * nc + t, 0, 0, 0))
    m, nn, y1, y0 = pl.pallas_call(
        functools.partial(_rwkv_chunk_kernel, c=c),
        grid=(batch, nc),
        in_specs=[blk] * 6,
        out_specs=[mat, mat, blk, blk],
        out_shape=[jax.ShapeDtypeStruct((batch * nc, npair, 128, 128), F32)] * 2
        + [jax.ShapeDtypeStruct((n, g), F32)] * 2,
        compiler_params=_cp("arbitrary", "arbitrary"),
        name="rwkv_chunk",
    )(r, e, kk, ab, k2, v)
    st = jnp.swapaxes(s0, -1, -2).reshape(batch, npair, 2, 64, 64)
    eye2 = jnp.eye(2, dtype=F32)
    h0 = (st[:, :, :, :, None, :] * eye2[None, None, :, None, :, None]).reshape(batch, npair, 128, 128)
    state = pl.BlockSpec((1, npair, 128, 128), lambda b, t: (b, 0, 0, 0))
    vec = pl.BlockSpec((1, g), lambda b, t: (0, 0))
    y, hout = pl.pallas_call(
        _rwkv_scan_kernel,
        grid=(batch, nc),
        in_specs=[mat, mat, blk, blk, blk, blk, blk, blk, state, vec, vec, vec],
        out_specs=[blk, state],
        out_shape=[jax.ShapeDtypeStruct((n, g), F32), jax.ShapeDtypeStruct((batch, npair, 128, 128), F32)],
        scratch_shapes=[pltpu.VMEM((npair, 128, 128), F32)],
        compiler_params=_cp("arbitrary", "arbitrary"),
        name="rwkv_scan",
    )(m, nn, y1, y0, r, k2, v, gate, h0, ln_w.reshape(1, g), ln_b.reshape(1, g), r_k.reshape(1, g))
    hb = hout.reshape(batch, npair, 2, 64, 2, 64)
    s_last = jnp.stack([hb[:, :, 0, :, 0, :], hb[:, :, 1, :, 1, :]], axis=2).reshape(batch, 2 * npair, 64, 64)
    return y, jnp.swapaxes(s_last, -1, -2)


PAGES_PER_STEP = 16


def _head_rows(q, n_maps):
    width = q.shape[1]
    r = lax.broadcasted_iota(jnp.int32, (n_maps, width), 0)
    c = lax.broadcasted_iota(jnp.int32, (n_maps, width), 1) // 64
    return jnp.where(r == c, q, 0.0)


def _past_bucket_row(first_pos, width, q_pos):
    rel = q_pos - (first_pos + np.arange(width))
    return _t5_bucket_np(rel).astype(np.int32).reshape(1, width)


def _bias_rows(bucket_row, t5t_ref):
    out = jnp.zeros((8, bucket_row.shape[1]), F32)
    for b in range(N_BUCKETS):
        out = jnp.where(bucket_row == b, t5t_ref[:, b:b + 1], out)
    return out


def _moba_scan_kernel(pt_ref, q_ref, *refs, n_steps):
    pages = refs[:PAGES_PER_STEP]
    lg_ref, idx_ref, ksum = refs[PAGES_PER_STEP:]
    s = pl.program_id(1)
    qh = _head_rows(q_ref[0] * 0.125, 8)
    for i, pg in enumerate(pages):
        kp = pg[0]
        lg = _nt(qh, kp)
        blk, half = divmod(i, 2)
        lg_ref[0, :, blk, half * PAGE:(half + 1) * PAGE] = lg
        psum = jnp.sum(kp, axis=0, keepdims=True)
        row = pl.ds(s * (PAGES_PER_STEP // 2) + blk, 1)
        if half == 0:
            ksum[row, :] = psum
        else:
            ksum[row, :] = ksum[row, :] + psum

    @pl.when(s == n_steps - 1)
    def _():
        nb = ksum.shape[0]
        gate = _nt(_head_rows(q_ref[0], 8), ksum[...] * (1.0 / ATT_BLOCK))
        bi = lax.broadcasted_iota(jnp.int32, (8, nb), 1)
        lane = lax.broadcasted_iota(jnp.int32, (8, 128), 1)
        out = jnp.zeros((8, 128), jnp.int32)
        for j in range(MOBA_TOPK):
            mx = jnp.max(gate, axis=1, keepdims=True)
            first = jnp.min(jnp.where(gate == mx, bi, nb), axis=1, keepdims=True)
            out = jnp.where(lane == j, first, out)
            gate = jnp.where(bi == first, -jnp.inf, gate)
        idx_ref[0] = out


def _moba_gather_kernel(idx_ref, pt_ref, t5_ref, q_ref, kn_ref, vn_ref, lg_ref, bk_ref, *refs, n_blocks):
    vpages = refs[:4 * MOBA_TOPK]
    o_ref = refs[4 * MOBA_TOPK]
    b = pl.program_id(0)
    g = pl.program_id(1)
    q = q_ref[0]
    kn = kn_ref[0]
    vn = vn_ref[0]
    lane = lax.broadcasted_iota(jnp.int32, (1, 128), 1)
    outs = []
    for hh in range(2):
        h = 2 * g + hh
        mask = (lane >= 64) if hh else (lane < 64)
        own = jnp.sum(jnp.where(mask, q * kn, 0.0), axis=1, keepdims=True) * 0.125 + t5_ref[0, h]
        far = t5_ref[N_BUCKETS - 1, h]
        near = far
        for bkt in range(N_BUCKETS):
            near = jnp.where(bk_ref[...] == bkt, t5_ref[bkt, h], near)
        logits = []
        for s in range(MOBA_TOPK):
            blk = idx_ref[b, h * MOBA_TOPK + s]
            lg = lg_ref[0, h, pl.ds(blk, 1), :]
            logits.append(lg + jnp.where(blk == n_blocks - 1, near, far))
        mx = own
        for lg in logits:
            mx = jnp.maximum(mx, jnp.max(lg, axis=1, keepdims=True))
        p_own = jnp.exp(own - mx)
        den = p_own
        acc = p_own * vn
        for s, lg in enumerate(logits):
            p = jnp.exp(lg - mx)
            den = den + jnp.sum(p, axis=1, keepdims=True)
            for pg in range(2):
                vp = vpages[(hh * MOBA_TOPK + s) * 2 + pg][0]
                acc = acc + _dot_hi(p[:, pg * PAGE:(pg + 1) * PAGE], vp)
        outs.append(acc / den)
    o_ref[0] = jnp.where(lane >= 64, outs[1], outs[0])


def moba_step(q, k_new, v_new, cache_k, cache_v, page_table, t5_bias):
    nb, c = q.shape
    n_pages = page_table.shape[1]
    n_steps = n_pages // PAGES_PER_STEP
    n_blocks = n_pages * PAGE // ATT_BLOCK
    bps = PAGES_PER_STEP // 2
    q3 = q.reshape(nb, 1, c)

    def page_spec(i):
        return pl.BlockSpec((1, PAGE, c), lambda b, s, pt: (pt[b, s * PAGES_PER_STEP + i], 0, 0))

    logits, idx = pl.pallas_call(
        functools.partial(_moba_scan_kernel, n_steps=n_steps),
        grid_spec=pltpu.PrefetchScalarGridSpec(
            num_scalar_prefetch=1,
            grid=(nb, n_steps),
            in_specs=[pl.BlockSpec((1, 1, c), lambda b, s, pt: (b, 0, 0))]
            + [page_spec(i) for i in range(PAGES_PER_STEP)],
            out_specs=[pl.BlockSpec((1, 8, bps, ATT_BLOCK), lambda b, s, pt: (b, 0, s, 0)),
                       pl.BlockSpec((1, 8, 128), lambda b, s, pt: (b, 0, 0))],
            scratch_shapes=[pltpu.VMEM((n_blocks, c), F32)]),
        out_shape=[jax.ShapeDtypeStruct((nb, 8, n_blocks, ATT_BLOCK), F32),
                   jax.ShapeDtypeStruct((nb, 8, 128), jnp.int32)],
        compiler_params=_cp("arbitrary", "arbitrary"),
        name="moba_scan",
    )(page_table, q3, *([cache_k] * PAGES_PER_STEP))
    sel = idx[:, :, :MOBA_TOPK].reshape(nb, 8 * MOBA_TOPK)
    bucket = jnp.asarray(_past_bucket_row((n_blocks - 1) * ATT_BLOCK, ATT_BLOCK, n_pages * PAGE))

    def vpage_spec(hh, s, pg):
        def imap(b, g, sel, pt):
            return (pt[b, 2 * sel[b, (2 * g + hh) * MOBA_TOPK + s] + pg], 0, g)
        return pl.BlockSpec((1, PAGE, 128), imap)

    row = pl.BlockSpec((1, 1, 128), lambda b, g, sel, pt: (b, 0, g))
    out = pl.pallas_call(
        functools.partial(_moba_gather_kernel, n_blocks=n_blocks),
        grid_spec=pltpu.PrefetchScalarGridSpec(
            num_scalar_prefetch=2,
            grid=(nb, c // 128),
            in_specs=[pl.BlockSpec(memory_space=pltpu.SMEM), row, row, row,
                      pl.BlockSpec((1, 8, n_blocks, ATT_BLOCK), lambda b, g, sel, pt: (b, 0, 0, 0)),
                      pl.BlockSpec((1, ATT_BLOCK), lambda b, g, sel, pt: (0, 0))]
            + [vpage_spec(hh, s, pg) for hh in range(2) for s in range(MOBA_TOPK) for pg in range(2)],
            out_specs=row),
        out_shape=jax.ShapeDtypeStruct((nb, 1, c), F32),
        compiler_params=_cp("arbitrary", "arbitrary"),
        name="moba_gather",
    )(sel, page_table, t5_bias, q3, k_new.reshape(nb, 1, c), v_new.reshape(nb, 1, c), logits, bucket,
      *([cache_v] * (4 * MOBA_TOPK)))
    return out.reshape(nb, c)


def _diff_step_kernel(pt_ref, lam_ref, q_ref, kn_ref, vn_ref, t5t_ref, bk_ref, sub_ref, *refs, n_steps, lam_init):
    kpages = refs[:PAGES_PER_STEP]
    vpages = refs[PAGES_PER_STEP:2 * PAGES_PER_STEP]
    o_ref, m_s, l_s, acc_s = refs[2 * PAGES_PER_STEP:]
    s = pl.program_id(1)
    qh = _head_rows(q_ref[0] * 0.125, 8)

    @pl.when(s == 0)
    def _():
        own = jnp.sum(qh * kn_ref[0], axis=1, keepdims=True) + t5t_ref[:, 0:1]
        m_s[...] = own
        l_s[...] = jnp.ones_like(l_s)
        acc_s[...] = jnp.broadcast_to(vn_ref[0], acc_s.shape)

    far = t5t_ref[:, N_BUCKETS - 1:N_BUCKETS]
    near = _bias_rows(bk_ref[...], t5t_ref)
    m, l, acc = m_s[...], l_s[...], acc_s[...]
    for i in range(PAGES_PER_STEP):
        lg = _nt(qh, kpages[i][0])
        if i == PAGES_PER_STEP - 1:
            lg = lg + jnp.where(s == n_steps - 1, near, far)
        else:
            lg = lg + far
        mn = jnp.maximum(m, jnp.max(lg, axis=1, keepdims=True))
        alpha = jnp.exp(m - mn)
        p = jnp.exp(lg - mn)
        l = alpha * l + jnp.sum(p, axis=1, keepdims=True)
        acc = alpha * acc + _dot(p.astype(BF16), vpages[i][0].astype(BF16))
        m = mn
    m_s[...], l_s[...], acc_s[...] = m, l, acc

    @pl.when(s == n_steps - 1)
    def _():
        a = acc / l
        heads = []
        for h in range(4):
            ls = slice(h * 128, (h + 1) * 128)
            att = a[2 * h:2 * h + 1, ls] - lam_ref[0] * a[2 * h + 1:2 * h + 2, ls]
            heads.append(_rms(att, sub_ref[...], 1e-5) * (1.0 - lam_init))
        o_ref[0] = jnp.concatenate(heads, axis=1)


def diff_step(q, k_new, v_new, cache_k, cache_v, page_table, t5_bias, lam, subln_w, lam_init):
    nb, c = q.shape
    n_pages = page_table.shape[1]
    n_steps = n_pages // PAGES_PER_STEP
    bucket = jnp.asarray(_past_bucket_row((n_pages - 1) * PAGE, PAGE, n_pages * PAGE))

    def page_spec(i):
        return pl.BlockSpec((1, PAGE, c), lambda b, s, pt: (pt[b, s * PAGES_PER_STEP + i], 0, 0))

    row = pl.BlockSpec((1, 1, c), lambda b, s, pt: (b, 0, 0))
    out = pl.pallas_call(
        functools.partial(_diff_step_kernel, n_steps=n_steps, lam_init=lam_init),
        grid_spec=pltpu.PrefetchScalarGridSpec(
            num_scalar_prefetch=1,
            grid=(nb, n_steps),
            in_specs=[pl.BlockSpec(memory_space=pltpu.SMEM), row, row, row,
                      pl.BlockSpec((8, N_BUCKETS), lambda b, s, pt: (0, 0)),
                      pl.BlockSpec((1, PAGE), lambda b, s, pt: (0, 0)),
                      pl.BlockSpec((1, 128), lambda b, s, pt: (0, 0))]
            + [page_spec(i) for i in range(PAGES_PER_STEP)] * 2,
            out_specs=row,
            scratch_shapes=[pltpu.VMEM((8, 1), F32), pltpu.VMEM((8, 1), F32), pltpu.VMEM((8, c), F32)]),
        out_shape=jax.ShapeDtypeStruct((nb, 1, c), F32),
        compiler_params=_cp("arbitrary", "arbitrary"),
        name="diff_step",
    )(page_table, lam.reshape(1), q.reshape(nb, 1, c), k_new.reshape(nb, 1, c), v_new.reshape(nb, 1, c),
      t5_bias.T, bucket, subln_w.reshape(1, 128), *([cache_k] * PAGES_PER_STEP), *([cache_v] * PAGES_PER_STEP))
    return out.reshape(nb, c)


def rwkv_mix(z, shift0, s0, mu, w0, w2, a0, a2, g2, k_k, k_a, r_k, ln_w, ln_b, batch, step):
    params = _rwkv_prep_params(mu, w0, w2, a0, a2, g2, k_k, k_a)
    if not step:
        outs = rwkv_prep(z, batch, params)
        r, e, kk, ab, k2, v, gate = outs
        return rwkv_scan(r, e, kk, ab, k2, v, gate, s0, ln_w, ln_b, r_k, batch, 64)
    outs = rwkv_prep(z, batch, params, shift=shift0)
    r, e, kk, ab, k2, v, gate = (jnp.pad(o[:, None, :], ((0, 0), (0, 7), (0, 0))).reshape(batch * 8, GROUP)
                                 for o in outs)
    y, s_last = rwkv_scan(r, e, kk, ab, k2, v, gate, s0, ln_w, ln_b, r_k, batch, 8)
    return y.reshape(batch, 8, GROUP)[:, 0], s_last


def _trunk(x, p, batch, step, st, W):
    depth = p.shape[0]
    outs = {k: [] for k in ("moba_k", "moba_v", "lru_conv", "lru_h", "diff_k", "diff_v",
                            "rwkv_shift", "rwkv_s", "ffn_conv")}
    t = x.shape[0] // batch
    for i in range(depth):
        j = i // 2
        if i % 2 == 0:
            gate, xin, q, k, v = norm_linear(x, W["norm_mix_pre"][i], W["even_w_in"][j], (GROUP,) * 5)
            lru_w = (W["lru_conv_w"][j], W["lru_conv_b"][j], W["lru_w_a"][j], W["lru_b_a"][j],
                     W["lru_w_x"][j], W["lru_b_x"][j], W["lru_lambda"][j])
            if step:
                ya, conv, h_last = lru_step(gate, xin, st["lru_conv"][j], st["lru_h"][j], *lru_w)
                yb = moba_step(q, k, v, st["moba_k"][j], st["moba_v"][j], st["page_table"], W["t5_bias"])
            else:
                ya, conv, h_last = lru_prompt(gate, xin, *lru_w, batch)
                yb = attn_prompt(q, k, v, W["t5_bias"], batch, "moba")
            outs["moba_k"].append(k)
            outs["moba_v"].append(v)
            outs["lru_conv"].append(conv)
            outs["lru_h"].append(h_last)
            w_out = W["even_w_out"][j]
        else:
            z, q, k, v = norm_linear(x, W["norm_mix_pre"][i], W["odd_w_in"][j], (RWKV_COLS, GROUP, GROUP, GROUP))
            rw = (W["rwkv_mu"][j], W["rwkv_w0"][j], W["rwkv_w2"][j], W["rwkv_a0"][j], W["rwkv_a2"][j],
                  W["rwkv_g2"][j], W["rwkv_k_k"][j], W["rwkv_k_a"][j], W["rwkv_r_k"][j],
                  W["rwkv_ln_w"][j], W["rwkv_ln_b"][j])
            lam_init = 0.8 - 0.6 * math.exp(-0.3 * i)
            lf = W["diff_lambda"][j]
            lam = jnp.exp(jnp.sum(lf[0] * lf[1])) - jnp.exp(jnp.sum(lf[2] * lf[3])) + lam_init
            if step:
                ya, s_last = rwkv_mix(z, st["rwkv_shift"][j], st["rwkv_s"][j], *rw, batch, True)
                yb = diff_step(q, k, v, st["diff_k"][j], st["diff_v"][j], st["page_table"], W["t5_bias"],
                               lam, W["diff_subln_w"][j], lam_init)
                shift = z
            else:
                s0 = jnp.zeros((batch, 8, 64, 64), F32)
                ya, s_last = rwkv_mix(z, None, s0, *rw, batch, False)
                yb = attn_prompt(q, k, v, W["t5_bias"], batch, "diff", lam=lam, subln_w=W["diff_subln_w"][j],
                                 lam_init=lam_init)
                shift = z.reshape(batch, t, RWKV_COLS)[:, t - 1]
            outs["diff_k"].append(k)
            outs["diff_v"].append(v)
            outs["rwkv_shift"].append(shift)
            outs["rwkv_s"].append(s_last)
            w_out = W["odd_w_out"][j]
        x = out_proj(ya, yb, w_out, x, W["norm_mix_post"][i])
        ffn_w = (W["norm_ffn_pre"][i], W["ffn_w_up"][i], W["ffn_conv_w"][i], W["ffn_conv_b"][i], W["ffn_w_down"][i],
                 W["norm_ffn_post"][i], W["ple_w_gate"][i], W["ple_w_proj"][i])
        if step:
            x, fbuf = ffn_step(x, p[i], *ffn_w, st["ffn_conv"][i])
        else:
            x, fbuf = ffn_prompt(x, p[i], *ffn_w, batch)
        outs["ffn_conv"].append(fbuf)
    return x, {k: jnp.stack(v) for k, v in outs.items()}


def kernel(x_prompt, x_sample, cache_moba_k, cache_moba_v, state_lru_conv, state_lru_h, cache_diff_k, cache_diff_v, state_rwkv_shift, state_rwkv, state_ffn_conv, page_table, p_prompt, p_sample, t5_bias, norm_mix_pre, norm_mix_post, norm_ffn_pre, norm_ffn_post, even_w_in, even_w_out, lru_conv_w, lru_conv_b, lru_w_a, lru_b_a, lru_w_x, lru_b_x, lru_lambda, odd_w_in, odd_w_out, rwkv_mu, rwkv_w0, rwkv_w2, rwkv_a0, rwkv_a2, rwkv_g2, rwkv_k_k, rwkv_k_a, rwkv_r_k, rwkv_ln_w, rwkv_ln_b, diff_lambda, diff_subln_w, ffn_w_up, ffn_conv_w, ffn_conv_b, ffn_w_down, ple_w_proj, ple_w_gate):
    W = dict(t5_bias=t5_bias, norm_mix_pre=norm_mix_pre, norm_mix_post=norm_mix_post,
             norm_ffn_pre=norm_ffn_pre, norm_ffn_post=norm_ffn_post,
             even_w_in=even_w_in, even_w_out=even_w_out, lru_conv_w=lru_conv_w, lru_conv_b=lru_conv_b,
             lru_w_a=lru_w_a, lru_b_a=lru_b_a, lru_w_x=lru_w_x, lru_b_x=lru_b_x, lru_lambda=lru_lambda,
             odd_w_in=odd_w_in, odd_w_out=odd_w_out, rwkv_mu=rwkv_mu, rwkv_w0=rwkv_w0, rwkv_w2=rwkv_w2,
             rwkv_a0=rwkv_a0, rwkv_a2=rwkv_a2, rwkv_g2=rwkv_g2, rwkv_k_k=rwkv_k_k, rwkv_k_a=rwkv_k_a,
             rwkv_r_k=rwkv_r_k, rwkv_ln_w=rwkv_ln_w, rwkv_ln_b=rwkv_ln_b,
             diff_lambda=diff_lambda, diff_subln_w=diff_subln_w,
             ffn_w_up=ffn_w_up, ffn_conv_w=ffn_conv_w, ffn_conv_b=ffn_conv_b, ffn_w_down=ffn_w_down,
             ple_w_proj=ple_w_proj, ple_w_gate=ple_w_gate)
    bp, tp, d = x_prompt.shape
    bs, ts, _ = x_sample.shape
    depth = p_prompt.shape[0]
    n_pp = tp // PAGE
    assert ts == 1, "the sample group is a single-token step"

    yp, P = _trunk(x_prompt.reshape(bp * tp, d), p_prompt.reshape(depth, bp * tp, -1), bp, False, None, W)

    pool = cache_moba_k.shape[1]
    st = dict(moba_k=cache_moba_k.reshape(-1, pool, PAGE, GROUP), moba_v=cache_moba_v.reshape(-1, pool, PAGE, GROUP),
              diff_k=cache_diff_k.reshape(-1, pool, PAGE, GROUP), diff_v=cache_diff_v.reshape(-1, pool, PAGE, GROUP),
              lru_conv=state_lru_conv, lru_h=state_lru_h, rwkv_shift=state_rwkv_shift, rwkv_s=state_rwkv,
              ffn_conv=state_ffn_conv, page_table=page_table)
    ys, S = _trunk(x_sample.reshape(bs * ts, d), p_sample.reshape(depth, bs * ts, -1), bs, True, st, W)

    ne, no = P["moba_k"].shape[0], P["diff_k"].shape[0]
    return (yp.reshape(bp, tp, d), ys.reshape(bs, ts, d),
            P["moba_k"].reshape(ne, bp, n_pp, PAGE, 8, 64), P["moba_v"].reshape(ne, bp, n_pp, PAGE, 8, 64),
            S["moba_k"].reshape(ne, bs, ts, 8, 64), S["moba_v"].reshape(ne, bs, ts, 8, 64),
            P["lru_conv"], S["lru_conv"], P["lru_h"], S["lru_h"],
            P["diff_k"].reshape(no, bp, n_pp, PAGE, 4, 2, 64), P["diff_v"].reshape(no, bp, n_pp, PAGE, 4, 128),
            S["diff_k"].reshape(no, bs, ts, 4, 2, 64), S["diff_v"].reshape(no, bs, ts, 4, 128),
            P["rwkv_shift"], S["rwkv_shift"], P["rwkv_s"], S["rwkv_s"],
            P["ffn_conv"], S["ffn_conv"])
```

```python
import functools
import math

import numpy as np
import jax
import jax.numpy as jnp
from jax import lax
from jax.experimental import pallas as pl
from jax.experimental.pallas import tpu as pltpu

F32 = jnp.float32
BF16 = jnp.bfloat16
HI = lax.Precision.HIGHEST

D_MODEL = 1024
GROUP = 512
PAGE = 128
VMEM_LIMIT = 56 * 1024 * 1024


def _cp(*sem):
    return pltpu.CompilerParams(dimension_semantics=sem, vmem_limit_bytes=VMEM_LIMIT)


def _rms(x, g, eps):
    return x * lax.rsqrt(jnp.mean(x * x, axis=-1, keepdims=True) + eps) * g


def _gelu(x):
    return 0.5 * x * (1.0 + jnp.tanh(math.sqrt(2.0 / math.pi) * (x + 0.044715 * (x * x * x))))


def _sigmoid(x):
    return 1.0 / (1.0 + jnp.exp(-x))


def _dot(a, b):
    return jnp.dot(a, b, preferred_element_type=F32)


def _shift_rows(x, d, fill):
    r = pltpu.roll(x, d, 0)
    row = lax.broadcasted_iota(jnp.int32, x.shape, 0)
    for i in range(d):
        r = jnp.where(row == i, fill[i], r)
    return r


def _norm_linear_kernel(x_ref, g_ref, w_ref, *out_refs, splits):
    h = _rms(x_ref[...], g_ref[...], 1e-6).astype(BF16)
    off = 0
    for o_ref, n in zip(out_refs, splits):
        o_ref[...] = _dot(h, w_ref[:, off:off + n])
        off += n


def norm_linear(x, g, w, splits):
    n, d = x.shape
    m = w.shape[1]
    tm = min(n, 512)
    return pl.pallas_call(
        functools.partial(_norm_linear_kernel, splits=splits),
        grid=(n // tm,),
        in_specs=[pl.BlockSpec((tm, d), lambda i: (i, 0)),
                  pl.BlockSpec((1, d), lambda i: (0, 0)),
                  pl.BlockSpec((d, m), lambda i: (0, 0))],
        out_specs=[pl.BlockSpec((tm, s), lambda i: (i, 0)) for s in splits],
        out_shape=[jax.ShapeDtypeStruct((n, s), F32) for s in splits],
        compiler_params=_cp("arbitrary"),
        name="norm_linear",
    )(x, g.reshape(1, d), w.astype(BF16))


def _out_proj_kernel(a_ref, b_ref, w_ref, x_ref, g_ref, o_ref):
    y = _dot(a_ref[...].astype(BF16), w_ref[:GROUP, :]) + _dot(b_ref[...].astype(BF16), w_ref[GROUP:, :])
    o_ref[...] = x_ref[...] + _rms(y, g_ref[...], 1e-6)


def out_proj(a, b, w, x, g):
    n, d = x.shape
    tm = min(n, 512)
    return pl.pallas_call(
        _out_proj_kernel,
        grid=(n // tm,),
        in_specs=[pl.BlockSpec((tm, GROUP), lambda i: (i, 0)),
                  pl.BlockSpec((tm, GROUP), lambda i: (i, 0)),
                  pl.BlockSpec((2 * GROUP, d), lambda i: (0, 0)),
                  pl.BlockSpec((tm, d), lambda i: (i, 0)),
                  pl.BlockSpec((1, d), lambda i: (0, 0))],
        out_specs=pl.BlockSpec((tm, d), lambda i: (i, 0)),
        out_shape=jax.ShapeDtypeStruct((n, d), F32),
        compiler_params=_cp("arbitrary"),
        name="out_proj",
    )(a, b, w.astype(BF16), x, g.reshape(1, d))


def _ffn_tail(acc, x, gpost, p, wg_ref, wp_ref):
    x1 = x + _rms(acc, gpost, 1e-6)
    gate = _sigmoid(_dot(x1.astype(BF16), wg_ref[...]))
    return x1 + gate * _dot(p.astype(BF16), wp_ref[...])


def _ffn_kernel(x_ref, p_ref, gpre_ref, wug_ref, wuv_ref, cwg_ref, cwv_ref, cbg_ref, cbv_ref, wd_ref,
                gpost_ref, wg_ref, wp_ref, o_ref, st_ref, hn, acc, carry, *, tm, f, nf):
    t = pl.program_id(1)
    j = pl.program_id(2)

    @pl.when(j == 0)
    def _():
        hn[...] = _rms(x_ref[...], gpre_ref[...], 1e-6).astype(BF16)
        acc[...] = jnp.zeros_like(acc)

    @pl.when(t == 0)
    def _():
        carry[j] = jnp.zeros((8, 2 * f), F32)

    h = hn[...]
    ug = _dot(h, wug_ref[...])
    uv = _dot(h, wuv_ref[...])
    prev = carry[j]

    def conv(u, pv, cw_ref, cb_ref):
        cw = cw_ref[...]
        u1 = _shift_rows(u, 1, [pv[7:8]])
        u2 = _shift_rows(u, 2, [pv[6:7], pv[7:8]])
        return cb_ref[...] + cw[0:1] * u2 + cw[1:2] * u1 + cw[2:3] * u

    cg = conv(ug, prev[:, :f], cwg_ref, cbg_ref)
    cv = conv(uv, prev[:, f:], cwv_ref, cbv_ref)
    carry[j] = jnp.concatenate([ug[tm - 8:, :], uv[tm - 8:, :]], axis=1)
    st_ref[0, 0, 0, 0:1, :] = ug[tm - 2:tm - 1, :]
    st_ref[0, 0, 0, 1:2, :] = uv[tm - 2:tm - 1, :]
    st_ref[0, 0, 1, 0:1, :] = ug[tm - 1:tm, :]
    st_ref[0, 0, 1, 1:2, :] = uv[tm - 1:tm, :]
    act = (_gelu(cg) * cv).astype(BF16)
    acc[...] += _dot(act, wd_ref[...])

    @pl.when(j == nf - 1)
    def _():
        o_ref[...] = _ffn_tail(acc[...], x_ref[...], gpost_ref[...], p_ref[...], wg_ref, wp_ref)


def ffn_prompt(x, p, gpre, w_up, conv_w, conv_b, w_down, gpost, w_gate, w_proj, batch, tm=512, f=512):
    n, d = x.shape
    dff = w_down.shape[0]
    nf = dff // f
    nt = n // batch // tm
    pd = p.shape[1]
    w_up = w_up.astype(BF16)
    conv_b = conv_b.reshape(1, 2 * dff)
    row = lambda b, t, j: (b * nt + t, 0)
    const = lambda b, t, j: (0, 0)
    out, st = pl.pallas_call(
        functools.partial(_ffn_kernel, tm=tm, f=f, nf=nf),
        grid=(batch, nt, nf),
        in_specs=[pl.BlockSpec((tm, d), row),
                  pl.BlockSpec((tm, pd), row),
                  pl.BlockSpec((1, d), const),
                  pl.BlockSpec((d, f), lambda b, t, j: (0, j)),
                  pl.BlockSpec((d, f), lambda b, t, j: (0, nf + j)),
                  pl.BlockSpec((3, f), lambda b, t, j: (0, j)),
                  pl.BlockSpec((3, f), lambda b, t, j: (0, nf + j)),
                  pl.BlockSpec((1, f), lambda b, t, j: (0, j)),
                  pl.BlockSpec((1, f), lambda b, t, j: (0, nf + j)),
                  pl.BlockSpec((f, d), lambda b, t, j: (j, 0)),
                  pl.BlockSpec((1, d), const),
                  pl.BlockSpec((d, d), const),
                  pl.BlockSpec((pd, d), const)],
        out_specs=[pl.BlockSpec((tm, d), row),
                   pl.BlockSpec((1, 1, 2, 2, f), lambda b, t, j: (b, t, 0, 0, j))],
        out_shape=[jax.ShapeDtypeStruct((n, d), F32),
                   jax.ShapeDtypeStruct((batch, nt, 2, 2, dff), F32)],
        scratch_shapes=[pltpu.VMEM((tm, d), BF16), pltpu.VMEM((tm, d), F32), pltpu.VMEM((nf, 8, 2 * f), F32)],
        compiler_params=_cp("arbitrary", "arbitrary", "arbitrary"),
        name="ffn_prompt",
    )(x, p, gpre.reshape(1, d), w_up, w_up, conv_w, conv_w, conv_b, conv_b, w_down.astype(BF16),
      gpost.reshape(1, d), w_gate.astype(BF16), w_proj.astype(BF16))
    return out, st[:, nt - 1].reshape(batch, 2, 2 * dff)


def _ffn_step_kernel(x_ref, p_ref, gpre_ref, wug_ref, wuv_ref, cwg_ref, cwv_ref, cbg_ref, cbv_ref, wd_ref,
                     gpost_ref, wg_ref, wp_ref, s0g_ref, s0v_ref, s1g_ref, s1v_ref,
                     o_ref, ug_ref, uv_ref, acc, *, nf):
    j = pl.program_id(0)

    @pl.when(j == 0)
    def _():
        acc[...] = jnp.zeros_like(acc)

    h = _rms(x_ref[...], gpre_ref[...], 1e-6).astype(BF16)
    ug = _dot(h, wug_ref[...])
    uv = _dot(h, wuv_ref[...])
    ug_ref[...] = ug
    uv_ref[...] = uv
    cwg = cwg_ref[...]
    cwv = cwv_ref[...]
    cg = cbg_ref[...] + cwg[0:1] * s0g_ref[...] + cwg[1:2] * s1g_ref[...] + cwg[2:3] * ug
    cv = cbv_ref[...] + cwv[0:1] * s0v_ref[...] + cwv[1:2] * s1v_ref[...] + cwv[2:3] * uv
    acc[...] += _dot((_gelu(cg) * cv).astype(BF16), wd_ref[...])

    @pl.when(j == nf - 1)
    def _():
        o_ref[...] = _ffn_tail(acc[...], x_ref[...], gpost_ref[...], p_ref[...], wg_ref, wp_ref)


def ffn_step(x, p, gpre, w_up, conv_w, conv_b, w_down, gpost, w_gate, w_proj, state, f=512):
    n, d = x.shape
    dff = w_down.shape[0]
    nf = dff // f
    pd = p.shape[1]
    w_up = w_up.astype(BF16)
    conv_b = conv_b.reshape(1, 2 * dff)
    s0, s1 = state[:, 0, :], state[:, 1, :]
    const = lambda j: (0, 0)
    lo = lambda j: (0, j)
    hi = lambda j: (0, nf + j)
    out, ug, uv = pl.pallas_call(
        functools.partial(_ffn_step_kernel, nf=nf),
        grid=(nf,),
        in_specs=[pl.BlockSpec((n, d), const),
                  pl.BlockSpec((n, pd), const),
                  pl.BlockSpec((1, d), const),
                  pl.BlockSpec((d, f), lo), pl.BlockSpec((d, f), hi),
                  pl.BlockSpec((3, f), lo), pl.BlockSpec((3, f), hi),
                  pl.BlockSpec((1, f), lo), pl.BlockSpec((1, f), hi),
                  pl.BlockSpec((f, d), lambda j: (j, 0)),
                  pl.BlockSpec((1, d), const),
                  pl.BlockSpec((d, d), const),
                  pl.BlockSpec((pd, d), const),
                  pl.BlockSpec((n, f), lo), pl.BlockSpec((n, f), hi),
                  pl.BlockSpec((n, f), lo), pl.BlockSpec((n, f), hi)],
        out_specs=[pl.BlockSpec((n, d), const), pl.BlockSpec((n, f), lo), pl.BlockSpec((n, f), lo)],
        out_shape=[jax.ShapeDtypeStruct((n, d), F32), jax.ShapeDtypeStruct((n, dff), F32),
                   jax.ShapeDtypeStruct((n, dff), F32)],
        scratch_shapes=[pltpu.VMEM((n, d), F32)],
        compiler_params=_cp("arbitrary"),
        name="ffn_step",
    )(x, p, gpre.reshape(1, d), w_up, w_up, conv_w, conv_w, conv_b, conv_b, w_down.astype(BF16),
      gpost.reshape(1, d), w_gate.astype(BF16), w_proj.astype(BF16), s0, s0, s1, s1)
    new_state = jnp.stack([s1, jnp.concatenate([ug, uv], axis=1)], axis=1)
    return out, new_state


def _lru_gates(xc, wa_ref, ba_ref, wx_ref, bx_ref, lam_ref):
    xb = xc.astype(BF16)
    r = _sigmoid(_dot(xb, wa_ref[...]) + ba_ref[...])
    i = _sigmoid(_dot(xb, wx_ref[...]) + bx_ref[...])
    lam = lam_ref[...]
    softplus_neg = jnp.maximum(-lam, 0.0) + jnp.log1p(jnp.exp(-jnp.abs(lam)))
    log_a = -8.0 * softplus_neg * r
    a = jnp.exp(log_a)
    th = jnp.tanh(log_a)
    u = jnp.sqrt(-2.0 * th / (1.0 - th)) * (i * xc)
    return a, u


def _lru_kernel(gate_ref, x_ref, cw_ref, cb_ref, wa_ref, ba_ref, wx_ref, bx_ref, lam_ref,
                y_ref, conv_ref, hlast_ref, xcarry, hcarry, *, tl):
    t = pl.program_id(1)

    @pl.when(t == 0)
    def _():
        xcarry[...] = jnp.zeros_like(xcarry)
        hcarry[...] = jnp.zeros_like(hcarry)

    x = x_ref[...]
    c = xcarry[...]
    cw = cw_ref[...]
    xs1 = _shift_rows(x, 1, [c[7:8]])
    xs2 = _shift_rows(x, 2, [c[6:7], c[7:8]])
    xs3 = _shift_rows(x, 3, [c[5:6], c[6:7], c[7:8]])
    xc = cb_ref[...] + cw[0:1] * xs3 + cw[1:2] * xs2 + cw[2:3] * xs1 + cw[3:4] * x
    a, u = _lru_gates(xc, wa_ref, ba_ref, wx_ref, bx_ref, lam_ref)
    row = lax.broadcasted_iota(jnp.int32, a.shape, 0)
    d = 1
    while d < tl:
        a_s = jnp.where(row < d, 1.0, pltpu.roll(a, d, 0))
        u_s = jnp.where(row < d, 0.0, pltpu.roll(u, d, 0))
        u = u + a * u_s
        a = a * a_s
        d *= 2
    h = a * hcarry[...] + u
    y_ref[...] = h * _gelu(gate_ref[...])
    hcarry[...] = h[tl - 1:tl, :]
    hlast_ref[0] = h[tl - 1:tl, :]
    xcarry[...] = x[tl - 8:, :]
    conv_ref[0] = x[tl - 3:, :]


def _block_diag(w):
    h, a, b = w.shape
    eye = jnp.eye(h, dtype=w.dtype)
    return (eye[:, None, :, None] * w[:, :, None, :]).reshape(h * a, h * b)


def lru_prompt(gate, x, conv_w, conv_b, w_a, b_a, w_x, b_x, lam, batch, tl=256):
    n, c = x.shape
    nt = n // batch // tl
    row = lambda b, t: (b * nt + t, 0)
    const = lambda b, t: (0, 0)
    vec = pl.BlockSpec((1, c), const)
    y, conv, hlast = pl.pallas_call(
        functools.partial(_lru_kernel, tl=tl),
        grid=(batch, nt),
        in_specs=[pl.BlockSpec((tl, c), row), pl.BlockSpec((tl, c), row),
                  pl.BlockSpec((4, c), const), vec,
                  pl.BlockSpec((c, c), const), vec, pl.BlockSpec((c, c), const), vec, vec],
        out_specs=[pl.BlockSpec((tl, c), row),
                   pl.BlockSpec((1, 3, c), lambda b, t: (b, 0, 0)),
                   pl.BlockSpec((1, 1, c), lambda b, t: (b, 0, 0))],
        out_shape=[jax.ShapeDtypeStruct((n, c), F32), jax.ShapeDtypeStruct((batch, 3, c), F32),
                   jax.ShapeDtypeStruct((batch, 1, c), F32)],
        scratch_shapes=[pltpu.VMEM((8, c), F32), pltpu.VMEM((1, c), F32)],
        compiler_params=_cp("arbitrary", "arbitrary"),
        name="lru_prompt",
    )(gate, x, conv_w, conv_b.reshape(1, c), _block_diag(w_a).astype(BF16), b_a.reshape(1, c),
      _block_diag(w_x).astype(BF16), b_x.reshape(1, c), lam.reshape(1, c))
    return y, conv, hlast.reshape(batch, c)


def _lru_step_kernel(gate_ref, x_ref, s0_ref, s1_ref, s2_ref, h0_ref, cw_ref, cb_ref, wa_ref, ba_ref,
                     wx_ref, bx_ref, lam_ref, y_ref, h_ref):
    cw = cw_ref[...]
    x = x_ref[...]
    xc = cb_ref[...] + cw[0:1] * s0_ref[...] + cw[1:2] * s1_ref[...] + cw[2:3] * s2_ref[...] + cw[3:4] * x
    a, u = _lru_gates(xc, wa_ref, ba_ref, wx_ref, bx_ref, lam_ref)
    h = a * h0_ref[...] + u
    h_ref[...] = h
    y_ref[...] = h * _gelu(gate_ref[...])


def lru_step(gate, x, conv_state, h0, conv_w, conv_b, w_a, b_a, w_x, b_x, lam):
    n, c = x.shape
    y, h = pl.pallas_call(
        _lru_step_kernel,
        out_shape=[jax.ShapeDtypeStruct((n, c), F32), jax.ShapeDtypeStruct((n, c), F32)],
        name="lru_step",
    )(gate, x, conv_state[:, 0], conv_state[:, 1], conv_state[:, 2], h0, conv_w, conv_b.reshape(1, c),
      _block_diag(w_a).astype(BF16), b_a.reshape(1, c), _block_diag(w_x).astype(BF16), b_x.reshape(1, c),
      lam.reshape(1, c))
    new_conv = jnp.stack([conv_state[:, 1], conv_state[:, 2], x], axis=1)
    return y, new_conv, h


N_BUCKETS = 32
T5_MAX_EXACT = 16
T5_MAX_DISTANCE = 128
NEG = -1e30
LOG2E = math.log2(math.e)
ATT_BLOCK = 256
MOBA_TOPK = 3


def _t5_bucket_np(rel):
    n = np.maximum(rel, 0)
    nf = np.maximum(n, 1).astype(np.float32)
    large = T5_MAX_EXACT + (np.log(nf / np.float32(T5_MAX_EXACT)) / np.float32(math.log(T5_MAX_DISTANCE / T5_MAX_EXACT))
                            * np.float32(N_BUCKETS - T5_MAX_EXACT)).astype(np.int32)
    large = np.minimum(large, N_BUCKETS - 1)
    return np.where(n < T5_MAX_EXACT, n, large).astype(np.int32)


def _prompt_bucket_table():
    r = np.arange(ATT_BLOCK)[:, None]
    c = np.arange(ATT_BLOCK)[None, :]
    tabs = []
    for o in range(2):
        rel = o * ATT_BLOCK + r - c
        tabs.append(np.where(rel >= 0, _t5_bucket_np(rel), -1))
    return np.stack(tabs).astype(np.int32)


def _bias_from_buckets(bucket, t5_ref, col):
    out = jnp.full(bucket.shape, NEG, F32)
    for b in range(N_BUCKETS):
        out = jnp.where(bucket == b, t5_ref[b, col], out)
    return out


def _attn_kernel(*refs, mode, nq, lam_init):
    if mode == "moba":
        (t5_ref, q_ref, k_ref, v_ref, bk_ref, o_ref, kb, vb, bias, qs_s, m_s, l_s, acc_s, kmean, sel_s) = refs
    else:
        (t5_ref, lam_ref, q_ref, k_ref, v_ref, bk_ref, sub_ref, o_ref, kb, vb, bias, qs_s, m_s, l_s, acc_s) = refs
    g = pl.program_id(1)
    qi = pl.program_id(2)
    blk = ATT_BLOCK

    @pl.when(qi == 0)
    def _():
        def cast(j, c):
            rows = pl.ds(pl.multiple_of(j * blk, blk), blk)
            kf = k_ref[rows, :]
            kb[rows, :] = kf.astype(BF16)
            vb[rows, :] = v_ref[rows, :].astype(BF16)
            if mode == "moba":
                kmean[pl.ds(j, 1), :] = jnp.sum(kf, axis=0, keepdims=True) * (1.0 / blk)
            return c
        lax.fori_loop(0, nq, cast, 0)
        for m in range(2):
            col = 2 * g + m
            bias[m, 0] = _bias_from_buckets(bk_ref[0], t5_ref, col) * LOG2E
            bias[m, 1] = _bias_from_buckets(bk_ref[1], t5_ref, col) * LOG2E
            bias[m, 2] = jnp.full((blk, blk), t5_ref[N_BUCKETS - 1, col] * LOG2E, F32)

    lane = lax.broadcasted_iota(jnp.int32, (blk, 128), 1)
    upper = lane >= 64
    for h2 in range(2):
        q = q_ref[h2 * blk:(h2 + 1) * blk, :]
        qb = 2 * qi + h2
        for m in range(2):
            c = 2 * h2 + m
            qm = jnp.where(upper, q, 0.0) if m else jnp.where(upper, 0.0, q)
            qs_s[c] = (qm * (0.125 * LOG2E)).astype(BF16)
            m_s[c] = jnp.full((blk, 128), NEG, F32)
            l_s[c] = jnp.zeros((blk, 128), F32)
            acc_s[c] = jnp.zeros((blk, 128), F32)
            if mode == "moba":
                nb = kmean.shape[0]
                gate = lax.dot_general(qm, kmean[...], (((1,), (1,)), ((), ())), precision=HI,
                                       preferred_element_type=F32)
                bi = lax.broadcasted_iota(jnp.int32, (blk, nb), 1)
                gt = jnp.where(bi < qb, gate, -jnp.inf)
                sel = jnp.zeros((blk, nb), F32)
                for _ in range(MOBA_TOPK):
                    mx = jnp.max(gt, axis=1, keepdims=True)
                    cand = jnp.where((gt == mx) & (mx > -jnp.inf), bi, nb)
                    first = jnp.min(cand, axis=1, keepdims=True)
                    pick = bi == first
                    sel = jnp.where(pick, 1.0, sel)
                    gt = jnp.where(pick, -jnp.inf, gt)
                sel_s[c] = jnp.zeros((blk, 128), BF16)
                sel_s[c, :, 0:nb] = sel.astype(BF16)

    def chain(c, j, dd, masked):
        m = c % 2
        rows = pl.ds(pl.multiple_of(j * blk, blk), blk)
        s = lax.dot_general(qs_s[c], kb[rows, :], (((1,), (1,)), ((), ())), preferred_element_type=F32)
        s = s + bias[m, dd]
        if masked:
            oh = jnp.where(lax.broadcasted_iota(jnp.int32, (128, 128), 0) == j, 1.0, 0.0).astype(BF16)
            col = (_dot(sel_s[c], oh) - 1.0) * (-NEG)
            s = s + jnp.concatenate([col, col], axis=1)
        mp = m_s[c]
        mn = jnp.maximum(mp, jnp.max(s, axis=1, keepdims=True))
        alpha = jnp.exp2(mp - mn)
        p = jnp.exp2(s - jnp.concatenate([mn, mn], axis=1))
        l_s[c] = alpha * l_s[c] + jnp.sum(p, axis=1, keepdims=True)
        acc_s[c] = alpha * acc_s[c] + _dot(p.astype(BF16), vb[rows, :])
        m_s[c] = mn

    for m in range(2):
        chain(2 + m, 2 * qi + 1, 0, False)

    masked = mode == "moba"
    for m in range(2):
        chain(m, 2 * qi, 0, False)
        chain(2 + m, 2 * qi, 1, masked)

    def body(d, carry):
        for c in range(4):
            chain(c, 2 * qi - d, jnp.minimum(d + c // 2, 2), masked)
        return carry
    lax.fori_loop(1, 2 * qi + 1, body, 0)

    for h2 in range(2):
        o0 = acc_s[2 * h2] / l_s[2 * h2]
        o1 = acc_s[2 * h2 + 1] / l_s[2 * h2 + 1]
        if mode == "moba":
            out = jnp.where(upper, o1, o0)
        else:
            att = o0 - lam_ref[0] * o1
            out = _rms(att, sub_ref[...], 1e-5) * (1.0 - lam_init)
        o_ref[h2 * blk:(h2 + 1) * blk, :] = out


def attn_prompt(q, k, v, t5_bias, batch, mode, lam=None, subln_w=None, lam_init=0.0):
    n, c = q.shape
    t = n // batch
    nq = t // ATT_BLOCK
    ng = c // 128
    blk = ATT_BLOCK
    smem = pl.BlockSpec(memory_space=pltpu.SMEM)
    nsteps = nq // 2
    qspec = pl.BlockSpec((2 * blk, 128), lambda b, g, i: (b * nsteps + i, g))
    kvspec = pl.BlockSpec((t, 128), lambda b, g, i: (b, g))
    bkspec = pl.BlockSpec((2, blk, blk), lambda b, g, i: (0, 0, 0))
    scratch = [pltpu.VMEM((t, 128), BF16), pltpu.VMEM((t, 128), BF16), pltpu.VMEM((2, 3, blk, blk), F32),
               pltpu.VMEM((4, blk, 128), BF16)] + [pltpu.VMEM((4, blk, 128), F32)] * 3
    bk = jnp.asarray(_prompt_bucket_table())
    if mode == "moba":
        in_specs = [smem, qspec, kvspec, kvspec, bkspec]
        args = (t5_bias, q, k, v, bk)
        scratch += [pltpu.VMEM((nq, 128), F32), pltpu.VMEM((4, blk, 128), BF16)]
    else:
        in_specs = [smem, smem, qspec, kvspec, kvspec, bkspec, pl.BlockSpec((1, 128), lambda b, g, i: (0, 0))]
        args = (t5_bias, lam.reshape(1), q, k, v, bk, subln_w.reshape(1, 128))
    return pl.pallas_call(
        functools.partial(_attn_kernel, mode=mode, nq=nq, lam_init=lam_init),
        grid=(batch, ng, nsteps),
        in_specs=in_specs,
        out_specs=qspec,
        out_shape=jax.ShapeDtypeStruct((n, c), F32),
        scratch_shapes=scratch,
        compiler_params=_cp("arbitrary", "arbitrary", "arbitrary"),
        name="attn_" + mode,
    )(*args)


RWKV_COLS = 1792
RWKV_LN_EPS = 64e-5


def _seg_ones(n):
    r = lax.broadcasted_iota(jnp.int32, (n, n), 0) // 64
    c = lax.broadcasted_iota(jnp.int32, (n, n), 1) // 64
    return jnp.where(r == c, 1.0, 0.0).astype(F32)


def _dot_hi(a, b):
    return jnp.dot(a, b, precision=HI, preferred_element_type=F32)


def _rwkv_prep_math(z, z_prev, mu_ref, w0_ref, w2_ref, a0_ref, a2_ref, g2_ref, kk_ref, ka_ref, outs):
    r_ref, e_ref, kkn_ref, ab_ref, k2_ref, v_ref, gate_ref = outs
    g = GROUP
    zs = z + mu_ref[...] * (z_prev - z)
    k = zs[:, g:2 * g]
    lora = zs[:, 3 * g:3 * g + 128]
    wlin = w0_ref[...] + _dot(jnp.tanh(lora).astype(BF16), w2_ref[...])
    softplus_neg = jnp.maximum(-wlin, 0.0) + jnp.log1p(jnp.exp(-jnp.abs(wlin)))
    a = _sigmoid(a0_ref[...] + _dot(lora.astype(BF16), a2_ref[...]))
    kk = k * kk_ref[...]
    norm = jnp.sqrt(_dot_hi(kk * kk, _seg_ones(g)))
    kk = kk / jnp.maximum(norm, 1e-12)
    r_ref[...] = zs[:, :g]
    e_ref[...] = jnp.exp(-softplus_neg - 0.5)
    kkn_ref[...] = kk
    ab_ref[...] = kk * a
    k2_ref[...] = k * (1.0 + (a - 1.0) * ka_ref[...])
    v_ref[...] = zs[:, 2 * g:3 * g]
    gate_ref[...] = _dot(_sigmoid(zs[:, 3 * g + 128:]).astype(BF16), g2_ref[...])


def _rwkv_prep_kernel(z_ref, mu_ref, w0_ref, w2_ref, a0_ref, a2_ref, g2_ref, kk_ref, ka_ref, *rest):
    outs, zcarry = rest[:7], rest[7]
    t = pl.program_id(1)

    @pl.when(t == 0)
    def _():
        zcarry[...] = jnp.zeros_like(zcarry)

    z = z_ref[...]
    z_prev = _shift_rows(z, 1, [zcarry[7:8, :]])
    zcarry[...] = z[z.shape[0] - 8:, :]
    _rwkv_prep_math(z, z_prev, mu_ref, w0_ref, w2_ref, a0_ref, a2_ref, g2_ref, kk_ref, ka_ref, outs)


def _rwkv_prep_step_kernel(z_ref, zp_ref, mu_ref, w0_ref, w2_ref, a0_ref, a2_ref, g2_ref, kk_ref, ka_ref, *outs):
    _rwkv_prep_math(z_ref[...], zp_ref[...], mu_ref, w0_ref, w2_ref, a0_ref, a2_ref, g2_ref, kk_ref, ka_ref, outs)


def _rwkv_prep_params(mu, w0, w2, a0, a2, g2, k_k, k_a):
    g = GROUP
    zero = jnp.zeros_like(w2)
    return (mu.reshape(1, RWKV_COLS), w0.reshape(1, g), jnp.concatenate([w2, zero], 0).astype(BF16),
            a0.reshape(1, g), jnp.concatenate([zero, a2], 0).astype(BF16), g2.astype(BF16),
            k_k.reshape(1, g), k_a.reshape(1, g))


def rwkv_prep(z, batch, params, shift=None, tl=256):
    n = z.shape[0]
    g = GROUP
    out_shape = [jax.ShapeDtypeStruct((n, g), F32)] * 7
    if shift is not None:
        return pl.pallas_call(_rwkv_prep_step_kernel, out_shape=out_shape, name="rwkv_prep_step")(z, shift, *params)
    nt = n // batch // tl
    row = lambda b, t: (b * nt + t, 0)
    const = lambda b, t: (0, 0)
    vec = pl.BlockSpec((1, g), const)
    lora = pl.BlockSpec((128, g), const)
    return pl.pallas_call(
        _rwkv_prep_kernel,
        grid=(batch, nt),
        in_specs=[pl.BlockSpec((tl, RWKV_COLS), row), pl.BlockSpec((1, RWKV_COLS), const),
                  vec, lora, vec, lora, lora, vec, vec],
        out_specs=[pl.BlockSpec((tl, g), row)] * 7,
        out_shape=out_shape,
        scratch_shapes=[pltpu.VMEM((8, RWKV_COLS), F32)],
        compiler_params=_cp("arbitrary", "arbitrary"),
        name="rwkv_prep",
    )(z, *params)


def _nt(a, b):
    return lax.dot_general(a, b, (((1,), (1,)), ((), ())), precision=HI, preferred_element_type=F32)


def _tn(a, b):
    return lax.dot_general(a, b, (((0,), (0,)), ((), ())), precision=HI, preferred_element_type=F32)


def _rwkv_chunk_pair(r, e, kk, ab, k2, v, c):
    shape = (c, 128)
    half = lax.broadcasted_iota(jnp.int32, shape, 1) >= 64
    row = lax.broadcasted_iota(jnp.int32, shape, 0)
    cum = e
    d = 1
    while d < c:
        cum = cum + jnp.where(row < d, 0.0, pltpu.roll(cum, d, 0))
        d *= 2
    g_inv = jnp.exp(cum)
    at = -kk * jnp.exp(e - cum)
    bt = ab * g_inv
    kt = k2 * g_inv
    rt = r * jnp.exp(-cum)
    g_end = jnp.exp(-cum[c - 1:c, :])
    ri = lax.broadcasted_iota(jnp.int32, (c, c), 0)
    ci = lax.broadcasted_iota(jnp.int32, (c, c), 1)
    per_head = []
    for h in range(2):
        mask = half if h else jnp.logical_not(half)
        ah = jnp.where(mask, at, 0.0)
        rh = jnp.where(mask, rt, 0.0)
        lab = jnp.where(ri > ci, _nt(ah, bt), 0.0)
        lak = jnp.where(ri > ci, _nt(ah, kt), 0.0)
        rb = jnp.where(ri >= ci, _nt(rh, bt), 0.0)
        rk = jnp.where(ri >= ci, _nt(rh, kt), 0.0)
        x = jnp.where(ri == ci, 1.0, 0.0) + lab
        p = lab
        n = 2
        while n < c:
            p = _dot_hi(p, p)
            x = x + _dot_hi(x, p)
            n *= 2
        pa = _dot_hi(x, at)
        q = _dot_hi(x, _dot_hi(lak, v))
        per_head.append((pa, q, _dot_hi(rb, pa), _dot_hi(rb, q) + _dot_hi(rk, v)))
    pa, q, y1, y0 = (jnp.where(half, per_head[1][i], per_head[0][i]) for i in range(4))
    y1 = rt + y1
    bg = bt * g_end
    kg = kt * g_end
    r2 = lax.broadcasted_iota(jnp.int32, (128, 128), 0)
    c2 = lax.broadcasted_iota(jnp.int32, (128, 128), 1)
    same_head = (r2 >= 64) == (c2 >= 64)
    m = jnp.where(r2 == c2, g_end, 0.0) + jnp.where(same_head, _tn(bg, pa), 0.0)
    nn = jnp.where(same_head, _tn(bg, q) + _tn(kg, v), 0.0)
    return m, nn, y1, y0


def _rwkv_chunk_kernel(r_ref, e_ref, kk_ref, ab_ref, k2_ref, v_ref, m_ref, n_ref, y1_ref, y0_ref, *, c):
    for g in range(GROUP // 128):
        ls = slice(g * 128, (g + 1) * 128)
        m, nn, y1, y0 = _rwkv_chunk_pair(r_ref[:, ls], e_ref[:, ls], kk_ref[:, ls], ab_ref[:, ls],
                                         k2_ref[:, ls], v_ref[:, ls], c)
        m_ref[0, g] = m
        n_ref[0, g] = nn
        y1_ref[:, ls] = y1
        y0_ref[:, ls] = y0


def _rwkv_scan_kernel(m_ref, n_ref, y1_ref, y0_ref, r_ref, k2_ref, v_ref, gate_ref, h0_ref,
                      lnw_ref, lnb_ref, rk_ref, y_ref, hout_ref, h):
    t = pl.program_id(1)

    @pl.when(t == 0)
    def _():
        h[...] = h0_ref[0]

    ones = _seg_ones(128)
    for g in range(GROUP // 128):
        ls = slice(g * 128, (g + 1) * 128)
        hg = h[g]
        y = _dot_hi(y1_ref[:, ls], hg) + y0_ref[:, ls]
        h[g] = _dot_hi(m_ref[0, g], hg) + n_ref[0, g]
        mean = _dot_hi(y, ones) * (1.0 / 64)
        yc = y - mean
        var = _dot_hi(yc * yc, ones) * (1.0 / 64)
        yn = yc * lax.rsqrt(var + RWKV_LN_EPS) * lnw_ref[:, ls] + lnb_ref[:, ls]
        v = v_ref[:, ls]
        bonus = _dot_hi(r_ref[:, ls] * k2_ref[:, ls] * rk_ref[:, ls], ones) * v
        y_ref[:, ls] = (yn + bonus) * gate_ref[:, ls]
    hout_ref[0] = h[...]


def rwkv_scan(r, e, kk, ab, k2, v, gate, s0, ln_w, ln_b, r_k, batch, c):
    n, g = r.shape
    nc = n // batch // c
    npair = g // 128
    row = lambda b, t: (b * nc + t, 0)
    blk = pl.BlockSpec((c, g), row)
    mat = pl.BlockSpec((1, npair, 128, 128), lambda b, t: (b * nc + t, 0, 0, 0))
    m, nn, y1, y0 = pl.pallas_call(
        functools.partial(_rwkv_chunk_kernel, c=c),
        grid=(batch, nc),
        in_specs=[blk] * 6,
        out_specs=[mat, mat, blk, blk],
        out_shape=[jax.ShapeDtypeStruct((batch * nc, npair, 128, 128), F32)] * 2
        + [jax.ShapeDtypeStruct((n, g), F32)] * 2,
        compiler_params=_cp("arbitrary", "arbitrary"),
        name="rwkv_chunk",
    )(r, e, kk, ab, k2, v)
    st = jnp.swapaxes(s0, -1, -2).reshape(batch, npair, 2, 64, 64)
    eye2 = jnp.eye(2, dtype=F32)
    h0 = (st[:, :, :, :, None, :] * eye2[None, None, :, None, :, None]).reshape(batch, npair, 128, 128)
    state = pl.BlockSpec((1, npair, 128, 128), lambda b, t: (b, 0, 0, 0))
    vec = pl.BlockSpec((1, g), lambda b, t: (0, 0))
    y, hout = pl.pallas_call(
        _rwkv_scan_kernel,
        grid=(batch, nc),
        in_specs=[mat, mat, blk, blk, blk, blk, blk, blk, state, vec, vec, vec],
        out_specs=[blk, state],
        out_shape=[jax.ShapeDtypeStruct((n, g), F32), jax.ShapeDtypeStruct((batch, npair, 128, 128), F32)],
        scratch_shapes=[pltpu.VMEM((npair, 128, 128), F32)],
        compiler_params=_cp("arbitrary", "arbitrary"),
        name="rwkv_scan",
    )(m, nn, y1, y0, r, k2, v, gate, h0, ln_w.reshape(1, g), ln_b.reshape(1, g), r_k.reshape(1, g))
    hb = hout.reshape(batch, npair, 2, 64, 2, 64)
    s_last = jnp.stack([hb[:, :, 0, :, 0, :], hb[:, :, 1, :, 1, :]], axis=2).reshape(batch, 2 * npair, 64, 64)
    return y, jnp.swapaxes(s_last, -1, -2)


PAGES_PER_STEP = 16


def _head_rows(q, n_maps):
    width = q.shape[1]
    r = lax.broadcasted_iota(jnp.int32, (n_maps, width), 0)
    c = lax.broadcasted_iota(jnp.int32, (n_maps, width), 1) // 64
    return jnp.where(r == c, q, 0.0)


def _past_bucket_row(first_pos, width, q_pos):
    rel = q_pos - (first_pos + np.arange(width))
    return _t5_bucket_np(rel).astype(np.int32).reshape(1, width)


def _bias_rows(bucket_row, t5t_ref):
    out = jnp.zeros((8, bucket_row.shape[1]), F32)
    for b in range(N_BUCKETS):
        out = jnp.where(bucket_row == b, t5t_ref[:, b:b + 1], out)
    return out


def _page_logits(qcol, kt):
    return jnp.sum(qcol * kt, axis=1)


def _moba_scan_kernel(pt_ref, q_ref, *refs, n_steps, n_blocks):
    pages = refs[:PAGES_PER_STEP]
    lg_ref, idx_ref, gate_s = refs[PAGES_PER_STEP:]
    s = pl.program_id(1)

    @pl.when(s == 0)
    def _():
        gate_s[...] = jnp.zeros_like(gate_s)

    qcol = q_ref[0]
    lane = lax.broadcasted_iota(jnp.int32, (8, 128), 1)
    gate = gate_s[...]
    for i, pg in enumerate(pages):
        lg = _page_logits(qcol, pg[0])
        blk, half = divmod(i, 2)
        lg_ref[0, :, blk, half * PAGE:(half + 1) * PAGE] = lg * 0.125
        gate = gate + jnp.where(lane == s * (PAGES_PER_STEP // 2) + blk, jnp.sum(lg, axis=1, keepdims=True), 0.0)
    gate_s[...] = gate

    @pl.when(s == n_steps - 1)
    def _():
        gt = jnp.where(lane < n_blocks, gate * (1.0 / ATT_BLOCK), -jnp.inf)
        out = jnp.zeros((8, 128), jnp.int32)
        for j in range(MOBA_TOPK):
            mx = jnp.max(gt, axis=1, keepdims=True)
            first = jnp.min(jnp.where(gt == mx, lane, 128), axis=1, keepdims=True)
            out = jnp.where(lane == j, first, out)
            gt = jnp.where(lane == first, -jnp.inf, gt)
        idx_ref[0] = out


def _moba_gather_kernel(idx_ref, pt_ref, t5_ref, q_ref, kn_ref, vn_ref, lg_ref, bk_ref, *refs, n_blocks):
    vpages = refs[:2 * MOBA_TOPK]
    o_ref = refs[2 * MOBA_TOPK]
    b = pl.program_id(0)
    h = pl.program_id(1)
    q = q_ref[0, 0]
    own = jnp.sum(q * kn_ref[0, 0], axis=1, keepdims=True) * 0.125 + t5_ref[0, h]
    far = t5_ref[N_BUCKETS - 1, h]
    near = far
    for bkt in range(N_BUCKETS):
        near = jnp.where(bk_ref[...] == bkt, t5_ref[bkt, h], near)
    logits = []
    for s in range(MOBA_TOPK):
        blk = idx_ref[b, h * MOBA_TOPK + s]
        lg = lg_ref[0, 0, pl.ds(blk, 1), :]
        logits.append(lg + jnp.where(blk == n_blocks - 1, near, far))
    mx = own
    for lg in logits:
        mx = jnp.maximum(mx, jnp.max(lg, axis=1, keepdims=True))
    p_own = jnp.exp(own - mx)
    den = p_own
    acc = p_own * vn_ref[0, 0]
    for s, lg in enumerate(logits):
        p = jnp.exp(lg - mx)
        den = den + jnp.sum(p, axis=1, keepdims=True)
        for pg in range(2):
            acc = acc + _nt(p[:, pg * PAGE:(pg + 1) * PAGE], vpages[s * 2 + pg][0, 0])
    o_ref[0, 0] = acc / den


def moba_step(q, k_new, v_new, cache_k, cache_v, page_table, t5_bias):
    nb, c = q.shape
    nh = c // 64
    n_pages = page_table.shape[1]
    n_steps = n_pages // PAGES_PER_STEP
    n_blocks = n_pages * PAGE // ATT_BLOCK
    bps = PAGES_PER_STEP // 2
    qcol = jnp.broadcast_to(q.reshape(nb, nh, 64, 1), (nb, nh, 64, PAGE))

    def page_spec(i):
        return pl.BlockSpec((1, nh, 64, PAGE), lambda b, s, pt: (pt[b, s * PAGES_PER_STEP + i], 0, 0, 0))

    logits, idx = pl.pallas_call(
        functools.partial(_moba_scan_kernel, n_steps=n_steps, n_blocks=n_blocks),
        grid_spec=pltpu.PrefetchScalarGridSpec(
            num_scalar_prefetch=1,
            grid=(nb, n_steps),
            in_specs=[pl.BlockSpec((1, nh, 64, PAGE), lambda b, s, pt: (b, 0, 0, 0))]
            + [page_spec(i) for i in range(PAGES_PER_STEP)],
            out_specs=[pl.BlockSpec((1, nh, bps, ATT_BLOCK), lambda b, s, pt: (b, 0, s, 0)),
                       pl.BlockSpec((1, nh, 128), lambda b, s, pt: (b, 0, 0))],
            scratch_shapes=[pltpu.VMEM((nh, 128), F32)]),
        out_shape=[jax.ShapeDtypeStruct((nb, nh, n_blocks, ATT_BLOCK), F32),
                   jax.ShapeDtypeStruct((nb, nh, 128), jnp.int32)],
        compiler_params=_cp("arbitrary", "arbitrary"),
        name="moba_scan",
    )(page_table, qcol, *([cache_k] * PAGES_PER_STEP))
    sel = idx[:, :, :MOBA_TOPK].reshape(nb, nh * MOBA_TOPK)
    bucket = jnp.asarray(_past_bucket_row((n_blocks - 1) * ATT_BLOCK, ATT_BLOCK, n_pages * PAGE))

    def vpage_spec(s, pg):
        def imap(b, h, sel, pt):
            return (pt[b, 2 * sel[b, h * MOBA_TOPK + s] + pg], h, 0, 0)
        return pl.BlockSpec((1, 1, 64, PAGE), imap)

    row = pl.BlockSpec((1, 1, 1, 64), lambda b, h, sel, pt: (b, h, 0, 0))
    out = pl.pallas_call(
        functools.partial(_moba_gather_kernel, n_blocks=n_blocks),
        grid_spec=pltpu.PrefetchScalarGridSpec(
            num_scalar_prefetch=2,
            grid=(nb, nh),
            in_specs=[pl.BlockSpec(memory_space=pltpu.SMEM), row, row, row,
                      pl.BlockSpec((1, 1, n_blocks, ATT_BLOCK), lambda b, h, sel, pt: (b, h, 0, 0)),
                      pl.BlockSpec((1, ATT_BLOCK), lambda b, h, sel, pt: (0, 0))]
            + [vpage_spec(s, pg) for s in range(MOBA_TOPK) for pg in range(2)],
            out_specs=row),
        out_shape=jax.ShapeDtypeStruct((nb, nh, 1, 64), F32),
        compiler_params=_cp("arbitrary", "arbitrary"),
        name="moba_gather",
    )(sel, page_table, t5_bias, q.reshape(nb, nh, 1, 64), k_new.reshape(nb, nh, 1, 64),
      v_new.reshape(nb, nh, 1, 64), logits, bucket, *([cache_v] * (2 * MOBA_TOPK)))
    return out.reshape(nb, c)


def _diff_step_kernel(pt_ref, lam_ref, qcol_ref, q_ref, kn_ref, vn_ref, t5t_ref, bk_ref, sub_ref, *refs,
                      n_steps, lam_init):
    kpages = refs[:PAGES_PER_STEP]
    vpages = refs[PAGES_PER_STEP:2 * PAGES_PER_STEP]
    o_ref, m_s, l_s, acc_s = refs[2 * PAGES_PER_STEP:]
    s = pl.program_id(1)
    nh = vn_ref.shape[1]
    head_of_row = lax.broadcasted_iota(jnp.int32, (2 * nh, 128), 0) // 2

    def per_head(rows):
        out = jnp.broadcast_to(rows[0], (2 * nh, 128))
        for h in range(1, nh):
            out = jnp.where(head_of_row == h, rows[h], out)
        return out

    @pl.when(s == 0)
    def _():
        m_s[...] = jnp.sum(q_ref[0] * kn_ref[0], axis=1, keepdims=True) * 0.125 + t5t_ref[:, 0:1]
        l_s[...] = jnp.ones_like(l_s)
        vn = vn_ref[0]
        acc_s[...] = per_head([vn[h:h + 1, :] for h in range(nh)])

    far = t5t_ref[:, N_BUCKETS - 1:N_BUCKETS]
    near = _bias_rows(bk_ref[...], t5t_ref)
    qcol = qcol_ref[0]
    m, l, acc = m_s[...], l_s[...], acc_s[...]
    for i in range(PAGES_PER_STEP):
        lg = _page_logits(qcol, kpages[i][0]) * 0.125
        if i == PAGES_PER_STEP - 1:
            lg = lg + jnp.where(s == n_steps - 1, near, far)
        else:
            lg = lg + far
        mn = jnp.maximum(m, jnp.max(lg, axis=1, keepdims=True))
        alpha = jnp.exp(m - mn)
        p = jnp.exp(lg - mn)
        l = alpha * l + jnp.sum(p, axis=1, keepdims=True)
        pb = p.astype(BF16)
        pv = per_head([_dot(pb, vpages[i][0, :, h, :].astype(BF16)) for h in range(nh)])
        acc = alpha * acc + pv
        m = mn
    m_s[...], l_s[...], acc_s[...] = m, l, acc

    @pl.when(s == n_steps - 1)
    def _():
        a = acc / l
        for h in range(nh):
            att = a[2 * h:2 * h + 1, :] - lam_ref[0] * a[2 * h + 1:2 * h + 2, :]
            o_ref[0, h:h + 1, :] = _rms(att, sub_ref[...], 1e-5) * (1.0 - lam_init)


def diff_step(q, k_new, v_new, cache_k, cache_v, page_table, t5_bias, lam, subln_w, lam_init):
    nb, c = q.shape
    nm = c // 64
    nh = nm // 2
    n_pages = page_table.shape[1]
    n_steps = n_pages // PAGES_PER_STEP
    bucket = jnp.asarray(_past_bucket_row((n_pages - 1) * PAGE, PAGE, n_pages * PAGE))
    qcol = jnp.broadcast_to(q.reshape(nb, nm, 64, 1), (nb, nm, 64, PAGE))

    def kpage_spec(i):
        return pl.BlockSpec((1, nm, 64, PAGE), lambda b, s, pt: (pt[b, s * PAGES_PER_STEP + i], 0, 0, 0))

    def vpage_spec(i):
        return pl.BlockSpec((1, PAGE, nh, 128), lambda b, s, pt: (pt[b, s * PAGES_PER_STEP + i], 0, 0, 0))

    maps = pl.BlockSpec((1, nm, 64), lambda b, s, pt: (b, 0, 0))
    heads = pl.BlockSpec((1, nh, 128), lambda b, s, pt: (b, 0, 0))
    out = pl.pallas_call(
        functools.partial(_diff_step_kernel, n_steps=n_steps, lam_init=lam_init),
        grid_spec=pltpu.PrefetchScalarGridSpec(
            num_scalar_prefetch=1,
            grid=(nb, n_steps),
            in_specs=[pl.BlockSpec(memory_space=pltpu.SMEM),
                      pl.BlockSpec((1, nm, 64, PAGE), lambda b, s, pt: (b, 0, 0, 0)), maps, maps, heads,
                      pl.BlockSpec((nm, N_BUCKETS), lambda b, s, pt: (0, 0)),
                      pl.BlockSpec((1, PAGE), lambda b, s, pt: (0, 0)),
                      pl.BlockSpec((1, 128), lambda b, s, pt: (0, 0))]
            + [kpage_spec(i) for i in range(PAGES_PER_STEP)] + [vpage_spec(i) for i in range(PAGES_PER_STEP)],
            out_specs=heads,
            scratch_shapes=[pltpu.VMEM((nm, 1), F32), pltpu.VMEM((nm, 1), F32), pltpu.VMEM((nm, 128), F32)]),
        out_shape=jax.ShapeDtypeStruct((nb, nh, 128), F32),
        compiler_params=_cp("arbitrary", "arbitrary"),
        name="diff_step",
    )(page_table, lam.reshape(1), qcol, q.reshape(nb, nm, 64), k_new.reshape(nb, nm, 64), v_new.reshape(nb, nh, 128),
      t5_bias.T, bucket, subln_w.reshape(1, 128), *([cache_k] * PAGES_PER_STEP), *([cache_v] * PAGES_PER_STEP))
    return out.reshape(nb, c)


def rwkv_mix(z, shift0, s0, mu, w0, w2, a0, a2, g2, k_k, k_a, r_k, ln_w, ln_b, batch, step):
    params = _rwkv_prep_params(mu, w0, w2, a0, a2, g2, k_k, k_a)
    if not step:
        outs = rwkv_prep(z, batch, params)
        r, e, kk, ab, k2, v, gate = outs
        return rwkv_scan(r, e, kk, ab, k2, v, gate, s0, ln_w, ln_b, r_k, batch, 64)
    outs = rwkv_prep(z, batch, params, shift=shift0)
    r, e, kk, ab, k2, v, gate = (jnp.pad(o[:, None, :], ((0, 0), (0, 7), (0, 0))).reshape(batch * 8, GROUP)
                                 for o in outs)
    y, s_last = rwkv_scan(r, e, kk, ab, k2, v, gate, s0, ln_w, ln_b, r_k, batch, 8)
    return y.reshape(batch, 8, GROUP)[:, 0], s_last


def _trunk(x, p, batch, step, st, W):
    depth = p.shape[0]
    outs = {k: [] for k in ("moba_k", "moba_v", "lru_conv", "lru_h", "diff_k", "diff_v",
                            "rwkv_shift", "rwkv_s", "ffn_conv")}
    t = x.shape[0] // batch
    for i in range(depth):
        j = i // 2
        if i % 2 == 0:
            gate, xin, q, k, v = norm_linear(x, W["norm_mix_pre"][i], W["even_w_in"][j], (GROUP,) * 5)
            lru_w = (W["lru_conv_w"][j], W["lru_conv_b"][j], W["lru_w_a"][j], W["lru_b_a"][j],
                     W["lru_w_x"][j], W["lru_b_x"][j], W["lru_lambda"][j])
            if step:
                ya, conv, h_last = lru_step(gate, xin, st["lru_conv"][j], st["lru_h"][j], *lru_w)
                yb = moba_step(q, k, v, st["moba_k"][j], st["moba_v"][j], st["page_table"], W["t5_bias"])
            else:
                ya, conv, h_last = lru_prompt(gate, xin, *lru_w, batch)
                yb = attn_prompt(q, k, v, W["t5_bias"], batch, "moba")
            outs["moba_k"].append(k)
            outs["moba_v"].append(v)
            outs["lru_conv"].append(conv)
            outs["lru_h"].append(h_last)
            w_out = W["even_w_out"][j]
        else:
            z, q, k, v = norm_linear(x, W["norm_mix_pre"][i], W["odd_w_in"][j], (RWKV_COLS, GROUP, GROUP, GROUP))
            rw = (W["rwkv_mu"][j], W["rwkv_w0"][j], W["rwkv_w2"][j], W["rwkv_a0"][j], W["rwkv_a2"][j],
                  W["rwkv_g2"][j], W["rwkv_k_k"][j], W["rwkv_k_a"][j], W["rwkv_r_k"][j],
                  W["rwkv_ln_w"][j], W["rwkv_ln_b"][j])
            lam_init = 0.8 - 0.6 * math.exp(-0.3 * i)
            lf = W["diff_lambda"][j]
            lam = jnp.exp(jnp.sum(lf[0] * lf[1])) - jnp.exp(jnp.sum(lf[2] * lf[3])) + lam_init
            if step:
                ya, s_last = rwkv_mix(z, st["rwkv_shift"][j], st["rwkv_s"][j], *rw, batch, True)
                yb = diff_step(q, k, v, st["diff_k"][j], st["diff_v"][j], st["page_table"], W["t5_bias"],
                               lam, W["diff_subln_w"][j], lam_init)
                shift = z
            else:
                s0 = jnp.zeros((batch, 8, 64, 64), F32)
                ya, s_last = rwkv_mix(z, None, s0, *rw, batch, False)
                yb = attn_prompt(q, k, v, W["t5_bias"], batch, "diff", lam=lam, subln_w=W["diff_subln_w"][j],
                                 lam_init=lam_init)
                shift = z.reshape(batch, t, RWKV_COLS)[:, t - 1]
            outs["diff_k"].append(k)
            outs["diff_v"].append(v)
            outs["rwkv_shift"].append(shift)
            outs["rwkv_s"].append(s_last)
            w_out = W["odd_w_out"][j]
        x = out_proj(ya, yb, w_out, x, W["norm_mix_post"][i])
        ffn_w = (W["norm_ffn_pre"][i], W["ffn_w_up"][i], W["ffn_conv_w"][i], W["ffn_conv_b"][i], W["ffn_w_down"][i],
                 W["norm_ffn_post"][i], W["ple_w_gate"][i], W["ple_w_proj"][i])
        if step:
            x, fbuf = ffn_step(x, p[i], *ffn_w, st["ffn_conv"][i])
        else:
            x, fbuf = ffn_prompt(x, p[i], *ffn_w, batch)
        outs["ffn_conv"].append(fbuf)
    return x, {k: jnp.stack(v) for k, v in outs.items()}


def kernel(x_prompt, x_sample, cache_moba_k, cache_moba_v, state_lru_conv, state_lru_h, cache_diff_k, cache_diff_v, state_rwkv_shift, state_rwkv, state_ffn_conv, page_table, p_prompt, p_sample, t5_bias, norm_mix_pre, norm_mix_post, norm_ffn_pre, norm_ffn_post, even_w_in, even_w_out, lru_conv_w, lru_conv_b, lru_w_a, lru_b_a, lru_w_x, lru_b_x, lru_lambda, odd_w_in, odd_w_out, rwkv_mu, rwkv_w0, rwkv_w2, rwkv_a0, rwkv_a2, rwkv_g2, rwkv_k_k, rwkv_k_a, rwkv_r_k, rwkv_ln_w, rwkv_ln_b, diff_lambda, diff_subln_w, ffn_w_up, ffn_conv_w, ffn_conv_b, ffn_w_down, ple_w_proj, ple_w_gate):
    W = dict(t5_bias=t5_bias, norm_mix_pre=norm_mix_pre, norm_mix_post=norm_mix_post,
             norm_ffn_pre=norm_ffn_pre, norm_ffn_post=norm_ffn_post,
             even_w_in=even_w_in, even_w_out=even_w_out, lru_conv_w=lru_conv_w, lru_conv_b=lru_conv_b,
             lru_w_a=lru_w_a, lru_b_a=lru_b_a, lru_w_x=lru_w_x, lru_b_x=lru_b_x, lru_lambda=lru_lambda,
             odd_w_in=odd_w_in, odd_w_out=odd_w_out, rwkv_mu=rwkv_mu, rwkv_w0=rwkv_w0, rwkv_w2=rwkv_w2,
             rwkv_a0=rwkv_a0, rwkv_a2=rwkv_a2, rwkv_g2=rwkv_g2, rwkv_k_k=rwkv_k_k, rwkv_k_a=rwkv_k_a,
             rwkv_r_k=rwkv_r_k, rwkv_ln_w=rwkv_ln_w, rwkv_ln_b=rwkv_ln_b,
             diff_lambda=diff_lambda, diff_subln_w=diff_subln_w,
             ffn_w_up=ffn_w_up, ffn_conv_w=ffn_conv_w, ffn_conv_b=ffn_conv_b, ffn_w_down=ffn_w_down,
             ple_w_proj=ple_w_proj, ple_w_gate=ple_w_gate)
    bp, tp, d = x_prompt.shape
    bs, ts, _ = x_sample.shape
    depth = p_prompt.shape[0]
    n_pp = tp // PAGE
    assert ts == 1, "the sample group is a single-token step"

    yp, P = _trunk(x_prompt.reshape(bp * tp, d), p_prompt.reshape(depth, bp * tp, -1), bp, False, None, W)

    pool = cache_moba_k.shape[1]
    st = dict(moba_k=jnp.transpose(cache_moba_k, (0, 1, 3, 4, 2)), moba_v=jnp.transpose(cache_moba_v, (0, 1, 3, 4, 2)),
              diff_k=jnp.transpose(cache_diff_k, (0, 1, 3, 4, 5, 2)).reshape(-1, pool, 8, 64, PAGE),
              diff_v=cache_diff_v,
              lru_conv=state_lru_conv, lru_h=state_lru_h, rwkv_shift=state_rwkv_shift, rwkv_s=state_rwkv,
              ffn_conv=state_ffn_conv, page_table=page_table)
    ys, S = _trunk(x_sample.reshape(bs * ts, d), p_sample.reshape(depth, bs * ts, -1), bs, True, st, W)

    ne, no = P["moba_k"].shape[0], P["diff_k"].shape[0]
    return (yp.reshape(bp, tp, d), ys.reshape(bs, ts, d),
            P["moba_k"].reshape(ne, bp, n_pp, PAGE, 8, 64), P["moba_v"].reshape(ne, bp, n_pp, PAGE, 8, 64),
            S["moba_k"].reshape(ne, bs, ts, 8, 64), S["moba_v"].reshape(ne, bs, ts, 8, 64),
            P["lru_conv"], S["lru_conv"], P["lru_h"], S["lru_h"],
            P["diff_k"].reshape(no, bp, n_pp, PAGE, 4, 2, 64), P["diff_v"].reshape(no, bp, n_pp, PAGE, 4, 128),
            S["diff_k"].reshape(no, bs, ts, 4, 2, 64), S["diff_v"].reshape(no, bs, ts, 4, 128),
            P["rwkv_shift"], S["rwkv_shift"], P["rwkv_s"], S["rwkv_s"],
            P["ffn_conv"], S["ffn_conv"])
```

```python
import functools
import math

import numpy as np
import jax
import jax.numpy as jnp
from jax import lax
from jax.experimental import pallas as pl
from jax.experimental.pallas import tpu as pltpu

F32 = jnp.float32
BF16 = jnp.bfloat16
HI = lax.Precision.HIGHEST

D_MODEL = 1024
GROUP = 512
PAGE = 128
VMEM_LIMIT = 56 * 1024 * 1024


def _cp(*sem):
    return pltpu.CompilerParams(dimension_semantics=sem, vmem_limit_bytes=VMEM_LIMIT)


def _rms(x, g, eps):
    return x * lax.rsqrt(jnp.mean(x * x, axis=-1, keepdims=True) + eps) * g


def _gelu(x):
    return 0.5 * x * (1.0 + jnp.tanh(math.sqrt(2.0 / math.pi) * (x + 0.044715 * (x * x * x))))


def _sigmoid(x):
    return 1.0 / (1.0 + jnp.exp(-x))


def _dot(a, b):
    return jnp.dot(a, b, preferred_element_type=F32)


def _round_robin(chains):
    done = {}
    while len(done) < len(chains):
        for i, chain in enumerate(chains):
            if i not in done:
                try:
                    next(chain)
                except StopIteration as stop:
                    done[i] = stop.value
    return [done[i] for i in range(len(chains))]


def _shift_rows(x, d, fill):
    r = pltpu.roll(x, d, 0)
    row = lax.broadcasted_iota(jnp.int32, x.shape, 0)
    for i in range(d):
        r = jnp.where(row == i, fill[i], r)
    return r


def _norm_linear_kernel(x_ref, g_ref, w_ref, *out_refs, splits):
    h = _rms(x_ref[...], g_ref[...], 1e-6).astype(BF16)
    off = 0
    for o_ref, n in zip(out_refs, splits):
        o_ref[...] = _dot(h, w_ref[:, off:off + n])
        off += n


def norm_linear(x, g, w, splits):
    n, d = x.shape
    m = w.shape[1]
    tm = min(n, 512)
    return pl.pallas_call(
        functools.partial(_norm_linear_kernel, splits=splits),
        grid=(n // tm,),
        in_specs=[pl.BlockSpec((tm, d), lambda i: (i, 0)),
                  pl.BlockSpec((1, d), lambda i: (0, 0)),
                  pl.BlockSpec((d, m), lambda i: (0, 0))],
        out_specs=[pl.BlockSpec((tm, s), lambda i: (i, 0)) for s in splits],
        out_shape=[jax.ShapeDtypeStruct((n, s), F32) for s in splits],
        compiler_params=_cp("arbitrary"),
        name="norm_linear",
    )(x, g.reshape(1, d), w.astype(BF16))


def _out_proj_kernel(a_ref, b_ref, w_ref, x_ref, g_ref, o_ref):
    y = _dot(a_ref[...].astype(BF16), w_ref[:GROUP, :]) + _dot(b_ref[...].astype(BF16), w_ref[GROUP:, :])
    o_ref[...] = x_ref[...] + _rms(y, g_ref[...], 1e-6)


def out_proj(a, b, w, x, g):
    n, d = x.shape
    tm = min(n, 512)
    return pl.pallas_call(
        _out_proj_kernel,
        grid=(n // tm,),
        in_specs=[pl.BlockSpec((tm, GROUP), lambda i: (i, 0)),
                  pl.BlockSpec((tm, GROUP), lambda i: (i, 0)),
                  pl.BlockSpec((2 * GROUP, d), lambda i: (0, 0)),
                  pl.BlockSpec((tm, d), lambda i: (i, 0)),
                  pl.BlockSpec((1, d), lambda i: (0, 0))],
        out_specs=pl.BlockSpec((tm, d), lambda i: (i, 0)),
        out_shape=jax.ShapeDtypeStruct((n, d), F32),
        compiler_params=_cp("arbitrary"),
        name="out_proj",
    )(a, b, w.astype(BF16), x, g.reshape(1, d))


def _ffn_tail(acc, x, gpost, p, wg_ref, wp_ref):
    x1 = x + _rms(acc, gpost, 1e-6)
    gate = _sigmoid(_dot(x1.astype(BF16), wg_ref[...]))
    return x1 + gate * _dot(p.astype(BF16), wp_ref[...])


def _ffn_kernel(x_ref, p_ref, gpre_ref, wug_ref, wuv_ref, cwg_ref, cwv_ref, cbg_ref, cbv_ref, wd_ref,
                gpost_ref, wg_ref, wp_ref, o_ref, st_ref, hn, acc, carry, *, tm, f, nf):
    t = pl.program_id(1)
    j = pl.program_id(2)

    @pl.when(j == 0)
    def _():
        hn[...] = _rms(x_ref[...], gpre_ref[...], 1e-6).astype(BF16)
        acc[...] = jnp.zeros_like(acc)

    @pl.when(t == 0)
    def _():
        carry[j] = jnp.zeros((8, 2 * f), F32)

    h = hn[...]
    ug = _dot(h, wug_ref[...])
    uv = _dot(h, wuv_ref[...])
    prev = carry[j]

    def conv(u, pv, cw_ref, cb_ref):
        cw = cw_ref[...]
        u1 = _shift_rows(u, 1, [pv[7:8]])
        u2 = _shift_rows(u, 2, [pv[6:7], pv[7:8]])
        return cb_ref[...] + cw[0:1] * u2 + cw[1:2] * u1 + cw[2:3] * u

    cg = conv(ug, prev[:, :f], cwg_ref, cbg_ref)
    cv = conv(uv, prev[:, f:], cwv_ref, cbv_ref)
    carry[j] = jnp.concatenate([ug[tm - 8:, :], uv[tm - 8:, :]], axis=1)
    st_ref[0, 0, 0, 0:1, :] = ug[tm - 2:tm - 1, :]
    st_ref[0, 0, 0, 1:2, :] = uv[tm - 2:tm - 1, :]
    st_ref[0, 0, 1, 0:1, :] = ug[tm - 1:tm, :]
    st_ref[0, 0, 1, 1:2, :] = uv[tm - 1:tm, :]
    act = (_gelu(cg) * cv).astype(BF16)
    acc[...] += _dot(act, wd_ref[...])

    @pl.when(j == nf - 1)
    def _():
        o_ref[...] = _ffn_tail(acc[...], x_ref[...], gpost_ref[...], p_ref[...], wg_ref, wp_ref)


def ffn_prompt(x, p, gpre, w_up, conv_w, conv_b, w_down, gpost, w_gate, w_proj, batch, tm=1024, f=512):
    n, d = x.shape
    dff = w_down.shape[0]
    nf = dff // f
    nt = n // batch // tm
    pd = p.shape[1]
    w_up = w_up.astype(BF16)
    conv_b = conv_b.reshape(1, 2 * dff)
    row = lambda b, t, j: (b * nt + t, 0)
    const = lambda b, t, j: (0, 0)
    out, st = pl.pallas_call(
        functools.partial(_ffn_kernel, tm=tm, f=f, nf=nf),
        grid=(batch, nt, nf),
        in_specs=[pl.BlockSpec((tm, d), row),
                  pl.BlockSpec((tm, pd), row),
                  pl.BlockSpec((1, d), const),
                  pl.BlockSpec((d, f), lambda b, t, j: (0, j)),
                  pl.BlockSpec((d, f), lambda b, t, j: (0, nf + j)),
                  pl.BlockSpec((3, f), lambda b, t, j: (0, j)),
                  pl.BlockSpec((3, f), lambda b, t, j: (0, nf + j)),
                  pl.BlockSpec((1, f), lambda b, t, j: (0, j)),
                  pl.BlockSpec((1, f), lambda b, t, j: (0, nf + j)),
                  pl.BlockSpec((f, d), lambda b, t, j: (j, 0)),
                  pl.BlockSpec((1, d), const),
                  pl.BlockSpec((d, d), const),
                  pl.BlockSpec((pd, d), const)],
        out_specs=[pl.BlockSpec((tm, d), row),
                   pl.BlockSpec((1, 1, 2, 2, f), lambda b, t, j: (b, t, 0, 0, j))],
        out_shape=[jax.ShapeDtypeStruct((n, d), F32),
                   jax.ShapeDtypeStruct((batch, nt, 2, 2, dff), F32)],
        scratch_shapes=[pltpu.VMEM((tm, d), BF16), pltpu.VMEM((tm, d), F32), pltpu.VMEM((nf, 8, 2 * f), F32)],
        compiler_params=_cp("arbitrary", "arbitrary", "arbitrary"),
        name="ffn_prompt",
    )(x, p, gpre.reshape(1, d), w_up, w_up, conv_w, conv_w, conv_b, conv_b, w_down.astype(BF16),
      gpost.reshape(1, d), w_gate.astype(BF16), w_proj.astype(BF16))
    return out, st[:, nt - 1].reshape(batch, 2, 2 * dff)


def _ffn_step_kernel(x_ref, p_ref, gpre_ref, wug_ref, wuv_ref, cwg_ref, cwv_ref, cbg_ref, cbv_ref, wd_ref,
                     gpost_ref, wg_ref, wp_ref, s0g_ref, s0v_ref, s1g_ref, s1v_ref,
                     o_ref, ug_ref, uv_ref, acc, *, nf):
    j = pl.program_id(0)

    @pl.when(j == 0)
    def _():
        acc[...] = jnp.zeros_like(acc)

    h = _rms(x_ref[...], gpre_ref[...], 1e-6).astype(BF16)
    ug = _dot(h, wug_ref[...])
    uv = _dot(h, wuv_ref[...])
    ug_ref[...] = ug
    uv_ref[...] = uv
    cwg = cwg_ref[...]
    cwv = cwv_ref[...]
    cg = cbg_ref[...] + cwg[0:1] * s0g_ref[...] + cwg[1:2] * s1g_ref[...] + cwg[2:3] * ug
    cv = cbv_ref[...] + cwv[0:1] * s0v_ref[...] + cwv[1:2] * s1v_ref[...] + cwv[2:3] * uv
    acc[...] += _dot((_gelu(cg) * cv).astype(BF16), wd_ref[...])

    @pl.when(j == nf - 1)
    def _():
        o_ref[...] = _ffn_tail(acc[...], x_ref[...], gpost_ref[...], p_ref[...], wg_ref, wp_ref)


def ffn_step(x, p, gpre, w_up, conv_w, conv_b, w_down, gpost, w_gate, w_proj, state, f=512):
    n, d = x.shape
    dff = w_down.shape[0]
    nf = dff // f
    pd = p.shape[1]
    w_up = w_up.astype(BF16)
    conv_b = conv_b.reshape(1, 2 * dff)
    s0, s1 = state[:, 0, :], state[:, 1, :]
    const = lambda j: (0, 0)
    lo = lambda j: (0, j)
    hi = lambda j: (0, nf + j)
    out, ug, uv = pl.pallas_call(
        functools.partial(_ffn_step_kernel, nf=nf),
        grid=(nf,),
        in_specs=[pl.BlockSpec((n, d), const),
                  pl.BlockSpec((n, pd), const),
                  pl.BlockSpec((1, d), const),
                  pl.BlockSpec((d, f), lo), pl.BlockSpec((d, f), hi),
                  pl.BlockSpec((3, f), lo), pl.BlockSpec((3, f), hi),
                  pl.BlockSpec((1, f), lo), pl.BlockSpec((1, f), hi),
                  pl.BlockSpec((f, d), lambda j: (j, 0)),
                  pl.BlockSpec((1, d), const),
                  pl.BlockSpec((d, d), const),
                  pl.BlockSpec((pd, d), const),
                  pl.BlockSpec((n, f), lo), pl.BlockSpec((n, f), hi),
                  pl.BlockSpec((n, f), lo), pl.BlockSpec((n, f), hi)],
        out_specs=[pl.BlockSpec((n, d), const), pl.BlockSpec((n, f), lo), pl.BlockSpec((n, f), lo)],
        out_shape=[jax.ShapeDtypeStruct((n, d), F32), jax.ShapeDtypeStruct((n, dff), F32),
                   jax.ShapeDtypeStruct((n, dff), F32)],
        scratch_shapes=[pltpu.VMEM((n, d), F32)],
        compiler_params=_cp("arbitrary"),
        name="ffn_step",
    )(x, p, gpre.reshape(1, d), w_up, w_up, conv_w, conv_w, conv_b, conv_b, w_down.astype(BF16),
      gpost.reshape(1, d), w_gate.astype(BF16), w_proj.astype(BF16), s0, s0, s1, s1)
    new_state = jnp.stack([s1, jnp.concatenate([ug, uv], axis=1)], axis=1)
    return out, new_state


def _lru_gates(xc, wa_ref, ba_ref, wx_ref, bx_ref, lam_ref):
    xb = xc.astype(BF16)
    r = _sigmoid(_dot(xb, wa_ref[...]) + ba_ref[...])
    i = _sigmoid(_dot(xb, wx_ref[...]) + bx_ref[...])
    lam = lam_ref[...]
    softplus_neg = jnp.maximum(-lam, 0.0) + jnp.log1p(jnp.exp(-jnp.abs(lam)))
    log_a = -8.0 * softplus_neg * r
    a = jnp.exp(log_a)
    th = jnp.tanh(log_a)
    u = jnp.sqrt(-2.0 * th / (1.0 - th)) * (i * xc)
    return a, u


def _lru_kernel(gate_ref, x_ref, cw_ref, cb_ref, wa_ref, ba_ref, wx_ref, bx_ref, lam_ref,
                y_ref, conv_ref, hlast_ref, xcarry, hcarry, *, tl):
    t = pl.program_id(1)

    @pl.when(t == 0)
    def _():
        xcarry[...] = jnp.zeros_like(xcarry)
        hcarry[...] = jnp.zeros_like(hcarry)

    x = x_ref[...]
    c = xcarry[...]
    cw = cw_ref[...]
    xs1 = _shift_rows(x, 1, [c[7:8]])
    xs2 = _shift_rows(x, 2, [c[6:7], c[7:8]])
    xs3 = _shift_rows(x, 3, [c[5:6], c[6:7], c[7:8]])
    xc = cb_ref[...] + cw[0:1] * xs3 + cw[1:2] * xs2 + cw[2:3] * xs1 + cw[3:4] * x
    a, u = _lru_gates(xc, wa_ref, ba_ref, wx_ref, bx_ref, lam_ref)
    row = lax.broadcasted_iota(jnp.int32, a.shape, 0)
    d = 1
    while d < tl:
        a_s = jnp.where(row < d, 1.0, pltpu.roll(a, d, 0))
        u_s = jnp.where(row < d, 0.0, pltpu.roll(u, d, 0))
        u = u + a * u_s
        a = a * a_s
        d *= 2
    h = a * hcarry[...] + u
    y_ref[...] = h * _gelu(gate_ref[...])
    hcarry[...] = h[tl - 1:tl, :]
    hlast_ref[0] = h[tl - 1:tl, :]
    xcarry[...] = x[tl - 8:, :]
    conv_ref[0] = x[tl - 3:, :]


def _block_diag(w):
    h, a, b = w.shape
    eye = jnp.eye(h, dtype=w.dtype)
    return (eye[:, None, :, None] * w[:, :, None, :]).reshape(h * a, h * b)


def lru_prompt(gate, x, conv_w, conv_b, w_a, b_a, w_x, b_x, lam, batch, tl=256):
    n, c = x.shape
    nt = n // batch // tl
    row = lambda b, t: (b * nt + t, 0)
    const = lambda b, t: (0, 0)
    vec = pl.BlockSpec((1, c), const)
    y, conv, hlast = pl.pallas_call(
        functools.partial(_lru_kernel, tl=tl),
        grid=(batch, nt),
        in_specs=[pl.BlockSpec((tl, c), row), pl.BlockSpec((tl, c), row),
                  pl.BlockSpec((4, c), const), vec,
                  pl.BlockSpec((c, c), const), vec, pl.BlockSpec((c, c), const), vec, vec],
        out_specs=[pl.BlockSpec((tl, c), row),
                   pl.BlockSpec((1, 3, c), lambda b, t: (b, 0, 0)),
                   pl.BlockSpec((1, 1, c), lambda b, t: (b, 0, 0))],
        out_shape=[jax.ShapeDtypeStruct((n, c), F32), jax.ShapeDtypeStruct((batch, 3, c), F32),
                   jax.ShapeDtypeStruct((batch, 1, c), F32)],
        scratch_shapes=[pltpu.VMEM((8, c), F32), pltpu.VMEM((1, c), F32)],
        compiler_params=_cp("arbitrary", "arbitrary"),
        name="lru_prompt",
    )(gate, x, conv_w, conv_b.reshape(1, c), _block_diag(w_a).astype(BF16), b_a.reshape(1, c),
      _block_diag(w_x).astype(BF16), b_x.reshape(1, c), lam.reshape(1, c))
    return y, conv, hlast.reshape(batch, c)


def _lru_step_kernel(gate_ref, x_ref, s0_ref, s1_ref, s2_ref, h0_ref, cw_ref, cb_ref, wa_ref, ba_ref,
                     wx_ref, bx_ref, lam_ref, y_ref, h_ref):
    cw = cw_ref[...]
    x = x_ref[...]
    xc = cb_ref[...] + cw[0:1] * s0_ref[...] + cw[1:2] * s1_ref[...] + cw[2:3] * s2_ref[...] + cw[3:4] * x
    a, u = _lru_gates(xc, wa_ref, ba_ref, wx_ref, bx_ref, lam_ref)
    h = a * h0_ref[...] + u
    h_ref[...] = h
    y_ref[...] = h * _gelu(gate_ref[...])


def lru_step(gate, x, conv_state, h0, conv_w, conv_b, w_a, b_a, w_x, b_x, lam):
    n, c = x.shape
    y, h = pl.pallas_call(
        _lru_step_kernel,
        out_shape=[jax.ShapeDtypeStruct((n, c), F32), jax.ShapeDtypeStruct((n, c), F32)],
        name="lru_step",
    )(gate, x, conv_state[:, 0], conv_state[:, 1], conv_state[:, 2], h0, conv_w, conv_b.reshape(1, c),
      _block_diag(w_a).astype(BF16), b_a.reshape(1, c), _block_diag(w_x).astype(BF16), b_x.reshape(1, c),
      lam.reshape(1, c))
    new_conv = jnp.stack([conv_state[:, 1], conv_state[:, 2], x], axis=1)
    return y, new_conv, h


N_BUCKETS = 32
T5_MAX_EXACT = 16
T5_MAX_DISTANCE = 128
NEG = -1e30
LOG2E = math.log2(math.e)
ATT_BLOCK = 256
MOBA_TOPK = 3


def _t5_bucket_np(rel):
    n = np.maximum(rel, 0)
    nf = np.maximum(n, 1).astype(np.float32)
    large = T5_MAX_EXACT + (np.log(nf / np.float32(T5_MAX_EXACT)) / np.float32(math.log(T5_MAX_DISTANCE / T5_MAX_EXACT))
                            * np.float32(N_BUCKETS - T5_MAX_EXACT)).astype(np.int32)
    large = np.minimum(large, N_BUCKETS - 1)
    return np.where(n < T5_MAX_EXACT, n, large).astype(np.int32)


def _prompt_bucket_table():
    r = np.arange(ATT_BLOCK)[:, None]
    c = np.arange(ATT_BLOCK)[None, :]
    tabs = []
    for o in range(2):
        rel = o * ATT_BLOCK + r - c
        tabs.append(np.where(rel >= 0, _t5_bucket_np(rel), -1))
    return np.stack(tabs).astype(np.int32)


def _bias_from_buckets(bucket, t5_ref, col):
    out = jnp.full(bucket.shape, NEG, F32)
    for b in range(N_BUCKETS):
        out = jnp.where(bucket == b, t5_ref[b, col], out)
    return out


def _attn_kernel(*refs, mode, nq, lam_init):
    if mode == "moba":
        (t5_ref, q_ref, k_ref, v_ref, bk_ref, o_ref, kb, vb, bias, qs_s, m_s, l_s, acc_s, kmean, sel_s) = refs
    else:
        (t5_ref, lam_ref, q_ref, k_ref, v_ref, bk_ref, sub_ref, o_ref, kb, vb, bias, qs_s, m_s, l_s, acc_s) = refs
    g = pl.program_id(1)
    qi = pl.program_id(2)
    blk = ATT_BLOCK

    @pl.when(qi == 0)
    def _():
        def cast(j, c):
            rows = pl.ds(pl.multiple_of(j * blk, blk), blk)
            kf = k_ref[rows, :]
            kb[rows, :] = kf.astype(BF16)
            vb[rows, :] = v_ref[rows, :].astype(BF16)
            if mode == "moba":
                kmean[pl.ds(j, 1), :] = jnp.sum(kf, axis=0, keepdims=True) * (1.0 / blk)
            return c
        lax.fori_loop(0, nq, cast, 0)
        for m in range(2):
            col = 2 * g + m
            bias[m, 0] = _bias_from_buckets(bk_ref[0], t5_ref, col) * LOG2E
            bias[m, 1] = _bias_from_buckets(bk_ref[1], t5_ref, col) * LOG2E
            bias[m, 2] = jnp.full((blk, blk), t5_ref[N_BUCKETS - 1, col] * LOG2E, F32)

    lane = lax.broadcasted_iota(jnp.int32, (blk, 128), 1)
    upper = lane >= 64
    for h2 in range(2):
        q = q_ref[h2 * blk:(h2 + 1) * blk, :]
        qb = 2 * qi + h2
        for m in range(2):
            c = 2 * h2 + m
            qm = jnp.where(upper, q, 0.0) if m else jnp.where(upper, 0.0, q)
            qs_s[c] = (qm * (0.125 * LOG2E)).astype(BF16)
            m_s[c] = jnp.full((blk, 128), NEG, F32)
            l_s[c] = jnp.zeros((blk, 128), F32)
            acc_s[c] = jnp.zeros((blk, 128), F32)
            if mode == "moba":
                nb = kmean.shape[0]
                gate = lax.dot_general(qm, kmean[...], (((1,), (1,)), ((), ())), precision=HI,
                                       preferred_element_type=F32)
                bi = lax.broadcasted_iota(jnp.int32, (blk, nb), 1)
                gt = jnp.where(bi < qb, gate, -jnp.inf)
                sel = jnp.zeros((blk, nb), F32)
                for _ in range(MOBA_TOPK):
                    mx = jnp.max(gt, axis=1, keepdims=True)
                    cand = jnp.where((gt == mx) & (mx > -jnp.inf), bi, nb)
                    first = jnp.min(cand, axis=1, keepdims=True)
                    pick = bi == first
                    sel = jnp.where(pick, 1.0, sel)
                    gt = jnp.where(pick, -jnp.inf, gt)
                sel_s[c] = jnp.zeros((blk, 128), BF16)
                sel_s[c, :, 0:nb] = sel.astype(BF16)

    def chain(c, j, dd, masked):
        m = c % 2
        rows = pl.ds(pl.multiple_of(j * blk, blk), blk)
        s = lax.dot_general(qs_s[c], kb[rows, :], NT, preferred_element_type=F32)
        if masked:
            oh = jnp.where(lax.broadcasted_iota(jnp.int32, (128, 128), 0) == j, 1.0, 0.0).astype(BF16)
            hit = _dot(sel_s[c], oh)
        yield
        s = s + bias[m, dd]
        if masked:
            col = (hit - 1.0) * (-NEG)
            s = s + jnp.concatenate([col, col], axis=1)
        mp = m_s[c]
        mn = jnp.maximum(mp, jnp.max(s, axis=1, keepdims=True))
        yield
        alpha = jnp.exp2(mp - mn)
        p = jnp.exp2(s - jnp.concatenate([mn, mn], axis=1))
        pv = _dot(p.astype(BF16), vb[rows, :])
        yield
        l_s[c] = alpha * l_s[c] + jnp.sum(p, axis=1, keepdims=True)
        acc_s[c] = alpha * acc_s[c] + pv
        m_s[c] = mn

    _round_robin([chain(2 + m, 2 * qi + 1, 0, False) for m in range(2)])
    masked = mode == "moba"
    _round_robin([chain(m, 2 * qi, 0, False) for m in range(2)]
                 + [chain(2 + m, 2 * qi, 1, masked) for m in range(2)])

    def body(d, carry):
        _round_robin([chain(c, 2 * qi - d, jnp.minimum(d + c // 2, 2), masked) for c in range(4)])
        return carry
    lax.fori_loop(1, 2 * qi + 1, body, 0)

    for h2 in range(2):
        o0 = acc_s[2 * h2] / l_s[2 * h2]
        o1 = acc_s[2 * h2 + 1] / l_s[2 * h2 + 1]
        if mode == "moba":
            out = jnp.where(upper, o1, o0)
        else:
            att = o0 - lam_ref[0] * o1
            out = _rms(att, sub_ref[...], 1e-5) * (1.0 - lam_init)
        o_ref[h2 * blk:(h2 + 1) * blk, :] = out


def attn_prompt(q, k, v, t5_bias, batch, mode, lam=None, subln_w=None, lam_init=0.0):
    n, c = q.shape
    t = n // batch
    nq = t // ATT_BLOCK
    ng = c // 128
    blk = ATT_BLOCK
    smem = pl.BlockSpec(memory_space=pltpu.SMEM)
    nsteps = nq // 2
    qspec = pl.BlockSpec((2 * blk, 128), lambda b, g, i: (b * nsteps + i, g))
    kvspec = pl.BlockSpec((t, 128), lambda b, g, i: (b, g))
    bkspec = pl.BlockSpec((2, blk, blk), lambda b, g, i: (0, 0, 0))
    scratch = [pltpu.VMEM((t, 128), BF16), pltpu.VMEM((t, 128), BF16), pltpu.VMEM((2, 3, blk, blk), F32),
               pltpu.VMEM((4, blk, 128), BF16)] + [pltpu.VMEM((4, blk, 128), F32)] * 3
    bk = jnp.asarray(_prompt_bucket_table())
    if mode == "moba":
        in_specs = [smem, qspec, kvspec, kvspec, bkspec]
        args = (t5_bias, q, k, v, bk)
        scratch += [pltpu.VMEM((nq, 128), F32), pltpu.VMEM((4, blk, 128), BF16)]
    else:
        in_specs = [smem, smem, qspec, kvspec, kvspec, bkspec, pl.BlockSpec((1, 128), lambda b, g, i: (0, 0))]
        args = (t5_bias, lam.reshape(1), q, k, v, bk, subln_w.reshape(1, 128))
    return pl.pallas_call(
        functools.partial(_attn_kernel, mode=mode, nq=nq, lam_init=lam_init),
        grid=(batch, ng, nsteps),
        in_specs=in_specs,
        out_specs=qspec,
        out_shape=jax.ShapeDtypeStruct((n, c), F32),
        scratch_shapes=scratch,
        compiler_params=_cp("arbitrary", "arbitrary", "arbitrary"),
        name="attn_" + mode,
    )(*args)


RWKV_COLS = 1792
RWKV_LN_EPS = 64e-5


def _seg_ones(n):
    r = lax.broadcasted_iota(jnp.int32, (n, n), 0) // 64
    c = lax.broadcasted_iota(jnp.int32, (n, n), 1) // 64
    return jnp.where(r == c, 1.0, 0.0).astype(F32)


def _rwkv_prep_math(z, z_prev, mu_ref, w0_ref, w2_ref, a0_ref, a2_ref, g2_ref, kk_ref, ka_ref, outs):
    r_ref, e_ref, kkn_ref, ab_ref, k2_ref, v_ref, gate_ref = outs
    g = GROUP
    zs = z + mu_ref[...] * (z_prev - z)
    k = zs[:, g:2 * g]
    lora = zs[:, 3 * g:3 * g + 128]
    wlin = w0_ref[...] + _dot(jnp.tanh(lora).astype(BF16), w2_ref[...])
    softplus_neg = jnp.maximum(-wlin, 0.0) + jnp.log1p(jnp.exp(-jnp.abs(wlin)))
    a = _sigmoid(a0_ref[...] + _dot(lora.astype(BF16), a2_ref[...]))
    kk = k * kk_ref[...]
    norm = jnp.sqrt(_seg_sum(kk * kk, g))
    kk = kk / jnp.maximum(norm, 1e-12)
    r_ref[...] = zs[:, :g]
    e_ref[...] = jnp.exp(-softplus_neg - 0.5)
    kkn_ref[...] = kk
    ab_ref[...] = kk * a
    k2_ref[...] = k * (1.0 + (a - 1.0) * ka_ref[...])
    v_ref[...] = zs[:, 2 * g:3 * g]
    gate_ref[...] = _dot(_sigmoid(zs[:, 3 * g + 128:]).astype(BF16), g2_ref[...])


def _rwkv_prep_kernel(z_ref, mu_ref, w0_ref, w2_ref, a0_ref, a2_ref, g2_ref, kk_ref, ka_ref, *rest):
    outs, zcarry = rest[:7], rest[7]
    t = pl.program_id(1)

    @pl.when(t == 0)
    def _():
        zcarry[...] = jnp.zeros_like(zcarry)

    z = z_ref[...]
    z_prev = _shift_rows(z, 1, [zcarry[7:8, :]])
    zcarry[...] = z[z.shape[0] - 8:, :]
    _rwkv_prep_math(z, z_prev, mu_ref, w0_ref, w2_ref, a0_ref, a2_ref, g2_ref, kk_ref, ka_ref, outs)


def _rwkv_prep_step_kernel(z_ref, zp_ref, mu_ref, w0_ref, w2_ref, a0_ref, a2_ref, g2_ref, kk_ref, ka_ref, *outs):
    _rwkv_prep_math(z_ref[...], zp_ref[...], mu_ref, w0_ref, w2_ref, a0_ref, a2_ref, g2_ref, kk_ref, ka_ref, outs)


def _rwkv_prep_params(mu, w0, w2, a0, a2, g2, k_k, k_a):
    g = GROUP
    zero = jnp.zeros_like(w2)
    return (mu.reshape(1, RWKV_COLS), w0.reshape(1, g), jnp.concatenate([w2, zero], 0).astype(BF16),
            a0.reshape(1, g), jnp.concatenate([zero, a2], 0).astype(BF16), g2.astype(BF16),
            k_k.reshape(1, g), k_a.reshape(1, g))


def rwkv_prep(z, batch, params, shift=None, tl=256):
    n = z.shape[0]
    g = GROUP
    out_shape = [jax.ShapeDtypeStruct((n, g), F32)] * 7
    if shift is not None:
        return pl.pallas_call(_rwkv_prep_step_kernel, out_shape=out_shape, name="rwkv_prep_step")(z, shift, *params)
    nt = n // batch // tl
    row = lambda b, t: (b * nt + t, 0)
    const = lambda b, t: (0, 0)
    vec = pl.BlockSpec((1, g), const)
    lora = pl.BlockSpec((128, g), const)
    return pl.pallas_call(
        _rwkv_prep_kernel,
        grid=(batch, nt),
        in_specs=[pl.BlockSpec((tl, RWKV_COLS), row), pl.BlockSpec((1, RWKV_COLS), const),
                  vec, lora, vec, lora, lora, vec, vec],
        out_specs=[pl.BlockSpec((tl, g), row)] * 7,
        out_shape=out_shape,
        scratch_shapes=[pltpu.VMEM((8, RWKV_COLS), F32)],
        compiler_params=_cp("arbitrary", "arbitrary"),
        name="rwkv_prep",
    )(z, *params)


NN = (((1,), (0,)), ((), ()))
NT = (((1,), (1,)), ((), ()))
TN = (((0,), (0,)), ((), ()))


def _nt(a, b):
    return lax.dot_general(a, b, NT, precision=HI, preferred_element_type=F32)


def _split(x):
    hi = x.astype(BF16)
    return hi, (x - hi.astype(F32)).astype(BF16)


def _mm3(a, b, dims):
    ah, al = a if isinstance(a, tuple) else _split(a)
    bh, bl = b if isinstance(b, tuple) else _split(b)
    dg = functools.partial(lax.dot_general, dimension_numbers=dims, preferred_element_type=F32)
    return dg(ah, bh) + (dg(ah, bl) + dg(al, bh))


def _seg_sum(x, n):
    ones = _seg_ones(n).astype(BF16)
    hi, lo = _split(x)
    lo2 = (x - hi.astype(F32) - lo.astype(F32)).astype(BF16)
    return _dot(hi, ones) + (_dot(lo, ones) + _dot(lo2, ones))


def _rwkv_chunk_pair(r, e, kk, ab, k2, v, c):
    shape = (c, 128)
    upper = lax.broadcasted_iota(jnp.int32, shape, 1) >= 64
    row = lax.broadcasted_iota(jnp.int32, shape, 0)
    cum = e
    d = 1
    while d < c:
        cum = cum + jnp.where(row < d, 0.0, pltpu.roll(cum, d, 0))
        d *= 2
    g_inv = jnp.exp(cum)
    at = -kk * jnp.exp(e - cum)
    bt = ab * g_inv
    kt = k2 * g_inv
    rt = r * jnp.exp(-cum)
    g_end = jnp.exp(-cum[c - 1:c, :])

    def stack(x):
        return jnp.concatenate([jnp.where(upper, 0.0, x), jnp.where(upper, x, 0.0)], axis=0)

    def fold(x):
        return x[:c, :] + x[c:, :]

    a_st, r_st, v_st = _split(stack(at)), _split(stack(rt)), _split(stack(v))
    b2 = _split(jnp.concatenate([bt, bt], axis=0))
    k2s = _split(jnp.concatenate([kt, kt], axis=0))
    ri = lax.broadcasted_iota(jnp.int32, (2 * c, 2 * c), 0)
    ci = lax.broadcasted_iota(jnp.int32, (2 * c, 2 * c), 1)
    same = (ri >= c) == (ci >= c)
    strict = same & (ri > ci)
    incl = same & (ri >= ci)
    lab = jnp.where(strict, _mm3(a_st, b2, NT), 0.0)
    lak = jnp.where(strict, _mm3(a_st, k2s, NT), 0.0)
    rb = _split(jnp.where(incl, _mm3(r_st, b2, NT), 0.0))
    rk = jnp.where(incl, _mm3(r_st, k2s, NT), 0.0)
    x = jnp.where(ri == ci, 1.0, 0.0) + lab
    p = lab
    lv = _mm3(lak, v_st, NN)
    yield
    n = 2
    while n < c:
        ps = _split(p)
        p = _mm3(ps, ps, NN)
        x = x + _mm3(x, p, NN)
        n *= 2
        yield
    xs = _split(x)
    pa_st = _mm3(xs, a_st, NN)
    q_st = _mm3(xs, lv, NN)
    yield
    pa = fold(pa_st)
    q = fold(q_st)
    y1 = rt + fold(_mm3(rb, pa_st, NN))
    y0 = fold(_mm3(rb, q_st, NN) + _mm3(rk, v_st, NN))
    bg = _split(bt * g_end)
    r2 = lax.broadcasted_iota(jnp.int32, (128, 128), 0)
    c2 = lax.broadcasted_iota(jnp.int32, (128, 128), 1)
    same_head = (r2 >= 64) == (c2 >= 64)
    m = jnp.where(r2 == c2, g_end, 0.0) + jnp.where(same_head, _mm3(bg, pa, TN), 0.0)
    nn = jnp.where(same_head, _mm3(bg, q, TN) + _mm3(kt * g_end, v, TN), 0.0)
    return m, nn, y1, y0


def _rwkv_chunk_kernel(r_ref, e_ref, kk_ref, ab_ref, k2_ref, v_ref, m_ref, n_ref, y1_ref, y0_ref, *, c):
    lanes = [slice(g * 128, (g + 1) * 128) for g in range(GROUP // 128)]
    chains = [_rwkv_chunk_pair(r_ref[:, ls], e_ref[:, ls], kk_ref[:, ls], ab_ref[:, ls], k2_ref[:, ls],
                               v_ref[:, ls], c) for ls in lanes]
    for g, (ls, (m, nn, y1, y0)) in enumerate(zip(lanes, _round_robin(chains))):
        m_ref[0, g] = m
        n_ref[0, g] = nn
        y1_ref[:, ls] = y1
        y0_ref[:, ls] = y0


def _rwkv_scan_kernel(m_ref, n_ref, y1_ref, y0_ref, r_ref, k2_ref, v_ref, gate_ref, h0_ref,
                      lnw_ref, lnb_ref, rk_ref, y_ref, hout_ref, h):
    t = pl.program_id(1)

    @pl.when(t == 0)
    def _():
        h[...] = h0_ref[0]

    for g in range(GROUP // 128):
        ls = slice(g * 128, (g + 1) * 128)
        hg = _split(h[g])
        y = _mm3(y1_ref[:, ls], hg, NN) + y0_ref[:, ls]
        h[g] = _mm3(m_ref[0, g], hg, NN) + n_ref[0, g]
        mean = _seg_sum(y, 128) * (1.0 / 64)
        yc = y - mean
        var = _seg_sum(yc * yc, 128) * (1.0 / 64)
        yn = yc * lax.rsqrt(var + RWKV_LN_EPS) * lnw_ref[:, ls] + lnb_ref[:, ls]
        v = v_ref[:, ls]
        bonus = _seg_sum(r_ref[:, ls] * k2_ref[:, ls] * rk_ref[:, ls], 128) * v
        y_ref[:, ls] = (yn + bonus) * gate_ref[:, ls]
    hout_ref[0] = h[...]


def rwkv_scan(r, e, kk, ab, k2, v, gate, s0, ln_w, ln_b, r_k, batch, c):
    n, g = r.shape
    nc = n // batch // c
    npair = g // 128
    row = lambda b, t: (b * nc + t, 0)
    blk = pl.BlockSpec((c, g), row)
    mat = pl.BlockSpec((1, npair, 128, 128), lambda b, t: (b * nc + t, 0, 0, 0))
    m, nn, y1, y0 = pl.pallas_call(
        functools.partial(_rwkv_chunk_kernel, c=c),
        grid=(batch, nc),
        in_specs=[blk] * 6,
        out_specs=[mat, mat, blk, blk],
        out_shape=[jax.ShapeDtypeStruct((batch * nc, npair, 128, 128), F32)] * 2
        + [jax.ShapeDtypeStruct((n, g), F32)] * 2,
        compiler_params=_cp("arbitrary", "arbitrary"),
        name="rwkv_chunk",
    )(r, e, kk, ab, k2, v)
    st = jnp.swapaxes(s0, -1, -2).reshape(batch, npair, 2, 64, 64)
    eye2 = jnp.eye(2, dtype=F32)
    h0 = (st[:, :, :, :, None, :] * eye2[None, None, :, None, :, None]).reshape(batch, npair, 128, 128)
    state = pl.BlockSpec((1, npair, 128, 128), lambda b, t: (b, 0, 0, 0))
    vec = pl.BlockSpec((1, g), lambda b, t: (0, 0))
    y, hout = pl.pallas_call(
        _rwkv_scan_kernel,
        grid=(batch, nc),
        in_specs=[mat, mat, blk, blk, blk, blk, blk, blk, state, vec, vec, vec],
        out_specs=[blk, state],
        out_shape=[jax.ShapeDtypeStruct((n, g), F32), jax.ShapeDtypeStruct((batch, npair, 128, 128), F32)],
        scratch_shapes=[pltpu.VMEM((npair, 128, 128), F32)],
        compiler_params=_cp("arbitrary", "arbitrary"),
        name="rwkv_scan",
    )(m, nn, y1, y0, r, k2, v, gate, h0, ln_w.reshape(1, g), ln_b.reshape(1, g), r_k.reshape(1, g))
    hb = hout.reshape(batch, npair, 2, 64, 2, 64)
    s_last = jnp.stack([hb[:, :, 0, :, 0, :], hb[:, :, 1, :, 1, :]], axis=2).reshape(batch, 2 * npair, 64, 64)
    return y, jnp.swapaxes(s_last, -1, -2)


PAGES_PER_STEP = 16


def _past_bucket_row(first_pos, width, q_pos):
    rel = q_pos - (first_pos + np.arange(width))
    return _t5_bucket_np(rel).astype(np.int32).reshape(1, width)


def _bias_rows(bucket_row, t5t_ref):
    out = jnp.zeros((8, bucket_row.shape[1]), F32)
    for b in range(N_BUCKETS):
        out = jnp.where(bucket_row == b, t5t_ref[:, b:b + 1], out)
    return out


def _page_logits(qcol, kt):
    return jnp.sum(qcol * kt, axis=1)


def _moba_scan_kernel(pt_ref, q_ref, *refs, n_steps, n_blocks):
    pages = refs[:PAGES_PER_STEP]
    lg_ref, idx_ref, gate_s = refs[PAGES_PER_STEP:]
    s = pl.program_id(1)

    @pl.when(s == 0)
    def _():
        gate_s[...] = jnp.zeros_like(gate_s)

    qcol = q_ref[0]
    lane = lax.broadcasted_iota(jnp.int32, (8, 128), 1)
    gate = gate_s[...]
    for i, pg in enumerate(pages):
        lg = _page_logits(qcol, pg[0])
        blk, half = divmod(i, 2)
        lg_ref[0, :, blk, half * PAGE:(half + 1) * PAGE] = lg * 0.125
        gate = gate + jnp.where(lane == s * (PAGES_PER_STEP // 2) + blk, jnp.sum(lg, axis=1, keepdims=True), 0.0)
    gate_s[...] = gate

    @pl.when(s == n_steps - 1)
    def _():
        gt = jnp.where(lane < n_blocks, gate * (1.0 / ATT_BLOCK), -jnp.inf)
        out = jnp.zeros((8, 128), jnp.int32)
        for j in range(MOBA_TOPK):
            mx = jnp.max(gt, axis=1, keepdims=True)
            first = jnp.min(jnp.where(gt == mx, lane, 128), axis=1, keepdims=True)
            out = jnp.where(lane == j, first, out)
            gt = jnp.where(lane == first, -jnp.inf, gt)
        idx_ref[0] = out


def _moba_gather_kernel(idx_ref, pt_ref, t5_ref, q_ref, kn_ref, vn_ref, lg_ref, bk_ref, *refs, n_blocks):
    vpages = refs[:2 * MOBA_TOPK]
    o_ref = refs[2 * MOBA_TOPK]
    b = pl.program_id(0)
    h = pl.program_id(1)
    q = q_ref[0, 0]
    own = jnp.sum(q * kn_ref[0, 0], axis=1, keepdims=True) * 0.125 + t5_ref[0, h]
    far = t5_ref[N_BUCKETS - 1, h]
    near = far
    for bkt in range(N_BUCKETS):
        near = jnp.where(bk_ref[...] == bkt, t5_ref[bkt, h], near)
    logits = []
    for s in range(MOBA_TOPK):
        blk = idx_ref[b, h * MOBA_TOPK + s]
        lg = lg_ref[0, 0, pl.ds(blk, 1), :]
        logits.append(lg + jnp.where(blk == n_blocks - 1, near, far))
    mx = own
    for lg in logits:
        mx = jnp.maximum(mx, jnp.max(lg, axis=1, keepdims=True))
    p_own = jnp.exp(own - mx)
    den = p_own
    acc = p_own * vn_ref[0, 0]
    for s, lg in enumerate(logits):
        p = jnp.exp(lg - mx)
        den = den + jnp.sum(p, axis=1, keepdims=True)
        for pg in range(2):
            acc = acc + _nt(p[:, pg * PAGE:(pg + 1) * PAGE], vpages[s * 2 + pg][0, 0])
    o_ref[0, 0] = acc / den


def moba_step(q, k_new, v_new, cache_k, cache_v, page_table, t5_bias):
    nb, c = q.shape
    nh = c // 64
    n_pages = page_table.shape[1]
    n_steps = n_pages // PAGES_PER_STEP
    n_blocks = n_pages * PAGE // ATT_BLOCK
    bps = PAGES_PER_STEP // 2
    qcol = jnp.broadcast_to(q.reshape(nb, nh, 64, 1), (nb, nh, 64, PAGE))

    def page_spec(i):
        return pl.BlockSpec((1, nh, 64, PAGE), lambda b, s, pt: (pt[b, s * PAGES_PER_STEP + i], 0, 0, 0))

    logits, idx = pl.pallas_call(
        functools.partial(_moba_scan_kernel, n_steps=n_steps, n_blocks=n_blocks),
        grid_spec=pltpu.PrefetchScalarGridSpec(
            num_scalar_prefetch=1,
            grid=(nb, n_steps),
            in_specs=[pl.BlockSpec((1, nh, 64, PAGE), lambda b, s, pt: (b, 0, 0, 0))]
            + [page_spec(i) for i in range(PAGES_PER_STEP)],
            out_specs=[pl.BlockSpec((1, nh, bps, ATT_BLOCK), lambda b, s, pt: (b, 0, s, 0)),
                       pl.BlockSpec((1, nh, 128), lambda b, s, pt: (b, 0, 0))],
            scratch_shapes=[pltpu.VMEM((nh, 128), F32)]),
        out_shape=[jax.ShapeDtypeStruct((nb, nh, n_blocks, ATT_BLOCK), F32),
                   jax.ShapeDtypeStruct((nb, nh, 128), jnp.int32)],
        compiler_params=_cp("arbitrary", "arbitrary"),
        name="moba_scan",
    )(page_table, qcol, *([cache_k] * PAGES_PER_STEP))
    sel = idx[:, :, :MOBA_TOPK].reshape(nb, nh * MOBA_TOPK)
    bucket = jnp.asarray(_past_bucket_row((n_blocks - 1) * ATT_BLOCK, ATT_BLOCK, n_pages * PAGE))

    def vpage_spec(s, pg):
        def imap(b, h, sel, pt):
            return (pt[b, 2 * sel[b, h * MOBA_TOPK + s] + pg], h, 0, 0)
        return pl.BlockSpec((1, 1, 64, PAGE), imap)

    row = pl.BlockSpec((1, 1, 1, 64), lambda b, h, sel, pt: (b, h, 0, 0))
    out = pl.pallas_call(
        functools.partial(_moba_gather_kernel, n_blocks=n_blocks),
        grid_spec=pltpu.PrefetchScalarGridSpec(
            num_scalar_prefetch=2,
            grid=(nb, nh),
            in_specs=[pl.BlockSpec(memory_space=pltpu.SMEM), row, row, row,
                      pl.BlockSpec((1, 1, n_blocks, ATT_BLOCK), lambda b, h, sel, pt: (b, h, 0, 0)),
                      pl.BlockSpec((1, ATT_BLOCK), lambda b, h, sel, pt: (0, 0))]
            + [vpage_spec(s, pg) for s in range(MOBA_TOPK) for pg in range(2)],
            out_specs=row),
        out_shape=jax.ShapeDtypeStruct((nb, nh, 1, 64), F32),
        compiler_params=_cp("arbitrary", "arbitrary"),
        name="moba_gather",
    )(sel, page_table, t5_bias, q.reshape(nb, nh, 1, 64), k_new.reshape(nb, nh, 1, 64),
      v_new.reshape(nb, nh, 1, 64), logits, bucket, *([cache_v] * (2 * MOBA_TOPK)))
    return out.reshape(nb, c)


def _diff_step_kernel(pt_ref, lam_ref, qcol_ref, q_ref, kn_ref, vn_ref, t5t_ref, bk_ref, sub_ref, *refs,
                      n_steps, lam_init):
    kpages = refs[:PAGES_PER_STEP]
    vpages = refs[PAGES_PER_STEP:2 * PAGES_PER_STEP]
    o_ref, m_s, l_s, acc_s = refs[2 * PAGES_PER_STEP:]
    s = pl.program_id(1)
    nh = vn_ref.shape[1]
    head_of_row = lax.broadcasted_iota(jnp.int32, (2 * nh, 128), 0) // 2

    def per_head(rows):
        out = jnp.broadcast_to(rows[0], (2 * nh, 128))
        for h in range(1, nh):
            out = jnp.where(head_of_row == h, rows[h], out)
        return out

    @pl.when(s == 0)
    def _():
        m_s[...] = jnp.sum(q_ref[0] * kn_ref[0], axis=1, keepdims=True) * 0.125 + t5t_ref[:, 0:1]
        l_s[...] = jnp.ones_like(l_s)
        vn = vn_ref[0]
        acc_s[...] = per_head([vn[h:h + 1, :] for h in range(nh)])

    far = t5t_ref[:, N_BUCKETS - 1:N_BUCKETS]
    near = _bias_rows(bk_ref[...], t5t_ref)
    qcol = qcol_ref[0]
    m, l, acc = m_s[...], l_s[...], acc_s[...]
    for i in range(PAGES_PER_STEP):
        lg = _page_logits(qcol, kpages[i][0]) * 0.125
        if i == PAGES_PER_STEP - 1:
            lg = lg + jnp.where(s == n_steps - 1, near, far)
        else:
            lg = lg + far
        mn = jnp.maximum(m, jnp.max(lg, axis=1, keepdims=True))
        alpha = jnp.exp(m - mn)
        p = jnp.exp(lg - mn)
        l = alpha * l + jnp.sum(p, axis=1, keepdims=True)
        pb = p.astype(BF16)
        pv = per_head([_dot(pb, vpages[i][0, :, h, :].astype(BF16)) for h in range(nh)])
        acc = alpha * acc + pv
        m = mn
    m_s[...], l_s[...], acc_s[...] = m, l, acc

    @pl.when(s == n_steps - 1)
    def _():
        a = acc / l
        for h in range(nh):
            att = a[2 * h:2 * h + 1, :] - lam_ref[0] * a[2 * h + 1:2 * h + 2, :]
            o_ref[0, h:h + 1, :] = _rms(att, sub_ref[...], 1e-5) * (1.0 - lam_init)


def diff_step(q, k_new, v_new, cache_k, cache_v, page_table, t5_bias, lam, subln_w, lam_init):
    nb, c = q.shape
    nm = c // 64
    nh = nm // 2
    n_pages = page_table.shape[1]
    n_steps = n_pages // PAGES_PER_STEP
    bucket = jnp.asarray(_past_bucket_row((n_pages - 1) * PAGE, PAGE, n_pages * PAGE))
    qcol = jnp.broadcast_to(q.reshape(nb, nm, 64, 1), (nb, nm, 64, PAGE))

    def kpage_spec(i):
        return pl.BlockSpec((1, nm, 64, PAGE), lambda b, s, pt: (pt[b, s * PAGES_PER_STEP + i], 0, 0, 0))

    def vpage_spec(i):
        return pl.BlockSpec((1, PAGE, nh, 128), lambda b, s, pt: (pt[b, s * PAGES_PER_STEP + i], 0, 0, 0))

    maps = pl.BlockSpec((1, nm, 64), lambda b, s, pt: (b, 0, 0))
    heads = pl.BlockSpec((1, nh, 128), lambda b, s, pt: (b, 0, 0))
    out = pl.pallas_call(
        functools.partial(_diff_step_kernel, n_steps=n_steps, lam_init=lam_init),
        grid_spec=pltpu.PrefetchScalarGridSpec(
            num_scalar_prefetch=1,
            grid=(nb, n_steps),
            in_specs=[pl.BlockSpec(memory_space=pltpu.SMEM),
                      pl.BlockSpec((1, nm, 64, PAGE), lambda b, s, pt: (b, 0, 0, 0)), maps, maps, heads,
                      pl.BlockSpec((nm, N_BUCKETS), lambda b, s, pt: (0, 0)),
                      pl.BlockSpec((1, PAGE), lambda b, s, pt: (0, 0)),
                      pl.BlockSpec((1, 128), lambda b, s, pt: (0, 0))]
            + [kpage_spec(i) for i in range(PAGES_PER_STEP)] + [vpage_spec(i) for i in range(PAGES_PER_STEP)],
            out_specs=heads,
            scratch_shapes=[pltpu.VMEM((nm, 1), F32), pltpu.VMEM((nm, 1), F32), pltpu.VMEM((nm, 128), F32)]),
        out_shape=jax.ShapeDtypeStruct((nb, nh, 128), F32),
        compiler_params=_cp("arbitrary", "arbitrary"),
        name="diff_step",
    )(page_table, lam.reshape(1), qcol, q.reshape(nb, nm, 64), k_new.reshape(nb, nm, 64), v_new.reshape(nb, nh, 128),
      t5_bias.T, bucket, subln_w.reshape(1, 128), *([cache_k] * PAGES_PER_STEP), *([cache_v] * PAGES_PER_STEP))
    return out.reshape(nb, c)


def rwkv_mix(z, shift0, s0, mu, w0, w2, a0, a2, g2, k_k, k_a, r_k, ln_w, ln_b, batch, step):
    params = _rwkv_prep_params(mu, w0, w2, a0, a2, g2, k_k, k_a)
    if not step:
        outs = rwkv_prep(z, batch, params)
        r, e, kk, ab, k2, v, gate = outs
        return rwkv_scan(r, e, kk, ab, k2, v, gate, s0, ln_w, ln_b, r_k, batch, 64)
    outs = rwkv_prep(z, batch, params, shift=shift0)
    r, e, kk, ab, k2, v, gate = (jnp.pad(o[:, None, :], ((0, 0), (0, 7), (0, 0))).reshape(batch * 8, GROUP)
                                 for o in outs)
    y, s_last = rwkv_scan(r, e, kk, ab, k2, v, gate, s0, ln_w, ln_b, r_k, batch, 8)
    return y.reshape(batch, 8, GROUP)[:, 0], s_last


def _trunk(x, p, batch, step, st, W):
    depth = p.shape[0]
    outs = {k: [] for k in ("moba_k", "moba_v", "lru_conv", "lru_h", "diff_k", "diff_v",
                            "rwkv_shift", "rwkv_s", "ffn_conv")}
    t = x.shape[0] // batch
    for i in range(depth):
        j = i // 2
        if i % 2 == 0:
            gate, xin, q, k, v = norm_linear(x, W["norm_mix_pre"][i], W["even_w_in"][j], (GROUP,) * 5)
            lru_w = (W["lru_conv_w"][j], W["lru_conv_b"][j], W["lru_w_a"][j], W["lru_b_a"][j],
                     W["lru_w_x"][j], W["lru_b_x"][j], W["lru_lambda"][j])
            if step:
                ya, conv, h_last = lru_step(gate, xin, st["lru_conv"][j], st["lru_h"][j], *lru_w)
                yb = moba_step(q, k, v, st["moba_k"][j], st["moba_v"][j], st["page_table"], W["t5_bias"])
            else:
                ya, conv, h_last = lru_prompt(gate, xin, *lru_w, batch)
                yb = attn_prompt(q, k, v, W["t5_bias"], batch, "moba")
            outs["moba_k"].append(k)
            outs["moba_v"].append(v)
            outs["lru_conv"].append(conv)
            outs["lru_h"].append(h_last)
            w_out = W["even_w_out"][j]
        else:
            z, q, k, v = norm_linear(x, W["norm_mix_pre"][i], W["odd_w_in"][j], (RWKV_COLS, GROUP, GROUP, GROUP))
            rw = (W["rwkv_mu"][j], W["rwkv_w0"][j], W["rwkv_w2"][j], W["rwkv_a0"][j], W["rwkv_a2"][j],
                  W["rwkv_g2"][j], W["rwkv_k_k"][j], W["rwkv_k_a"][j], W["rwkv_r_k"][j],
                  W["rwkv_ln_w"][j], W["rwkv_ln_b"][j])
            lam_init = 0.8 - 0.6 * math.exp(-0.3 * i)
            lf = W["diff_lambda"][j]
            lam = jnp.exp(jnp.sum(lf[0] * lf[1])) - jnp.exp(jnp.sum(lf[2] * lf[3])) + lam_init
            if step:
                ya, s_last = rwkv_mix(z, st["rwkv_shift"][j], st["rwkv_s"][j], *rw, batch, True)
                yb = diff_step(q, k, v, st["diff_k"][j], st["diff_v"][j], st["page_table"], W["t5_bias"],
                               lam, W["diff_subln_w"][j], lam_init)
                shift = z
            else:
                s0 = jnp.zeros((batch, 8, 64, 64), F32)
                ya, s_last = rwkv_mix(z, None, s0, *rw, batch, False)
                yb = attn_prompt(q, k, v, W["t5_bias"], batch, "diff", lam=lam, subln_w=W["diff_subln_w"][j],
                                 lam_init=lam_init)
                shift = z.reshape(batch, t, RWKV_COLS)[:, t - 1]
            outs["diff_k"].append(k)
            outs["diff_v"].append(v)
            outs["rwkv_shift"].append(shift)
            outs["rwkv_s"].append(s_last)
            w_out = W["odd_w_out"][j]
        x = out_proj(ya, yb, w_out, x, W["norm_mix_post"][i])
        ffn_w = (W["norm_ffn_pre"][i], W["ffn_w_up"][i], W["ffn_conv_w"][i], W["ffn_conv_b"][i], W["ffn_w_down"][i],
                 W["norm_ffn_post"][i], W["ple_w_gate"][i], W["ple_w_proj"][i])
        if step:
            x, fbuf = ffn_step(x, p[i], *ffn_w, st["ffn_conv"][i])
        else:
            x, fbuf = ffn_prompt(x, p[i], *ffn_w, batch)
        outs["ffn_conv"].append(fbuf)
    return x, {k: jnp.stack(v) for k, v in outs.items()}


def kernel(x_prompt, x_sample, cache_moba_k, cache_moba_v, state_lru_conv, state_lru_h, cache_diff_k, cache_diff_v, state_rwkv_shift, state_rwkv, state_ffn_conv, page_table, p_prompt, p_sample, t5_bias, norm_mix_pre, norm_mix_post, norm_ffn_pre, norm_ffn_post, even_w_in, even_w_out, lru_conv_w, lru_conv_b, lru_w_a, lru_b_a, lru_w_x, lru_b_x, lru_lambda, odd_w_in, odd_w_out, rwkv_mu, rwkv_w0, rwkv_w2, rwkv_a0, rwkv_a2, rwkv_g2, rwkv_k_k, rwkv_k_a, rwkv_r_k, rwkv_ln_w, rwkv_ln_b, diff_lambda, diff_subln_w, ffn_w_up, ffn_conv_w, ffn_conv_b, ffn_w_down, ple_w_proj, ple_w_gate):
    W = dict(t5_bias=t5_bias, norm_mix_pre=norm_mix_pre, norm_mix_post=norm_mix_post,
             norm_ffn_pre=norm_ffn_pre, norm_ffn_post=norm_ffn_post,
             even_w_in=even_w_in, even_w_out=even_w_out, lru_conv_w=lru_conv_w, lru_conv_b=lru_conv_b,
             lru_w_a=lru_w_a, lru_b_a=lru_b_a, lru_w_x=lru_w_x, lru_b_x=lru_b_x, lru_lambda=lru_lambda,
             odd_w_in=odd_w_in, odd_w_out=odd_w_out, rwkv_mu=rwkv_mu, rwkv_w0=rwkv_w0, rwkv_w2=rwkv_w2,
             rwkv_a0=rwkv_a0, rwkv_a2=rwkv_a2, rwkv_g2=rwkv_g2, rwkv_k_k=rwkv_k_k, rwkv_k_a=rwkv_k_a,
             rwkv_r_k=rwkv_r_k, rwkv_ln_w=rwkv_ln_w, rwkv_ln_b=rwkv_ln_b,
             diff_lambda=diff_lambda, diff_subln_w=diff_subln_w,
             ffn_w_up=ffn_w_up, ffn_conv_w=ffn_conv_w, ffn_conv_b=ffn_conv_b, ffn_w_down=ffn_w_down,
             ple_w_proj=ple_w_proj, ple_w_gate=ple_w_gate)
    bp, tp, d = x_prompt.shape
    bs, ts, _ = x_sample.shape
    depth = p_prompt.shape[0]
    n_pp = tp // PAGE
    assert ts == 1, "the sample group is a single-token step"

    yp, P = _trunk(x_prompt.reshape(bp * tp, d), p_prompt.reshape(depth, bp * tp, -1), bp, False, None, W)

    pool = cache_moba_k.shape[1]
    st = dict(moba_k=jnp.transpose(cache_moba_k, (0, 1, 3, 4, 2)), moba_v=jnp.transpose(cache_moba_v, (0, 1, 3, 4, 2)),
              diff_k=jnp.transpose(cache_diff_k, (0, 1, 3, 4, 5, 2)).reshape(-1, pool, 8, 64, PAGE),
              diff_v=cache_diff_v,
              lru_conv=state_lru_conv, lru_h=state_lru_h, rwkv_shift=state_rwkv_shift, rwkv_s=state_rwkv,
              ffn_conv=state_ffn_conv, page_table=page_table)
    ys, S = _trunk(x_sample.reshape(bs * ts, d), p_sample.reshape(depth, bs * ts, -1), bs, True, st, W)

    ne, no = P["moba_k"].shape[0], P["diff_k"].shape[0]
    return (yp.reshape(bp, tp, d), ys.reshape(bs, ts, d),
            P["moba_k"].reshape(ne, bp, n_pp, PAGE, 8, 64), P["moba_v"].reshape(ne, bp, n_pp, PAGE, 8, 64),
            S["moba_k"].reshape(ne, bs, ts, 8, 64), S["moba_v"].reshape(ne, bs, ts, 8, 64),
            P["lru_conv"], S["lru_conv"], P["lru_h"], S["lru_h"],
            P["diff_k"].reshape(no, bp, n_pp, PAGE, 4, 2, 64), P["diff_v"].reshape(no, bp, n_pp, PAGE, 4, 128),
            S["diff_k"].reshape(no, bs, ts, 4, 2, 64), S["diff_v"].reshape(no, bs, ts, 4, 128),
            P["rwkv_shift"], S["rwkv_shift"], P["rwkv_s"], S["rwkv_s"],
            P["ffn_conv"], S["ffn_conv"])
```

```python
import functools
import math

import numpy as np
import jax
import jax.numpy as jnp
from jax import lax
from jax.experimental import pallas as pl
from jax.experimental.pallas import tpu as pltpu

F32 = jnp.float32
BF16 = jnp.bfloat16
HI = lax.Precision.HIGHEST

D_MODEL = 1024
GROUP = 512
PAGE = 128
VMEM_LIMIT = 56 * 1024 * 1024


def _cp(*sem):
    return pltpu.CompilerParams(dimension_semantics=sem, vmem_limit_bytes=VMEM_LIMIT)


def _rms(x, g, eps):
    return x * lax.rsqrt(jnp.mean(x * x, axis=-1, keepdims=True) + eps) * g


def _gelu(x):
    return 0.5 * x * (1.0 + jnp.tanh(math.sqrt(2.0 / math.pi) * (x + 0.044715 * (x * x * x))))


def _sigmoid(x):
    return 1.0 / (1.0 + jnp.exp(-x))


def _dot(a, b):
    return jnp.dot(a, b, preferred_element_type=F32)


def _round_robin(chains):
    done = {}
    while len(done) < len(chains):
        for i, chain in enumerate(chains):
            if i not in done:
                try:
                    next(chain)
                except StopIteration as stop:
                    done[i] = stop.value
    return [done[i] for i in range(len(chains))]


def _shift_rows(x, d, fill):
    r = pltpu.roll(x, d, 0)
    row = lax.broadcasted_iota(jnp.int32, x.shape, 0)
    for i in range(d):
        r = jnp.where(row == i, fill[i], r)
    return r


def _norm_linear_kernel(x_ref, g_ref, w_ref, *out_refs, outs, tm):
    h = _rms(x_ref[...], g_ref[...], 1e-6).astype(BF16)
    ys = {}
    for o_ref, (kind, off, n) in zip(out_refs, outs):
        if (off, n) not in ys:
            ys[(off, n)] = _dot(h, w_ref[:, off:off + n])
        y = ys[(off, n)]
        if kind == "f32":
            o_ref[...] = y
        elif kind == "bf16":
            o_ref[...] = y.astype(BF16)
        elif kind == "pages":
            for pg in range(tm // PAGE):
                o_ref[pg] = y[pg * PAGE:(pg + 1) * PAGE, :].T
        elif kind == "blockmean":
            o_ref[0] = jnp.zeros((8, n), F32)
            for bi in range(tm // ATT_BLOCK):
                o_ref[0, bi:bi + 1, :] = jnp.sum(y[bi * ATT_BLOCK:(bi + 1) * ATT_BLOCK, :], axis=0,
                                                 keepdims=True) * (1.0 / ATT_BLOCK)


def norm_linear(x, g, w, outs):
    n, d = x.shape
    m = w.shape[1]
    tm = min(n, 512)
    specs, shapes = [], []
    for kind, _, s in outs:
        if kind in ("f32", "bf16"):
            specs.append(pl.BlockSpec((tm, s), lambda i: (i, 0)))
            shapes.append(jax.ShapeDtypeStruct((n, s), F32 if kind == "f32" else BF16))
        elif kind == "pages":
            specs.append(pl.BlockSpec((tm // PAGE, s, PAGE), lambda i: (i, 0, 0)))
            shapes.append(jax.ShapeDtypeStruct((n // PAGE, s, PAGE), F32))
        else:
            specs.append(pl.BlockSpec((1, 8, s), lambda i: (i, 0, 0)))
            shapes.append(jax.ShapeDtypeStruct((n // tm, 8, s), F32))
    return pl.pallas_call(
        functools.partial(_norm_linear_kernel, outs=outs, tm=tm),
        grid=(n // tm,),
        in_specs=[pl.BlockSpec((tm, d), lambda i: (i, 0)),
                  pl.BlockSpec((1, d), lambda i: (0, 0)),
                  pl.BlockSpec((d, m), lambda i: (0, 0))],
        out_specs=specs,
        out_shape=shapes,
        compiler_params=_cp("arbitrary"),
        name="norm_linear",
    )(x, g.reshape(1, d), w.astype(BF16))


def _out_proj_kernel(a_ref, b_ref, w_ref, x_ref, g_ref, o_ref):
    y = _dot(a_ref[...].astype(BF16), w_ref[:GROUP, :]) + _dot(b_ref[...].astype(BF16), w_ref[GROUP:, :])
    o_ref[...] = x_ref[...] + _rms(y, g_ref[...], 1e-6)


def out_proj(a, b, w, x, g):
    n, d = x.shape
    tm = min(n, 512)
    return pl.pallas_call(
        _out_proj_kernel,
        grid=(n // tm,),
        in_specs=[pl.BlockSpec((tm, GROUP), lambda i: (i, 0)),
                  pl.BlockSpec((tm, GROUP), lambda i: (i, 0)),
                  pl.BlockSpec((2 * GROUP, d), lambda i: (0, 0)),
                  pl.BlockSpec((tm, d), lambda i: (i, 0)),
                  pl.BlockSpec((1, d), lambda i: (0, 0))],
        out_specs=pl.BlockSpec((tm, d), lambda i: (i, 0)),
        out_shape=jax.ShapeDtypeStruct((n, d), F32),
        compiler_params=_cp("arbitrary"),
        name="out_proj",
    )(a, b, w.astype(BF16), x, g.reshape(1, d))


def _ffn_tail(acc, x, gpost, p, wg_ref, wp_ref):
    x1 = x + _rms(acc, gpost, 1e-6)
    gate = _sigmoid(_dot(x1.astype(BF16), wg_ref[...]))
    return x1 + gate * _dot(p.astype(BF16), wp_ref[...])


def _ffn_kernel(x_ref, p_ref, gpre_ref, wug_ref, wuv_ref, cwg_ref, cwv_ref, cbg_ref, cbv_ref, wd_ref,
                gpost_ref, wg_ref, wp_ref, o_ref, st_ref, hn, acc, carry, *, tm, f, nf):
    t = pl.program_id(1)
    j = pl.program_id(2)

    @pl.when(j == 0)
    def _():
        hn[...] = _rms(x_ref[...], gpre_ref[...], 1e-6).astype(BF16)
        acc[...] = jnp.zeros_like(acc)

    @pl.when(t == 0)
    def _():
        carry[j] = jnp.zeros((8, 2 * f), F32)

    h = hn[...]
    ug = _dot(h, wug_ref[...])
    uv = _dot(h, wuv_ref[...])
    prev = carry[j]

    def conv(u, pv, cw_ref, cb_ref):
        cw = cw_ref[...]
        u1 = _shift_rows(u, 1, [pv[7:8]])
        u2 = _shift_rows(u, 2, [pv[6:7], pv[7:8]])
        return cb_ref[...] + cw[0:1] * u2 + cw[1:2] * u1 + cw[2:3] * u

    cg = conv(ug, prev[:, :f], cwg_ref, cbg_ref)
    cv = conv(uv, prev[:, f:], cwv_ref, cbv_ref)
    carry[j] = jnp.concatenate([ug[tm - 8:, :], uv[tm - 8:, :]], axis=1)
    st_ref[0, 0, 0, 0:1, :] = ug[tm - 2:tm - 1, :]
    st_ref[0, 0, 0, 1:2, :] = uv[tm - 2:tm - 1, :]
    st_ref[0, 0, 1, 0:1, :] = ug[tm - 1:tm, :]
    st_ref[0, 0, 1, 1:2, :] = uv[tm - 1:tm, :]
    act = (_gelu(cg) * cv).astype(BF16)
    acc[...] += _dot(act, wd_ref[...])

    @pl.when(j == nf - 1)
    def _():
        o_ref[...] = _ffn_tail(acc[...], x_ref[...], gpost_ref[...], p_ref[...], wg_ref, wp_ref)


def ffn_prompt(x, p, gpre, w_up, conv_w, conv_b, w_down, gpost, w_gate, w_proj, batch, tm=1024, f=512):
    n, d = x.shape
    dff = w_down.shape[0]
    nf = dff // f
    nt = n // batch // tm
    pd = p.shape[1]
    w_up = w_up.astype(BF16)
    conv_b = conv_b.reshape(1, 2 * dff)
    row = lambda b, t, j: (b * nt + t, 0)
    const = lambda b, t, j: (0, 0)
    out, st = pl.pallas_call(
        functools.partial(_ffn_kernel, tm=tm, f=f, nf=nf),
        grid=(batch, nt, nf),
        in_specs=[pl.BlockSpec((tm, d), row),
                  pl.BlockSpec((tm, pd), row),
                  pl.BlockSpec((1, d), const),
                  pl.BlockSpec((d, f), lambda b, t, j: (0, j)),
                  pl.BlockSpec((d, f), lambda b, t, j: (0, nf + j)),
                  pl.BlockSpec((3, f), lambda b, t, j: (0, j)),
                  pl.BlockSpec((3, f), lambda b, t, j: (0, nf + j)),
                  pl.BlockSpec((1, f), lambda b, t, j: (0, j)),
                  pl.BlockSpec((1, f), lambda b, t, j: (0, nf + j)),
                  pl.BlockSpec((f, d), lambda b, t, j: (j, 0)),
                  pl.BlockSpec((1, d), const),
                  pl.BlockSpec((d, d), const),
                  pl.BlockSpec((pd, d), const)],
        out_specs=[pl.BlockSpec((tm, d), row),
                   pl.BlockSpec((1, 1, 2, 2, f), lambda b, t, j: (b, t, 0, 0, j))],
        out_shape=[jax.ShapeDtypeStruct((n, d), F32),
                   jax.ShapeDtypeStruct((batch, nt, 2, 2, dff), F32)],
        scratch_shapes=[pltpu.VMEM((tm, d), BF16), pltpu.VMEM((tm, d), F32), pltpu.VMEM((nf, 8, 2 * f), F32)],
        compiler_params=_cp("arbitrary", "arbitrary", "arbitrary"),
        name="ffn_prompt",
    )(x, p, gpre.reshape(1, d), w_up, w_up, conv_w, conv_w, conv_b, conv_b, w_down.astype(BF16),
      gpost.reshape(1, d), w_gate.astype(BF16), w_proj.astype(BF16))
    return out, st[:, nt - 1].reshape(batch, 2, 2 * dff)


def _ffn_step_kernel(x_ref, p_ref, gpre_ref, wug_ref, wuv_ref, cwg_ref, cwv_ref, cbg_ref, cbv_ref, wd_ref,
                     gpost_ref, wg_ref, wp_ref, s0g_ref, s0v_ref, s1g_ref, s1v_ref,
                     o_ref, ug_ref, uv_ref, acc, *, nf):
    j = pl.program_id(0)

    @pl.when(j == 0)
    def _():
        acc[...] = jnp.zeros_like(acc)

    h = _rms(x_ref[...], gpre_ref[...], 1e-6).astype(BF16)
    ug = _dot(h, wug_ref[...])
    uv = _dot(h, wuv_ref[...])
    ug_ref[...] = ug
    uv_ref[...] = uv
    cwg = cwg_ref[...]
    cwv = cwv_ref[...]
    cg = cbg_ref[...] + cwg[0:1] * s0g_ref[...] + cwg[1:2] * s1g_ref[...] + cwg[2:3] * ug
    cv = cbv_ref[...] + cwv[0:1] * s0v_ref[...] + cwv[1:2] * s1v_ref[...] + cwv[2:3] * uv
    acc[...] += _dot((_gelu(cg) * cv).astype(BF16), wd_ref[...])

    @pl.when(j == nf - 1)
    def _():
        o_ref[...] = _ffn_tail(acc[...], x_ref[...], gpost_ref[...], p_ref[...], wg_ref, wp_ref)


def ffn_step(x, p, gpre, w_up, conv_w, conv_b, w_down, gpost, w_gate, w_proj, state, f=512):
    n, d = x.shape
    dff = w_down.shape[0]
    nf = dff // f
    pd = p.shape[1]
    w_up = w_up.astype(BF16)
    conv_b = conv_b.reshape(1, 2 * dff)
    s0, s1 = state[:, 0, :], state[:, 1, :]
    const = lambda j: (0, 0)
    lo = lambda j: (0, j)
    hi = lambda j: (0, nf + j)
    out, ug, uv = pl.pallas_call(
        functools.partial(_ffn_step_kernel, nf=nf),
        grid=(nf,),
        in_specs=[pl.BlockSpec((n, d), const),
                  pl.BlockSpec((n, pd), const),
                  pl.BlockSpec((1, d), const),
                  pl.BlockSpec((d, f), lo), pl.BlockSpec((d, f), hi),
                  pl.BlockSpec((3, f), lo), pl.BlockSpec((3, f), hi),
                  pl.BlockSpec((1, f), lo), pl.BlockSpec((1, f), hi),
                  pl.BlockSpec((f, d), lambda j: (j, 0)),
                  pl.BlockSpec((1, d), const),
                  pl.BlockSpec((d, d), const),
                  pl.BlockSpec((pd, d), const),
                  pl.BlockSpec((n, f), lo), pl.BlockSpec((n, f), hi),
                  pl.BlockSpec((n, f), lo), pl.BlockSpec((n, f), hi)],
        out_specs=[pl.BlockSpec((n, d), const), pl.BlockSpec((n, f), lo), pl.BlockSpec((n, f), lo)],
        out_shape=[jax.ShapeDtypeStruct((n, d), F32), jax.ShapeDtypeStruct((n, dff), F32),
                   jax.ShapeDtypeStruct((n, dff), F32)],
        scratch_shapes=[pltpu.VMEM((n, d), F32)],
        compiler_params=_cp("arbitrary"),
        name="ffn_step",
    )(x, p, gpre.reshape(1, d), w_up, w_up, conv_w, conv_w, conv_b, conv_b, w_down.astype(BF16),
      gpost.reshape(1, d), w_gate.astype(BF16), w_proj.astype(BF16), s0, s0, s1, s1)
    new_state = jnp.stack([s1, jnp.concatenate([ug, uv], axis=1)], axis=1)
    return out, new_state


def _lru_gates(xc, wa_ref, ba_ref, wx_ref, bx_ref, lam_ref):
    xb = xc.astype(BF16)
    r = _sigmoid(_dot(xb, wa_ref[...]) + ba_ref[...])
    i = _sigmoid(_dot(xb, wx_ref[...]) + bx_ref[...])
    lam = lam_ref[...]
    softplus_neg = jnp.maximum(-lam, 0.0) + jnp.log1p(jnp.exp(-jnp.abs(lam)))
    log_a = -8.0 * softplus_neg * r
    a = jnp.exp(log_a)
    th = jnp.tanh(log_a)
    u = jnp.sqrt(-2.0 * th / (1.0 - th)) * (i * xc)
    return a, u


def _lru_kernel(gate_ref, x_ref, cw_ref, cb_ref, wa_ref, ba_ref, wx_ref, bx_ref, lam_ref,
                y_ref, conv_ref, hlast_ref, xcarry, hcarry, *, tl):
    t = pl.program_id(1)

    @pl.when(t == 0)
    def _():
        xcarry[...] = jnp.zeros_like(xcarry)
        hcarry[...] = jnp.zeros_like(hcarry)

    x = x_ref[...]
    c = xcarry[...]
    cw = cw_ref[...]
    xs1 = _shift_rows(x, 1, [c[7:8]])
    xs2 = _shift_rows(x, 2, [c[6:7], c[7:8]])
    xs3 = _shift_rows(x, 3, [c[5:6], c[6:7], c[7:8]])
    xc = cb_ref[...] + cw[0:1] * xs3 + cw[1:2] * xs2 + cw[2:3] * xs1 + cw[3:4] * x
    a, u = _lru_gates(xc, wa_ref, ba_ref, wx_ref, bx_ref, lam_ref)
    row = lax.broadcasted_iota(jnp.int32, a.shape, 0)
    d = 1
    while d < tl:
        a_s = jnp.where(row < d, 1.0, pltpu.roll(a, d, 0))
        u_s = jnp.where(row < d, 0.0, pltpu.roll(u, d, 0))
        u = u + a * u_s
        a = a * a_s
        d *= 2
    h = a * hcarry[...] + u
    y_ref[...] = h * _gelu(gate_ref[...])
    hcarry[...] = h[tl - 1:tl, :]
    hlast_ref[0] = h[tl - 1:tl, :]
    xcarry[...] = x[tl - 8:, :]
    conv_ref[0] = x[tl - 3:, :]


def _block_diag(w):
    h, a, b = w.shape
    eye = jnp.eye(h, dtype=w.dtype)
    return (eye[:, None, :, None] * w[:, :, None, :]).reshape(h * a, h * b)


def lru_prompt(gate, x, conv_w, conv_b, w_a, b_a, w_x, b_x, lam, batch, tl=256):
    n, c = x.shape
    nt = n // batch // tl
    row = lambda b, t: (b * nt + t, 0)
    const = lambda b, t: (0, 0)
    vec = pl.BlockSpec((1, c), const)
    y, conv, hlast = pl.pallas_call(
        functools.partial(_lru_kernel, tl=tl),
        grid=(batch, nt),
        in_specs=[pl.BlockSpec((tl, c), row), pl.BlockSpec((tl, c), row),
                  pl.BlockSpec((4, c), const), vec,
                  pl.BlockSpec((c, c), const), vec, pl.BlockSpec((c, c), const), vec, vec],
        out_specs=[pl.BlockSpec((tl, c), row),
                   pl.BlockSpec((1, 3, c), lambda b, t: (b, 0, 0)),
                   pl.BlockSpec((1, 1, c), lambda b, t: (b, 0, 0))],
        out_shape=[jax.ShapeDtypeStruct((n, c), F32), jax.ShapeDtypeStruct((batch, 3, c), F32),
                   jax.ShapeDtypeStruct((batch, 1, c), F32)],
        scratch_shapes=[pltpu.VMEM((8, c), F32), pltpu.VMEM((1, c), F32)],
        compiler_params=_cp("arbitrary", "arbitrary"),
        name="lru_prompt",
    )(gate, x, conv_w, conv_b.reshape(1, c), _block_diag(w_a).astype(BF16), b_a.reshape(1, c),
      _block_diag(w_x).astype(BF16), b_x.reshape(1, c), lam.reshape(1, c))
    return y, conv, hlast.reshape(batch, c)


def _lru_step_kernel(gate_ref, x_ref, s0_ref, s1_ref, s2_ref, h0_ref, cw_ref, cb_ref, wa_ref, ba_ref,
                     wx_ref, bx_ref, lam_ref, y_ref, h_ref):
    cw = cw_ref[...]
    x = x_ref[...]
    xc = cb_ref[...] + cw[0:1] * s0_ref[...] + cw[1:2] * s1_ref[...] + cw[2:3] * s2_ref[...] + cw[3:4] * x
    a, u = _lru_gates(xc, wa_ref, ba_ref, wx_ref, bx_ref, lam_ref)
    h = a * h0_ref[...] + u
    h_ref[...] = h
    y_ref[...] = h * _gelu(gate_ref[...])


def lru_step(gate, x, conv_state, h0, conv_w, conv_b, w_a, b_a, w_x, b_x, lam):
    n, c = x.shape
    y, h = pl.pallas_call(
        _lru_step_kernel,
        out_shape=[jax.ShapeDtypeStruct((n, c), F32), jax.ShapeDtypeStruct((n, c), F32)],
        name="lru_step",
    )(gate, x, conv_state[:, 0], conv_state[:, 1], conv_state[:, 2], h0, conv_w, conv_b.reshape(1, c),
      _block_diag(w_a).astype(BF16), b_a.reshape(1, c), _block_diag(w_x).astype(BF16), b_x.reshape(1, c),
      lam.reshape(1, c))
    new_conv = jnp.stack([conv_state[:, 1], conv_state[:, 2], x], axis=1)
    return y, new_conv, h


N_BUCKETS = 32
T5_MAX_EXACT = 16
T5_MAX_DISTANCE = 128
NEG = -1e30
LOG2E = math.log2(math.e)
ATT_BLOCK = 256
MOBA_TOPK = 3


def _t5_bucket_np(rel):
    n = np.maximum(rel, 0)
    nf = np.maximum(n, 1).astype(np.float32)
    large = T5_MAX_EXACT + (np.log(nf / np.float32(T5_MAX_EXACT)) / np.float32(math.log(T5_MAX_DISTANCE / T5_MAX_EXACT))
                            * np.float32(N_BUCKETS - T5_MAX_EXACT)).astype(np.int32)
    large = np.minimum(large, N_BUCKETS - 1)
    return np.where(n < T5_MAX_EXACT, n, large).astype(np.int32)


def _prompt_bucket_table():
    r = np.arange(ATT_BLOCK)[:, None]
    c = np.arange(ATT_BLOCK)[None, :]
    tabs = []
    for o in range(2):
        rel = o * ATT_BLOCK + r - c
        tabs.append(np.where(rel >= 0, _t5_bucket_np(rel), -1))
    return np.stack(tabs).astype(np.int32)


def _bias_from_buckets(bucket, t5_ref, col):
    out = jnp.full(bucket.shape, NEG, F32)
    for b in range(N_BUCKETS):
        out = jnp.where(bucket == b, t5_ref[b, col], out)
    return out


def _attn_kernel(*refs, mode, nq, lam_init):
    if mode == "moba":
        (t5_ref, q_ref, kb, vb, bk_ref, kmean, o_ref, bias, qs_s, m_s, l_s, acc_s, sel_s) = refs
    else:
        (t5_ref, lam_ref, q_ref, kb, vb, bk_ref, sub_ref, o_ref, bias, qs_s, m_s, l_s, acc_s) = refs
    g = pl.program_id(1)
    qi = pl.program_id(2)
    blk = ATT_BLOCK

    @pl.when(qi == 0)
    def _():
        for m in range(2):
            col = 2 * g + m
            bias[m, 0] = _bias_from_buckets(bk_ref[0], t5_ref, col) * LOG2E
            bias[m, 1] = _bias_from_buckets(bk_ref[1], t5_ref, col) * LOG2E
            bias[m, 2] = jnp.full((blk, blk), t5_ref[N_BUCKETS - 1, col] * LOG2E, F32)

    lane = lax.broadcasted_iota(jnp.int32, (blk, 128), 1)
    upper = lane >= 64
    for h2 in range(2):
        q = q_ref[h2 * blk:(h2 + 1) * blk, :]
        qb = 2 * qi + h2
        for m in range(2):
            c = 2 * h2 + m
            qm = jnp.where(upper, q, 0.0) if m else jnp.where(upper, 0.0, q)
            qs_s[c] = (qm * (0.125 * LOG2E)).astype(BF16)
            m_s[c] = jnp.full((blk, 128), NEG, F32)
            l_s[c] = jnp.zeros((blk, 128), F32)
            acc_s[c] = jnp.zeros((blk, 128), F32)
            if mode == "moba":
                nb = kmean.shape[0]
                gate = lax.dot_general(qm, kmean[...], (((1,), (1,)), ((), ())), precision=HI,
                                       preferred_element_type=F32)
                bi = lax.broadcasted_iota(jnp.int32, (blk, nb), 1)
                gt = jnp.where(bi < qb, gate, -jnp.inf)
                sel = jnp.zeros((blk, nb), F32)
                for _ in range(MOBA_TOPK):
                    mx = jnp.max(gt, axis=1, keepdims=True)
                    cand = jnp.where((gt == mx) & (mx > -jnp.inf), bi, nb)
                    first = jnp.min(cand, axis=1, keepdims=True)
                    pick = bi == first
                    sel = jnp.where(pick, 1.0, sel)
                    gt = jnp.where(pick, -jnp.inf, gt)
                sel_s[c] = jnp.zeros((blk, 128), BF16)
                sel_s[c, :, 0:nb] = sel.astype(BF16)

    def chain(c, j, dd, masked):
        m = c % 2
        rows = pl.ds(pl.multiple_of(j * blk, blk), blk)
        s = lax.dot_general(qs_s[c], kb[rows, :], NT, preferred_element_type=F32)
        if masked:
            oh = jnp.where(lax.broadcasted_iota(jnp.int32, (128, 128), 0) == j, 1.0, 0.0).astype(BF16)
            hit = _dot(sel_s[c], oh)
        yield
        s = s + bias[m, dd]
        if masked:
            col = (hit - 1.0) * (-NEG)
            s = s + jnp.concatenate([col, col], axis=1)
        mp = m_s[c]
        mn = jnp.maximum(mp, jnp.max(s, axis=1, keepdims=True))
        yield
        alpha = jnp.exp2(mp - mn)
        p = jnp.exp2(s - jnp.concatenate([mn, mn], axis=1))
        pv = _dot(p.astype(BF16), vb[rows, :])
        yield
        l_s[c] = alpha * l_s[c] + jnp.sum(p, axis=1, keepdims=True)
        acc_s[c] = alpha * acc_s[c] + pv
        m_s[c] = mn

    _round_robin([chain(2 + m, 2 * qi + 1, 0, False) for m in range(2)])
    masked = mode == "moba"
    _round_robin([chain(m, 2 * qi, 0, False) for m in range(2)]
                 + [chain(2 + m, 2 * qi, 1, masked) for m in range(2)])

    def body(d, carry):
        _round_robin([chain(c, 2 * qi - d, jnp.minimum(d + c // 2, 2), masked) for c in range(4)])
        return carry
    lax.fori_loop(1, 2 * qi + 1, body, 0)

    for h2 in range(2):
        o0 = acc_s[2 * h2] / l_s[2 * h2]
        o1 = acc_s[2 * h2 + 1] / l_s[2 * h2 + 1]
        if mode == "moba":
            out = jnp.where(upper, o1, o0)
        else:
            att = o0 - lam_ref[0] * o1
            out = _rms(att, sub_ref[...], 1e-5) * (1.0 - lam_init)
        o_ref[h2 * blk:(h2 + 1) * blk, :] = out


def attn_prompt(q, k, v, t5_bias, batch, mode, kmean=None, lam=None, subln_w=None, lam_init=0.0):
    n, c = q.shape
    t = n // batch
    nq = t // ATT_BLOCK
    ng = c // 128
    blk = ATT_BLOCK
    smem = pl.BlockSpec(memory_space=pltpu.SMEM)
    nsteps = nq // 2
    qspec = pl.BlockSpec((2 * blk, 128), lambda b, g, i: (b * nsteps + i, g))
    kvspec = pl.BlockSpec((t, 128), lambda b, g, i: (b, g))
    bkspec = pl.BlockSpec((2, blk, blk), lambda b, g, i: (0, 0, 0))
    scratch = [pltpu.VMEM((2, 3, blk, blk), F32),
               pltpu.VMEM((4, blk, 128), BF16)] + [pltpu.VMEM((4, blk, 128), F32)] * 3
    bk = jnp.asarray(_prompt_bucket_table())
    if mode == "moba":
        in_specs = [smem, qspec, kvspec, kvspec, bkspec, pl.BlockSpec((nq, 128), lambda b, g, i: (b, g))]
        args = (t5_bias, q, k, v, bk, kmean)
        scratch += [pltpu.VMEM((4, blk, 128), BF16)]
    else:
        in_specs = [smem, smem, qspec, kvspec, kvspec, bkspec, pl.BlockSpec((1, 128), lambda b, g, i: (0, 0))]
        args = (t5_bias, lam.reshape(1), q, k, v, bk, subln_w.reshape(1, 128))
    return pl.pallas_call(
        functools.partial(_attn_kernel, mode=mode, nq=nq, lam_init=lam_init),
        grid=(batch, ng, nsteps),
        in_specs=in_specs,
        out_specs=qspec,
        out_shape=jax.ShapeDtypeStruct((n, c), F32),
        scratch_shapes=scratch,
        compiler_params=_cp("arbitrary", "arbitrary", "arbitrary"),
        name="attn_" + mode,
    )(*args)


RWKV_COLS = 1792
RWKV_LN_EPS = 64e-5


def _seg_ones(n):
    r = lax.broadcasted_iota(jnp.int32, (n, n), 0) // 64
    c = lax.broadcasted_iota(jnp.int32, (n, n), 1) // 64
    return jnp.where(r == c, 1.0, 0.0).astype(F32)


def _rwkv_prep_math(z, z_prev, mu_ref, w0_ref, w2_ref, a0_ref, a2_ref, g2_ref, kk_ref, ka_ref, outs):
    r_ref, e_ref, kkn_ref, ab_ref, k2_ref, v_ref, gate_ref = outs
    g = GROUP
    zs = z + mu_ref[...] * (z_prev - z)
    k = zs[:, g:2 * g]
    lora = zs[:, 3 * g:3 * g + 128]
    wlin = w0_ref[...] + _dot(jnp.tanh(lora).astype(BF16), w2_ref[...])
    softplus_neg = jnp.maximum(-wlin, 0.0) + jnp.log1p(jnp.exp(-jnp.abs(wlin)))
    a = _sigmoid(a0_ref[...] + _dot(lora.astype(BF16), a2_ref[...]))
    kk = k * kk_ref[...]
    norm = jnp.sqrt(_seg_sum(kk * kk, g))
    kk = kk / jnp.maximum(norm, 1e-12)
    r_ref[...] = zs[:, :g]
    e_ref[...] = jnp.exp(-softplus_neg - 0.5)
    kkn_ref[...] = kk
    ab_ref[...] = kk * a
    k2_ref[...] = k * (1.0 + (a - 1.0) * ka_ref[...])
    v_ref[...] = zs[:, 2 * g:3 * g]
    gate_ref[...] = _dot(_sigmoid(zs[:, 3 * g + 128:]).astype(BF16), g2_ref[...])


def _rwkv_prep_kernel(z_ref, mu_ref, w0_ref, w2_ref, a0_ref, a2_ref, g2_ref, kk_ref, ka_ref, *rest):
    outs, zcarry = rest[:7], rest[7]
    t = pl.program_id(1)

    @pl.when(t == 0)
    def _():
        zcarry[...] = jnp.zeros_like(zcarry)

    z = z_ref[...]
    z_prev = _shift_rows(z, 1, [zcarry[7:8, :]])
    zcarry[...] = z[z.shape[0] - 8:, :]
    _rwkv_prep_math(z, z_prev, mu_ref, w0_ref, w2_ref, a0_ref, a2_ref, g2_ref, kk_ref, ka_ref, outs)


def _rwkv_prep_step_kernel(z_ref, zp_ref, mu_ref, w0_ref, w2_ref, a0_ref, a2_ref, g2_ref, kk_ref, ka_ref, *outs):
    _rwkv_prep_math(z_ref[...], zp_ref[...], mu_ref, w0_ref, w2_ref, a0_ref, a2_ref, g2_ref, kk_ref, ka_ref, outs)


def _rwkv_prep_params(mu, w0, w2, a0, a2, g2, k_k, k_a):
    g = GROUP
    zero = jnp.zeros_like(w2)
    return (mu.reshape(1, RWKV_COLS), w0.reshape(1, g), jnp.concatenate([w2, zero], 0).astype(BF16),
            a0.reshape(1, g), jnp.concatenate([zero, a2], 0).astype(BF16), g2.astype(BF16),
            k_k.reshape(1, g), k_a.reshape(1, g))


def rwkv_prep(z, batch, params, shift=None, tl=256):
    n = z.shape[0]
    g = GROUP
    out_shape = [jax.ShapeDtypeStruct((n, g), F32)] * 7
    if shift is not None:
        return pl.pallas_call(_rwkv_prep_step_kernel, out_shape=out_shape, name="rwkv_prep_step")(z, shift, *params)
    nt = n // batch // tl
    row = lambda b, t: (b * nt + t, 0)
    const = lambda b, t: (0, 0)
    vec = pl.BlockSpec((1, g), const)
    lora = pl.BlockSpec((128, g), const)
    return pl.pallas_call(
        _rwkv_prep_kernel,
        grid=(batch, nt),
        in_specs=[pl.BlockSpec((tl, RWKV_COLS), row), pl.BlockSpec((1, RWKV_COLS), const),
                  vec, lora, vec, lora, lora, vec, vec],
        out_specs=[pl.BlockSpec((tl, g), row)] * 7,
        out_shape=out_shape,
        scratch_shapes=[pltpu.VMEM((8, RWKV_COLS), F32)],
        compiler_params=_cp("arbitrary", "arbitrary"),
        name="rwkv_prep",
    )(z, *params)


NN = (((1,), (0,)), ((), ()))
NT = (((1,), (1,)), ((), ()))
TN = (((0,), (0,)), ((), ()))


def _nt(a, b):
    return lax.dot_general(a, b, NT, precision=HI, preferred_element_type=F32)


def _split(x):
    hi = x.astype(BF16)
    return hi, (x - hi.astype(F32)).astype(BF16)


def _mm3(a, b, dims):
    ah, al = a if isinstance(a, tuple) else _split(a)
    bh, bl = b if isinstance(b, tuple) else _split(b)
    dg = functools.partial(lax.dot_general, dimension_numbers=dims, preferred_element_type=F32)
    return dg(ah, bh) + (dg(ah, bl) + dg(al, bh))


def _seg_sum(x, n):
    ones = _seg_ones(n).astype(BF16)
    hi, lo = _split(x)
    lo2 = (x - hi.astype(F32) - lo.astype(F32)).astype(BF16)
    return _dot(hi, ones) + (_dot(lo, ones) + _dot(lo2, ones))


def _rwkv_chunk_pair(r, e, kk, ab, k2, v, c):
    shape = (c, 128)
    upper = lax.broadcasted_iota(jnp.int32, shape, 1) >= 64
    row = lax.broadcasted_iota(jnp.int32, shape, 0)
    cum = e
    d = 1
    while d < c:
        cum = cum + jnp.where(row < d, 0.0, pltpu.roll(cum, d, 0))
        d *= 2
    g_inv = jnp.exp(cum)
    at = -kk * jnp.exp(e - cum)
    bt = ab * g_inv
    kt = k2 * g_inv
    rt = r * jnp.exp(-cum)
    g_end = jnp.exp(-cum[c - 1:c, :])

    def stack(x):
        return jnp.concatenate([jnp.where(upper, 0.0, x), jnp.where(upper, x, 0.0)], axis=0)

    def fold(x):
        return x[:c, :] + x[c:, :]

    a_st, r_st, v_st = _split(stack(at)), _split(stack(rt)), _split(stack(v))
    b2 = _split(jnp.concatenate([bt, bt], axis=0))
    k2s = _split(jnp.concatenate([kt, kt], axis=0))
    ri = lax.broadcasted_iota(jnp.int32, (2 * c, 2 * c), 0)
    ci = lax.broadcasted_iota(jnp.int32, (2 * c, 2 * c), 1)
    same = (ri >= c) == (ci >= c)
    strict = same & (ri > ci)
    incl = same & (ri >= ci)
    lab = jnp.where(strict, _mm3(a_st, b2, NT), 0.0)
    lak = jnp.where(strict, _mm3(a_st, k2s, NT), 0.0)
    rb = _split(jnp.where(incl, _mm3(r_st, b2, NT), 0.0))
    rk = jnp.where(incl, _mm3(r_st, k2s, NT), 0.0)
    x = jnp.where(ri == ci, 1.0, 0.0) + lab
    p = lab
    lv = _mm3(lak, v_st, NN)
    yield
    n = 2
    while n < c:
        ps = _split(p)
        p = _mm3(ps, ps, NN)
        x = x + _mm3(x, p, NN)
        n *= 2
        yield
    xs = _split(x)
    pa_st = _mm3(xs, a_st, NN)
    q_st = _mm3(xs, lv, NN)
    yield
    pa = fold(pa_st)
    q = fold(q_st)
    y1 = rt + fold(_mm3(rb, pa_st, NN))
    y0 = fold(_mm3(rb, q_st, NN) + _mm3(rk, v_st, NN))
    bg = _split(bt * g_end)
    r2 = lax.broadcasted_iota(jnp.int32, (128, 128), 0)
    c2 = lax.broadcasted_iota(jnp.int32, (128, 128), 1)
    same_head = (r2 >= 64) == (c2 >= 64)
    m = jnp.where(r2 == c2, g_end, 0.0) + jnp.where(same_head, _mm3(bg, pa, TN), 0.0)
    nn = jnp.where(same_head, _mm3(bg, q, TN) + _mm3(kt * g_end, v, TN), 0.0)
    return m, nn, y1, y0


def _rwkv_chunk_kernel(r_ref, e_ref, kk_ref, ab_ref, k2_ref, v_ref, m_ref, n_ref, y1_ref, y0_ref, *, c):
    lanes = [slice(g * 128, (g + 1) * 128) for g in range(GROUP // 128)]
    chains = [_rwkv_chunk_pair(r_ref[:, ls], e_ref[:, ls], kk_ref[:, ls], ab_ref[:, ls], k2_ref[:, ls],
                               v_ref[:, ls], c) for ls in lanes]
    for g, (ls, (m, nn, y1, y0)) in enumerate(zip(lanes, _round_robin(chains))):
        m_ref[0, g] = m
        n_ref[0, g] = nn
        y1_ref[:, ls] = y1
        y0_ref[:, ls] = y0


def _rwkv_scan_kernel(m_ref, n_ref, y1_ref, y0_ref, r_ref, k2_ref, v_ref, gate_ref, h0_ref,
                      lnw_ref, lnb_ref, rk_ref, y_ref, hout_ref, h):
    t = pl.program_id(1)

    @pl.when(t == 0)
    def _():
        h[...] = h0_ref[0]

    def pair(g):
        ls = slice(g * 128, (g + 1) * 128)
        hg = _split(h[g])
        h[g] = _mm3(m_ref[0, g], hg, NN) + n_ref[0, g]
        y = _mm3(y1_ref[:, ls], hg, NN) + y0_ref[:, ls]
        bonus = _seg_sum(r_ref[:, ls] * k2_ref[:, ls] * rk_ref[:, ls], 128) * v_ref[:, ls]
        yield
        mean = _seg_sum(y, 128) * (1.0 / 64)
        yield
        yc = y - mean
        var = _seg_sum(yc * yc, 128) * (1.0 / 64)
        yield
        yn = yc * lax.rsqrt(var + RWKV_LN_EPS) * lnw_ref[:, ls] + lnb_ref[:, ls]
        y_ref[:, ls] = (yn + bonus) * gate_ref[:, ls]

    _round_robin([pair(g) for g in range(GROUP // 128)])
    hout_ref[0] = h[...]


def rwkv_scan(r, e, kk, ab, k2, v, gate, s0, ln_w, ln_b, r_k, batch, c):
    n, g = r.shape
    nc = n // batch // c
    npair = g // 128
    row = lambda b, t: (b * nc + t, 0)
    blk = pl.BlockSpec((c, g), row)
    mat = pl.BlockSpec((1, npair, 128, 128), lambda b, t: (b * nc + t, 0, 0, 0))
    m, nn, y1, y0 = pl.pallas_call(
        functools.partial(_rwkv_chunk_kernel, c=c),
        grid=(batch, nc),
        in_specs=[blk] * 6,
        out_specs=[mat, mat, blk, blk],
        out_shape=[jax.ShapeDtypeStruct((batch * nc, npair, 128, 128), F32)] * 2
        + [jax.ShapeDtypeStruct((n, g), F32)] * 2,
        compiler_params=_cp("arbitrary", "arbitrary"),
        name="rwkv_chunk",
    )(r, e, kk, ab, k2, v)
    st = jnp.swapaxes(s0, -1, -2).reshape(batch, npair, 2, 64, 64)
    eye2 = jnp.eye(2, dtype=F32)
    h0 = (st[:, :, :, :, None, :] * eye2[None, None, :, None, :, None]).reshape(batch, npair, 128, 128)
    state = pl.BlockSpec((1, npair, 128, 128), lambda b, t: (b, 0, 0, 0))
    vec = pl.BlockSpec((1, g), lambda b, t: (0, 0))
    y, hout = pl.pallas_call(
        _rwkv_scan_kernel,
        grid=(batch, nc),
        in_specs=[mat, mat, blk, blk, blk, blk, blk, blk, state, vec, vec, vec],
        out_specs=[blk, state],
        out_shape=[jax.ShapeDtypeStruct((n, g), F32), jax.ShapeDtypeStruct((batch, npair, 128, 128), F32)],
        scratch_shapes=[pltpu.VMEM((npair, 128, 128), F32)],
        compiler_params=_cp("arbitrary", "arbitrary"),
        name="rwkv_scan",
    )(m, nn, y1, y0, r, k2, v, gate, h0, ln_w.reshape(1, g), ln_b.reshape(1, g), r_k.reshape(1, g))
    hb = hout.reshape(batch, npair, 2, 64, 2, 64)
    s_last = jnp.stack([hb[:, :, 0, :, 0, :], hb[:, :, 1, :, 1, :]], axis=2).reshape(batch, 2 * npair, 64, 64)
    return y, jnp.swapaxes(s_last, -1, -2)


PAGES_PER_STEP = 16


def _past_bucket_row(first_pos, width, q_pos):
    rel = q_pos - (first_pos + np.arange(width))
    return _t5_bucket_np(rel).astype(np.int32).reshape(1, width)


def _bias_rows(bucket_row, t5t_ref):
    out = jnp.zeros((8, bucket_row.shape[1]), F32)
    for b in range(N_BUCKETS):
        out = jnp.where(bucket_row == b, t5t_ref[:, b:b + 1], out)
    return out


def _page_logits(qcol, kt):
    return jnp.sum(qcol * kt, axis=1)


def _moba_scan_kernel(pt_ref, q_ref, *refs, n_steps, n_blocks):
    pages = refs[:PAGES_PER_STEP]
    lg_ref, idx_ref, gate_s = refs[PAGES_PER_STEP:]
    s = pl.program_id(1)

    @pl.when(s == 0)
    def _():
        gate_s[...] = jnp.zeros_like(gate_s)

    qcol = q_ref[0]
    lane = lax.broadcasted_iota(jnp.int32, (8, 128), 1)
    gate = gate_s[...]
    for i, pg in enumerate(pages):
        lg = _page_logits(qcol, pg[0])
        blk, half = divmod(i, 2)
        lg_ref[0, :, blk, half * PAGE:(half + 1) * PAGE] = lg * 0.125
        gate = gate + jnp.where(lane == s * (PAGES_PER_STEP // 2) + blk, jnp.sum(lg, axis=1, keepdims=True), 0.0)
    gate_s[...] = gate

    @pl.when(s == n_steps - 1)
    def _():
        gt = jnp.where(lane < n_blocks, gate * (1.0 / ATT_BLOCK), -jnp.inf)
        out = jnp.zeros((8, 128), jnp.int32)
        for j in range(MOBA_TOPK):
            mx = jnp.max(gt, axis=1, keepdims=True)
            first = jnp.min(jnp.where(gt == mx, lane, 128), axis=1, keepdims=True)
            out = jnp.where(lane == j, first, out)
            gt = jnp.where(lane == first, -jnp.inf, gt)
        idx_ref[0] = out


def _moba_gather_kernel(idx_ref, pt_ref, t5_ref, q_ref, kn_ref, vn_ref, lg_ref, bk_ref, *refs, n_blocks):
    vpages = refs[:2 * MOBA_TOPK]
    o_ref = refs[2 * MOBA_TOPK]
    b = pl.program_id(0)
    h = pl.program_id(1)
    q = q_ref[0, 0]
    own = jnp.sum(q * kn_ref[0, 0], axis=1, keepdims=True) * 0.125 + t5_ref[0, h]
    far = t5_ref[N_BUCKETS - 1, h]
    near = far
    for bkt in range(N_BUCKETS):
        near = jnp.where(bk_ref[...] == bkt, t5_ref[bkt, h], near)
    logits = []
    for s in range(MOBA_TOPK):
        blk = idx_ref[b, h * MOBA_TOPK + s]
        lg = lg_ref[0, 0, pl.ds(blk, 1), :]
        logits.append(lg + jnp.where(blk == n_blocks - 1, near, far))
    mx = own
    for lg in logits:
        mx = jnp.maximum(mx, jnp.max(lg, axis=1, keepdims=True))
    p_own = jnp.exp(own - mx)
    den = p_own
    acc = p_own * vn_ref[0, 0]
    for s, lg in enumerate(logits):
        p = jnp.exp(lg - mx)
        den = den + jnp.sum(p, axis=1, keepdims=True)
        for pg in range(2):
            acc = acc + _nt(p[:, pg * PAGE:(pg + 1) * PAGE], vpages[s * 2 + pg][0, 0])
    o_ref[0, 0] = acc / den


def moba_step(q, k_new, v_new, cache_k, cache_v, page_table, t5_bias):
    nb, c = q.shape
    nh = c // 64
    n_pages = page_table.shape[1]
    n_steps = n_pages // PAGES_PER_STEP
    n_blocks = n_pages * PAGE // ATT_BLOCK
    bps = PAGES_PER_STEP // 2
    qcol = jnp.broadcast_to(q.reshape(nb, nh, 64, 1), (nb, nh, 64, PAGE))

    def page_spec(i):
        return pl.BlockSpec((1, nh, 64, PAGE), lambda b, s, pt: (pt[b, s * PAGES_PER_STEP + i], 0, 0, 0))

    logits, idx = pl.pallas_call(
        functools.partial(_moba_scan_kernel, n_steps=n_steps, n_blocks=n_blocks),
        grid_spec=pltpu.PrefetchScalarGridSpec(
            num_scalar_prefetch=1,
            grid=(nb, n_steps),
            in_specs=[pl.BlockSpec((1, nh, 64, PAGE), lambda b, s, pt: (b, 0, 0, 0))]
            + [page_spec(i) for i in range(PAGES_PER_STEP)],
            out_specs=[pl.BlockSpec((1, nh, bps, ATT_BLOCK), lambda b, s, pt: (b, 0, s, 0)),
                       pl.BlockSpec((1, nh, 128), lambda b, s, pt: (b, 0, 0))],
            scratch_shapes=[pltpu.VMEM((nh, 128), F32)]),
        out_shape=[jax.ShapeDtypeStruct((nb, nh, n_blocks, ATT_BLOCK), F32),
                   jax.ShapeDtypeStruct((nb, nh, 128), jnp.int32)],
        compiler_params=_cp("arbitrary", "arbitrary"),
        name="moba_scan",
    )(page_table, qcol, *([cache_k] * PAGES_PER_STEP))
    sel = idx[:, :, :MOBA_TOPK].reshape(nb, nh * MOBA_TOPK)
    bucket = jnp.asarray(_past_bucket_row((n_blocks - 1) * ATT_BLOCK, ATT_BLOCK, n_pages * PAGE))

    def vpage_spec(s, pg):
        def imap(b, h, sel, pt):
            return (pt[b, 2 * sel[b, h * MOBA_TOPK + s] + pg], h, 0, 0)
        return pl.BlockSpec((1, 1, 64, PAGE), imap)

    row = pl.BlockSpec((1, 1, 1, 64), lambda b, h, sel, pt: (b, h, 0, 0))
    out = pl.pallas_call(
        functools.partial(_moba_gather_kernel, n_blocks=n_blocks),
        grid_spec=pltpu.PrefetchScalarGridSpec(
            num_scalar_prefetch=2,
            grid=(nb, nh),
            in_specs=[pl.BlockSpec(memory_space=pltpu.SMEM), row, row, row,
                      pl.BlockSpec((1, 1, n_blocks, ATT_BLOCK), lambda b, h, sel, pt: (b, h, 0, 0)),
                      pl.BlockSpec((1, ATT_BLOCK), lambda b, h, sel, pt: (0, 0))]
            + [vpage_spec(s, pg) for s in range(MOBA_TOPK) for pg in range(2)],
            out_specs=row),
        out_shape=jax.ShapeDtypeStruct((nb, nh, 1, 64), F32),
        compiler_params=_cp("arbitrary", "arbitrary"),
        name="moba_gather",
    )(sel, page_table, t5_bias, q.reshape(nb, nh, 1, 64), k_new.reshape(nb, nh, 1, 64),
      v_new.reshape(nb, nh, 1, 64), logits, bucket, *([cache_v] * (2 * MOBA_TOPK)))
    return out.reshape(nb, c)


def _diff_step_kernel(pt_ref, lam_ref, qcol_ref, q_ref, kn_ref, vn_ref, t5t_ref, bk_ref, sub_ref, spread_ref,
                      *refs, n_steps, lam_init):
    kpages = refs[:PAGES_PER_STEP]
    vpages = refs[PAGES_PER_STEP:2 * PAGES_PER_STEP]
    o_ref, m_s, l_s, acc_s = refs[2 * PAGES_PER_STEP:]
    s = pl.program_id(1)
    nh = vn_ref.shape[1]
    head_of_row = lax.broadcasted_iota(jnp.int32, (2 * nh, 128), 0) // 2
    own_head = (lax.broadcasted_iota(jnp.int32, (2 * nh, nh * PAGE), 1) % nh
                == lax.broadcasted_iota(jnp.int32, (2 * nh, nh * PAGE), 0) // 2)

    def per_head(rows):
        out = jnp.broadcast_to(rows[0], (2 * nh, 128))
        for h in range(1, nh):
            out = jnp.where(head_of_row == h, rows[h], out)
        return out

    @pl.when(s == 0)
    def _():
        m_s[...] = jnp.sum(q_ref[0] * kn_ref[0], axis=1, keepdims=True) * 0.125 + t5t_ref[:, 0:1]
        l_s[...] = jnp.ones_like(l_s)
        vn = vn_ref[0]
        acc_s[...] = per_head([vn[h:h + 1, :] for h in range(nh)])

    far = t5t_ref[:, N_BUCKETS - 1:N_BUCKETS]
    near = _bias_rows(bk_ref[...], t5t_ref)
    qcol = qcol_ref[0]
    m, l, acc = m_s[...], l_s[...], acc_s[...]
    for i in range(PAGES_PER_STEP):
        lg = _page_logits(qcol, kpages[i][0]) * 0.125
        if i == PAGES_PER_STEP - 1:
            lg = lg + jnp.where(s == n_steps - 1, near, far)
        else:
            lg = lg + far
        mn = jnp.maximum(m, jnp.max(lg, axis=1, keepdims=True))
        alpha = jnp.exp(m - mn)
        p = jnp.exp(lg - mn)
        l = alpha * l + jnp.sum(p, axis=1, keepdims=True)
        p_rows = jnp.where(own_head, _dot(p.astype(BF16), spread_ref[...]), 0.0).astype(BF16)
        acc = alpha * acc + _dot(p_rows, vpages[i][0].astype(BF16))
        m = mn
    m_s[...], l_s[...], acc_s[...] = m, l, acc

    @pl.when(s == n_steps - 1)
    def _():
        a = acc / l
        for h in range(nh):
            att = a[2 * h:2 * h + 1, :] - lam_ref[0] * a[2 * h + 1:2 * h + 2, :]
            o_ref[0, h:h + 1, :] = _rms(att, sub_ref[...], 1e-5) * (1.0 - lam_init)


def diff_step(q, k_new, v_new, cache_k, cache_v, page_table, t5_bias, lam, subln_w, lam_init):
    nb, c = q.shape
    nm = c // 64
    nh = nm // 2
    n_pages = page_table.shape[1]
    n_steps = n_pages // PAGES_PER_STEP
    bucket = jnp.asarray(_past_bucket_row((n_pages - 1) * PAGE, PAGE, n_pages * PAGE))
    qcol = jnp.broadcast_to(q.reshape(nb, nm, 64, 1), (nb, nm, 64, PAGE))

    def kpage_spec(i):
        return pl.BlockSpec((1, nm, 64, PAGE), lambda b, s, pt: (pt[b, s * PAGES_PER_STEP + i], 0, 0, 0))

    cache_v = cache_v.reshape(cache_v.shape[0], PAGE * nh, 128)
    spread = jnp.asarray(np.repeat(np.eye(PAGE, dtype=np.float32), nh, axis=1), BF16)

    def vpage_spec(i):
        return pl.BlockSpec((1, PAGE * nh, 128), lambda b, s, pt: (pt[b, s * PAGES_PER_STEP + i], 0, 0))

    maps = pl.BlockSpec((1, nm, 64), lambda b, s, pt: (b, 0, 0))
    heads = pl.BlockSpec((1, nh, 128), lambda b, s, pt: (b, 0, 0))
    out = pl.pallas_call(
        functools.partial(_diff_step_kernel, n_steps=n_steps, lam_init=lam_init),
        grid_spec=pltpu.PrefetchScalarGridSpec(
            num_scalar_prefetch=1,
            grid=(nb, n_steps),
            in_specs=[pl.BlockSpec(memory_space=pltpu.SMEM),
                      pl.BlockSpec((1, nm, 64, PAGE), lambda b, s, pt: (b, 0, 0, 0)), maps, maps, heads,
                      pl.BlockSpec((nm, N_BUCKETS), lambda b, s, pt: (0, 0)),
                      pl.BlockSpec((1, PAGE), lambda b, s, pt: (0, 0)),
                      pl.BlockSpec((1, 128), lambda b, s, pt: (0, 0)),
                      pl.BlockSpec((PAGE, PAGE * nh), lambda b, s, pt: (0, 0))]
            + [kpage_spec(i) for i in range(PAGES_PER_STEP)] + [vpage_spec(i) for i in range(PAGES_PER_STEP)],
            out_specs=heads,
            scratch_shapes=[pltpu.VMEM((nm, 1), F32), pltpu.VMEM((nm, 1), F32), pltpu.VMEM((nm, 128), F32)]),
        out_shape=jax.ShapeDtypeStruct((nb, nh, 128), F32),
        compiler_params=_cp("arbitrary", "arbitrary"),
        name="diff_step",
    )(page_table, lam.reshape(1), qcol, q.reshape(nb, nm, 64), k_new.reshape(nb, nm, 64), v_new.reshape(nb, nh, 128),
      t5_bias.T, bucket, subln_w.reshape(1, 128), spread, *([cache_k] * PAGES_PER_STEP),
      *([cache_v] * PAGES_PER_STEP))
    return out.reshape(nb, c)


def rwkv_mix(z, shift0, s0, mu, w0, w2, a0, a2, g2, k_k, k_a, r_k, ln_w, ln_b, batch, step):
    params = _rwkv_prep_params(mu, w0, w2, a0, a2, g2, k_k, k_a)
    if not step:
        outs = rwkv_prep(z, batch, params)
        r, e, kk, ab, k2, v, gate = outs
        return rwkv_scan(r, e, kk, ab, k2, v, gate, s0, ln_w, ln_b, r_k, batch, 64)
    outs = rwkv_prep(z, batch, params, shift=shift0)
    r, e, kk, ab, k2, v, gate = (jnp.pad(o[:, None, :], ((0, 0), (0, 7), (0, 0))).reshape(batch * 8, GROUP)
                                 for o in outs)
    y, s_last = rwkv_scan(r, e, kk, ab, k2, v, gate, s0, ln_w, ln_b, r_k, batch, 8)
    return y.reshape(batch, 8, GROUP)[:, 0], s_last


def _trunk(x, p, batch, step, st, W):
    depth = p.shape[0]
    outs = {k: [] for k in ("moba_k", "moba_v", "lru_conv", "lru_h", "diff_k", "diff_v",
                            "rwkv_shift", "rwkv_s", "ffn_conv")}
    t = x.shape[0] // batch
    for i in range(depth):
        j = i // 2
        g = GROUP
        n_pp = t // PAGE
        if i % 2 == 0:
            lru_w = (W["lru_conv_w"][j], W["lru_conv_b"][j], W["lru_w_a"][j], W["lru_b_a"][j],
                     W["lru_w_x"][j], W["lru_b_x"][j], W["lru_lambda"][j])
            cols = [("f32", c * g, g) for c in range(3)]
            if step:
                gate, xin, q, k, v = norm_linear(x, W["norm_mix_pre"][i], W["even_w_in"][j],
                                                 cols + [("f32", 3 * g, g), ("f32", 4 * g, g)])
                ya, conv, h_last = lru_step(gate, xin, st["lru_conv"][j], st["lru_h"][j], *lru_w)
                yb = moba_step(q, k, v, st["moba_k"][j], st["moba_v"][j], st["page_table"], W["t5_bias"])
                k_out, v_out = k.reshape(batch, t, 8, 64), v.reshape(batch, t, 8, 64)
            else:
                gate, xin, q, kb, vb, kt, vt, kmean = norm_linear(
                    x, W["norm_mix_pre"][i], W["even_w_in"][j],
                    cols + [("bf16", 3 * g, g), ("bf16", 4 * g, g), ("pages", 3 * g, g), ("pages", 4 * g, g),
                            ("blockmean", 3 * g, g)])
                ya, conv, h_last = lru_prompt(gate, xin, *lru_w, batch)
                blocks_per_tile = x.shape[0] // kmean.shape[0] // ATT_BLOCK
                kmean = kmean[:, :blocks_per_tile].reshape(-1, g)
                yb = attn_prompt(q, kb, vb, W["t5_bias"], batch, "moba", kmean=kmean)
                k_out = jnp.transpose(kt.reshape(batch, n_pp, 8, 64, PAGE), (0, 1, 4, 2, 3))
                v_out = jnp.transpose(vt.reshape(batch, n_pp, 8, 64, PAGE), (0, 1, 4, 2, 3))
            outs["moba_k"].append(k_out)
            outs["moba_v"].append(v_out)
            outs["lru_conv"].append(conv)
            outs["lru_h"].append(h_last)
            w_out = W["even_w_out"][j]
        else:
            rw = (W["rwkv_mu"][j], W["rwkv_w0"][j], W["rwkv_w2"][j], W["rwkv_a0"][j], W["rwkv_a2"][j],
                  W["rwkv_g2"][j], W["rwkv_k_k"][j], W["rwkv_k_a"][j], W["rwkv_r_k"][j],
                  W["rwkv_ln_w"][j], W["rwkv_ln_b"][j])
            lam_init = 0.8 - 0.6 * math.exp(-0.3 * i)
            lf = W["diff_lambda"][j]
            lam = jnp.exp(jnp.sum(lf[0] * lf[1])) - jnp.exp(jnp.sum(lf[2] * lf[3])) + lam_init
            o = RWKV_COLS
            cols = [("f32", 0, o), ("f32", o, g)]
            if step:
                z, q, k, v = norm_linear(x, W["norm_mix_pre"][i], W["odd_w_in"][j],
                                         cols + [("f32", o + g, g), ("f32", o + 2 * g, g)])
                ya, s_last = rwkv_mix(z, st["rwkv_shift"][j], st["rwkv_s"][j], *rw, batch, True)
                yb = diff_step(q, k, v, st["diff_k"][j], st["diff_v"][j], st["page_table"], W["t5_bias"],
                               lam, W["diff_subln_w"][j], lam_init)
                shift = z
                k_out = k.reshape(batch, t, 4, 2, 64)
            else:
                z, q, kb, vb, kt, v = norm_linear(
                    x, W["norm_mix_pre"][i], W["odd_w_in"][j],
                    cols + [("bf16", o + g, g), ("bf16", o + 2 * g, g), ("pages", o + g, g), ("f32", o + 2 * g, g)])
                s0 = jnp.zeros((batch, 8, 64, 64), F32)
                ya, s_last = rwkv_mix(z, None, s0, *rw, batch, False)
                yb = attn_prompt(q, kb, vb, W["t5_bias"], batch, "diff", lam=lam, subln_w=W["diff_subln_w"][j],
                                 lam_init=lam_init)
                shift = z.reshape(batch, t, RWKV_COLS)[:, t - 1]
                k_out = jnp.transpose(kt.reshape(batch, n_pp, 4, 2, 64, PAGE), (0, 1, 5, 2, 3, 4))
            outs["diff_k"].append(k_out)
            outs["diff_v"].append(v.reshape(k_out.shape[:-3] + (4, 128)))
            outs["rwkv_shift"].append(shift)
            outs["rwkv_s"].append(s_last)
            w_out = W["odd_w_out"][j]
        x = out_proj(ya, yb, w_out, x, W["norm_mix_post"][i])
        ffn_w = (W["norm_ffn_pre"][i], W["ffn_w_up"][i], W["ffn_conv_w"][i], W["ffn_conv_b"][i], W["ffn_w_down"][i],
                 W["norm_ffn_post"][i], W["ple_w_gate"][i], W["ple_w_proj"][i])
        if step:
            x, fbuf = ffn_step(x, p[i], *ffn_w, st["ffn_conv"][i])
        else:
            x, fbuf = ffn_prompt(x, p[i], *ffn_w, batch)
        outs["ffn_conv"].append(fbuf)
    return x, {k: jnp.stack(v) for k, v in outs.items()}


def kernel(x_prompt, x_sample, cache_moba_k, cache_moba_v, state_lru_conv, state_lru_h, cache_diff_k, cache_diff_v, state_rwkv_shift, state_rwkv, state_ffn_conv, page_table, p_prompt, p_sample, t5_bias, norm_mix_pre, norm_mix_post, norm_ffn_pre, norm_ffn_post, even_w_in, even_w_out, lru_conv_w, lru_conv_b, lru_w_a, lru_b_a, lru_w_x, lru_b_x, lru_lambda, odd_w_in, odd_w_out, rwkv_mu, rwkv_w0, rwkv_w2, rwkv_a0, rwkv_a2, rwkv_g2, rwkv_k_k, rwkv_k_a, rwkv_r_k, rwkv_ln_w, rwkv_ln_b, diff_lambda, diff_subln_w, ffn_w_up, ffn_conv_w, ffn_conv_b, ffn_w_down, ple_w_proj, ple_w_gate):
    W = dict(t5_bias=t5_bias, norm_mix_pre=norm_mix_pre, norm_mix_post=norm_mix_post,
             norm_ffn_pre=norm_ffn_pre, norm_ffn_post=norm_ffn_post,
             even_w_in=even_w_in, even_w_out=even_w_out, lru_conv_w=lru_conv_w, lru_conv_b=lru_conv_b,
             lru_w_a=lru_w_a, lru_b_a=lru_b_a, lru_w_x=lru_w_x, lru_b_x=lru_b_x, lru_lambda=lru_lambda,
             odd_w_in=odd_w_in, odd_w_out=odd_w_out, rwkv_mu=rwkv_mu, rwkv_w0=rwkv_w0, rwkv_w2=rwkv_w2,
             rwkv_a0=rwkv_a0, rwkv_a2=rwkv_a2, rwkv_g2=rwkv_g2, rwkv_k_k=rwkv_k_k, rwkv_k_a=rwkv_k_a,
             rwkv_r_k=rwkv_r_k, rwkv_ln_w=rwkv_ln_w, rwkv_ln_b=rwkv_ln_b,
             diff_lambda=diff_lambda, diff_subln_w=diff_subln_w,
             ffn_w_up=ffn_w_up, ffn_conv_w=ffn_conv_w, ffn_conv_b=ffn_conv_b, ffn_w_down=ffn_w_down,
             ple_w_proj=ple_w_proj, ple_w_gate=ple_w_gate)
    bp, tp, d = x_prompt.shape
    bs, ts, _ = x_sample.shape
    depth = p_prompt.shape[0]
    n_pp = tp // PAGE
    assert ts == 1, "the sample group is a single-token step"

    yp, P = _trunk(x_prompt.reshape(bp * tp, d), p_prompt.reshape(depth, bp * tp, -1), bp, False, None, W)

    pool = cache_moba_k.shape[1]
    st = dict(moba_k=jnp.transpose(cache_moba_k, (0, 1, 3, 4, 2)), moba_v=jnp.transpose(cache_moba_v, (0, 1, 3, 4, 2)),
              diff_k=jnp.transpose(cache_diff_k, (0, 1, 3, 4, 5, 2)).reshape(-1, pool, 8, 64, PAGE),
              diff_v=cache_diff_v,
              lru_conv=state_lru_conv, lru_h=state_lru_h, rwkv_shift=state_rwkv_shift, rwkv_s=state_rwkv,
              ffn_conv=state_ffn_conv, page_table=page_table)
    ys, S = _trunk(x_sample.reshape(bs * ts, d), p_sample.reshape(depth, bs * ts, -1), bs, True, st, W)

    return (yp.reshape(bp, tp, d), ys.reshape(bs, ts, d),
            P["moba_k"], P["moba_v"], S["moba_k"], S["moba_v"],
            P["lru_conv"], S["lru_conv"], P["lru_h"], S["lru_h"],
            P["diff_k"], P["diff_v"], S["diff_k"], S["diff_v"],
            P["rwkv_shift"], S["rwkv_shift"], P["rwkv_s"], S["rwkv_s"],
            P["ffn_conv"], S["ffn_conv"])
```

```python
import functools
import math

import numpy as np
import jax
import jax.numpy as jnp
from jax import lax
from jax.experimental import pallas as pl
from jax.experimental.pallas import tpu as pltpu

F32 = jnp.float32
BF16 = jnp.bfloat16
HI = lax.Precision.HIGHEST

D_MODEL = 1024
GROUP = 512
PAGE = 128
VMEM_LIMIT = 56 * 1024 * 1024


def _cp(*sem):
    return pltpu.CompilerParams(dimension_semantics=sem, vmem_limit_bytes=VMEM_LIMIT)


def _rms(x, g, eps):
    return x * lax.rsqrt(jnp.mean(x * x, axis=-1, keepdims=True) + eps) * g


def _gelu(x):
    return 0.5 * x * (1.0 + jnp.tanh(math.sqrt(2.0 / math.pi) * (x + 0.044715 * (x * x * x))))


def _sigmoid(x):
    return 1.0 / (1.0 + jnp.exp(-x))


def _dot(a, b):
    return jnp.dot(a, b, preferred_element_type=F32)


def _round_robin(chains):
    done = {}
    while len(done) < len(chains):
        for i, chain in enumerate(chains):
            if i not in done:
                try:
                    next(chain)
                except StopIteration as stop:
                    done[i] = stop.value
    return [done[i] for i in range(len(chains))]


def _shift_rows(x, d, fill):
    r = pltpu.roll(x, d, 0)
    row = lax.broadcasted_iota(jnp.int32, x.shape, 0)
    for i in range(d):
        r = jnp.where(row == i, fill[i], r)
    return r


def _norm_linear_kernel(x_ref, g_ref, w_ref, *out_refs, outs, tm):
    h = _rms(x_ref[...], g_ref[...], 1e-6).astype(BF16)
    ys = {}
    for o_ref, (kind, off, n) in zip(out_refs, outs):
        if (off, n) not in ys:
            ys[(off, n)] = _dot(h, w_ref[:, off:off + n])
        y = ys[(off, n)]
        if kind == "f32":
            o_ref[...] = y
        elif kind == "bf16":
            o_ref[...] = y.astype(BF16)
        elif kind == "pages":
            for pg in range(tm // PAGE):
                o_ref[pg] = y[pg * PAGE:(pg + 1) * PAGE, :].T
        elif kind == "blockmean":
            o_ref[0] = jnp.zeros((8, n), F32)
            for bi in range(tm // ATT_BLOCK):
                o_ref[0, bi:bi + 1, :] = jnp.sum(y[bi * ATT_BLOCK:(bi + 1) * ATT_BLOCK, :], axis=0,
                                                 keepdims=True) * (1.0 / ATT_BLOCK)


def norm_linear(x, g, w, outs):
    n, d = x.shape
    m = w.shape[1]
    tm = min(n, 512)
    specs, shapes = [], []
    for kind, _, s in outs:
        if kind in ("f32", "bf16"):
            specs.append(pl.BlockSpec((tm, s), lambda i: (i, 0)))
            shapes.append(jax.ShapeDtypeStruct((n, s), F32 if kind == "f32" else BF16))
        elif kind == "pages":
            specs.append(pl.BlockSpec((tm // PAGE, s, PAGE), lambda i: (i, 0, 0)))
            shapes.append(jax.ShapeDtypeStruct((n // PAGE, s, PAGE), F32))
        else:
            specs.append(pl.BlockSpec((1, 8, s), lambda i: (i, 0, 0)))
            shapes.append(jax.ShapeDtypeStruct((n // tm, 8, s), F32))
    return pl.pallas_call(
        functools.partial(_norm_linear_kernel, outs=outs, tm=tm),
        grid=(n // tm,),
        in_specs=[pl.BlockSpec((tm, d), lambda i: (i, 0)),
                  pl.BlockSpec((1, d), lambda i: (0, 0)),
                  pl.BlockSpec((d, m), lambda i: (0, 0))],
        out_specs=specs,
        out_shape=shapes,
        compiler_params=_cp("arbitrary"),
        name="norm_linear",
    )(x, g.reshape(1, d), w.astype(BF16))


def _out_proj_kernel(a_ref, b_ref, w_ref, x_ref, g_ref, o_ref):
    y = _dot(a_ref[...].astype(BF16), w_ref[:GROUP, :]) + _dot(b_ref[...].astype(BF16), w_ref[GROUP:, :])
    o_ref[...] = x_ref[...] + _rms(y, g_ref[...], 1e-6)


def out_proj(a, b, w, x, g):
    n, d = x.shape
    tm = min(n, 512)
    return pl.pallas_call(
        _out_proj_kernel,
        grid=(n // tm,),
        in_specs=[pl.BlockSpec((tm, GROUP), lambda i: (i, 0)),
                  pl.BlockSpec((tm, GROUP), lambda i: (i, 0)),
                  pl.BlockSpec((2 * GROUP, d), lambda i: (0, 0)),
                  pl.BlockSpec((tm, d), lambda i: (i, 0)),
                  pl.BlockSpec((1, d), lambda i: (0, 0))],
        out_specs=pl.BlockSpec((tm, d), lambda i: (i, 0)),
        out_shape=jax.ShapeDtypeStruct((n, d), F32),
        compiler_params=_cp("arbitrary"),
        name="out_proj",
    )(a, b, w.astype(BF16), x, g.reshape(1, d))


def _ffn_tail(acc, x, gpost, p, wg_ref, wp_ref):
    x1 = x + _rms(acc, gpost, 1e-6)
    gate = _sigmoid(_dot(x1.astype(BF16), wg_ref[...]))
    return x1 + gate * _dot(p.astype(BF16), wp_ref[...])


def _ffn_kernel(x_ref, p_ref, gpre_ref, wug_ref, wuv_ref, cwg_ref, cwv_ref, cbg_ref, cbv_ref, wd_ref,
                gpost_ref, wg_ref, wp_ref, o_ref, st_ref, hn, acc, carry, *, tm, f, nf):
    t = pl.program_id(1)
    j = pl.program_id(2)

    @pl.when(j == 0)
    def _():
        hn[...] = _rms(x_ref[...], gpre_ref[...], 1e-6).astype(BF16)
        acc[...] = jnp.zeros_like(acc)

    @pl.when(t == 0)
    def _():
        carry[j] = jnp.zeros((8, 2 * f), F32)

    h = hn[...]
    ug = _dot(h, wug_ref[...])
    uv = _dot(h, wuv_ref[...])
    prev = carry[j]

    def conv(u, pv, cw_ref, cb_ref):
        cw = cw_ref[...]
        u1 = _shift_rows(u, 1, [pv[7:8]])
        u2 = _shift_rows(u, 2, [pv[6:7], pv[7:8]])
        return cb_ref[...] + cw[0:1] * u2 + cw[1:2] * u1 + cw[2:3] * u

    cg = conv(ug, prev[:, :f], cwg_ref, cbg_ref)
    cv = conv(uv, prev[:, f:], cwv_ref, cbv_ref)
    carry[j] = jnp.concatenate([ug[tm - 8:, :], uv[tm - 8:, :]], axis=1)
    st_ref[0, 0, 0, 0:1, :] = ug[tm - 2:tm - 1, :]
    st_ref[0, 0, 0, 1:2, :] = uv[tm - 2:tm - 1, :]
    st_ref[0, 0, 1, 0:1, :] = ug[tm - 1:tm, :]
    st_ref[0, 0, 1, 1:2, :] = uv[tm - 1:tm, :]
    act = (_gelu(cg) * cv).astype(BF16)
    acc[...] += _dot(act, wd_ref[...])

    @pl.when(j == nf - 1)
    def _():
        o_ref[...] = _ffn_tail(acc[...], x_ref[...], gpost_ref[...], p_ref[...], wg_ref, wp_ref)


def ffn_prompt(x, p, gpre, w_up, conv_w, conv_b, w_down, gpost, w_gate, w_proj, batch, tm=1024, f=512):
    n, d = x.shape
    dff = w_down.shape[0]
    nf = dff // f
    nt = n // batch // tm
    pd = p.shape[1]
    w_up = w_up.astype(BF16)
    conv_b = conv_b.reshape(1, 2 * dff)
    row = lambda b, t, j: (b * nt + t, 0)
    const = lambda b, t, j: (0, 0)
    out, st = pl.pallas_call(
        functools.partial(_ffn_kernel, tm=tm, f=f, nf=nf),
        grid=(batch, nt, nf),
        in_specs=[pl.BlockSpec((tm, d), row),
                  pl.BlockSpec((tm, pd), row),
                  pl.BlockSpec((1, d), const),
                  pl.BlockSpec((d, f), lambda b, t, j: (0, j)),
                  pl.BlockSpec((d, f), lambda b, t, j: (0, nf + j)),
                  pl.BlockSpec((3, f), lambda b, t, j: (0, j)),
                  pl.BlockSpec((3, f), lambda b, t, j: (0, nf + j)),
                  pl.BlockSpec((1, f), lambda b, t, j: (0, j)),
                  pl.BlockSpec((1, f), lambda b, t, j: (0, nf + j)),
                  pl.BlockSpec((f, d), lambda b, t, j: (j, 0)),
                  pl.BlockSpec((1, d), const),
                  pl.BlockSpec((d, d), const),
                  pl.BlockSpec((pd, d), const)],
        out_specs=[pl.BlockSpec((tm, d), row),
                   pl.BlockSpec((1, 1, 2, 2, f), lambda b, t, j: (b, t, 0, 0, j))],
        out_shape=[jax.ShapeDtypeStruct((n, d), F32),
                   jax.ShapeDtypeStruct((batch, nt, 2, 2, dff), F32)],
        scratch_shapes=[pltpu.VMEM((tm, d), BF16), pltpu.VMEM((tm, d), F32), pltpu.VMEM((nf, 8, 2 * f), F32)],
        compiler_params=_cp("arbitrary", "arbitrary", "arbitrary"),
        name="ffn_prompt",
    )(x, p, gpre.reshape(1, d), w_up, w_up, conv_w, conv_w, conv_b, conv_b, w_down.astype(BF16),
      gpost.reshape(1, d), w_gate.astype(BF16), w_proj.astype(BF16))
    return out, st[:, nt - 1].reshape(batch, 2, 2 * dff)


def _ffn_step_kernel(x_ref, p_ref, gpre_ref, wug_ref, wuv_ref, cwg_ref, cwv_ref, cbg_ref, cbv_ref, wd_ref,
                     gpost_ref, wg_ref, wp_ref, s0g_ref, s0v_ref, s1g_ref, s1v_ref,
                     o_ref, ug_ref, uv_ref, acc, *, nf):
    j = pl.program_id(0)

    @pl.when(j == 0)
    def _():
        acc[...] = jnp.zeros_like(acc)

    h = _rms(x_ref[...], gpre_ref[...], 1e-6).astype(BF16)
    ug = _dot(h, wug_ref[...])
    uv = _dot(h, wuv_ref[...])
    ug_ref[...] = ug
    uv_ref[...] = uv
    cwg = cwg_ref[...]
    cwv = cwv_ref[...]
    cg = cbg_ref[...] + cwg[0:1] * s0g_ref[...] + cwg[1:2] * s1g_ref[...] + cwg[2:3] * ug
    cv = cbv_ref[...] + cwv[0:1] * s0v_ref[...] + cwv[1:2] * s1v_ref[...] + cwv[2:3] * uv
    acc[...] += _dot((_gelu(cg) * cv).astype(BF16), wd_ref[...])

    @pl.when(j == nf - 1)
    def _():
        o_ref[...] = _ffn_tail(acc[...], x_ref[...], gpost_ref[...], p_ref[...], wg_ref, wp_ref)


def ffn_step(x, p, gpre, w_up, conv_w, conv_b, w_down, gpost, w_gate, w_proj, state, f=512):
    n, d = x.shape
    dff = w_down.shape[0]
    nf = dff // f
    pd = p.shape[1]
    w_up = w_up.astype(BF16)
    conv_b = conv_b.reshape(1, 2 * dff)
    s0, s1 = state[:, 0, :], state[:, 1, :]
    const = lambda j: (0, 0)
    lo = lambda j: (0, j)
    hi = lambda j: (0, nf + j)
    out, ug, uv = pl.pallas_call(
        functools.partial(_ffn_step_kernel, nf=nf),
        grid=(nf,),
        in_specs=[pl.BlockSpec((n, d), const),
                  pl.BlockSpec((n, pd), const),
                  pl.BlockSpec((1, d), const),
                  pl.BlockSpec((d, f), lo), pl.BlockSpec((d, f), hi),
                  pl.BlockSpec((3, f), lo), pl.BlockSpec((3, f), hi),
                  pl.BlockSpec((1, f), lo), pl.BlockSpec((1, f), hi),
                  pl.BlockSpec((f, d), lambda j: (j, 0)),
                  pl.BlockSpec((1, d), const),
                  pl.BlockSpec((d, d), const),
                  pl.BlockSpec((pd, d), const),
                  pl.BlockSpec((n, f), lo), pl.BlockSpec((n, f), hi),
                  pl.BlockSpec((n, f), lo), pl.BlockSpec((n, f), hi)],
        out_specs=[pl.BlockSpec((n, d), const), pl.BlockSpec((n, f), lo), pl.BlockSpec((n, f), lo)],
        out_shape=[jax.ShapeDtypeStruct((n, d), F32), jax.ShapeDtypeStruct((n, dff), F32),
                   jax.ShapeDtypeStruct((n, dff), F32)],
        scratch_shapes=[pltpu.VMEM((n, d), F32)],
        compiler_params=_cp("arbitrary"),
        name="ffn_step",
    )(x, p, gpre.reshape(1, d), w_up, w_up, conv_w, conv_w, conv_b, conv_b, w_down.astype(BF16),
      gpost.reshape(1, d), w_gate.astype(BF16), w_proj.astype(BF16), s0, s0, s1, s1)
    new_state = jnp.stack([s1, jnp.concatenate([ug, uv], axis=1)], axis=1)
    return out, new_state


def _lru_gates(xc, wa_ref, ba_ref, wx_ref, bx_ref, lam_ref):
    xb = xc.astype(BF16)
    r = _sigmoid(_dot(xb, wa_ref[...]) + ba_ref[...])
    i = _sigmoid(_dot(xb, wx_ref[...]) + bx_ref[...])
    lam = lam_ref[...]
    softplus_neg = jnp.maximum(-lam, 0.0) + jnp.log1p(jnp.exp(-jnp.abs(lam)))
    log_a = -8.0 * softplus_neg * r
    a = jnp.exp(log_a)
    th = jnp.tanh(log_a)
    u = jnp.sqrt(-2.0 * th / (1.0 - th)) * (i * xc)
    return a, u


def _lru_kernel(gate_ref, x_ref, cw_ref, cb_ref, wa_ref, ba_ref, wx_ref, bx_ref, lam_ref,
                y_ref, conv_ref, hlast_ref, xcarry, hcarry, *, tl):
    t = pl.program_id(1)

    @pl.when(t == 0)
    def _():
        xcarry[...] = jnp.zeros_like(xcarry)
        hcarry[...] = jnp.zeros_like(hcarry)

    x = x_ref[...]
    c = xcarry[...]
    cw = cw_ref[...]
    xs1 = _shift_rows(x, 1, [c[7:8]])
    xs2 = _shift_rows(x, 2, [c[6:7], c[7:8]])
    xs3 = _shift_rows(x, 3, [c[5:6], c[6:7], c[7:8]])
    xc = cb_ref[...] + cw[0:1] * xs3 + cw[1:2] * xs2 + cw[2:3] * xs1 + cw[3:4] * x
    a, u = _lru_gates(xc, wa_ref, ba_ref, wx_ref, bx_ref, lam_ref)
    row = lax.broadcasted_iota(jnp.int32, a.shape, 0)
    d = 1
    while d < tl:
        a_s = jnp.where(row < d, 1.0, pltpu.roll(a, d, 0))
        u_s = jnp.where(row < d, 0.0, pltpu.roll(u, d, 0))
        u = u + a * u_s
        a = a * a_s
        d *= 2
    h = a * hcarry[...] + u
    y_ref[...] = h * _gelu(gate_ref[...])
    hcarry[...] = h[tl - 1:tl, :]
    hlast_ref[0] = h[tl - 1:tl, :]
    xcarry[...] = x[tl - 8:, :]
    conv_ref[0] = x[tl - 3:, :]


def _block_diag(w):
    h, a, b = w.shape
    eye = jnp.eye(h, dtype=w.dtype)
    return (eye[:, None, :, None] * w[:, :, None, :]).reshape(h * a, h * b)


def lru_prompt(gate, x, conv_w, conv_b, w_a, b_a, w_x, b_x, lam, batch, tl=256):
    n, c = x.shape
    nt = n // batch // tl
    row = lambda b, t: (b * nt + t, 0)
    const = lambda b, t: (0, 0)
    vec = pl.BlockSpec((1, c), const)
    y, conv, hlast = pl.pallas_call(
        functools.partial(_lru_kernel, tl=tl),
        grid=(batch, nt),
        in_specs=[pl.BlockSpec((tl, c), row), pl.BlockSpec((tl, c), row),
                  pl.BlockSpec((4, c), const), vec,
                  pl.BlockSpec((c, c), const), vec, pl.BlockSpec((c, c), const), vec, vec],
        out_specs=[pl.BlockSpec((tl, c), row),
                   pl.BlockSpec((1, 3, c), lambda b, t: (b, 0, 0)),
                   pl.BlockSpec((1, 1, c), lambda b, t: (b, 0, 0))],
        out_shape=[jax.ShapeDtypeStruct((n, c), F32), jax.ShapeDtypeStruct((batch, 3, c), F32),
                   jax.ShapeDtypeStruct((batch, 1, c), F32)],
        scratch_shapes=[pltpu.VMEM((8, c), F32), pltpu.VMEM((1, c), F32)],
        compiler_params=_cp("arbitrary", "arbitrary"),
        name="lru_prompt",
    )(gate, x, conv_w, conv_b.reshape(1, c), _block_diag(w_a).astype(BF16), b_a.reshape(1, c),
      _block_diag(w_x).astype(BF16), b_x.reshape(1, c), lam.reshape(1, c))
    return y, conv, hlast.reshape(batch, c)


def _lru_step_kernel(gate_ref, x_ref, s0_ref, s1_ref, s2_ref, h0_ref, cw_ref, cb_ref, wa_ref, ba_ref,
                     wx_ref, bx_ref, lam_ref, y_ref, h_ref):
    cw = cw_ref[...]
    x = x_ref[...]
    xc = cb_ref[...] + cw[0:1] * s0_ref[...] + cw[1:2] * s1_ref[...] + cw[2:3] * s2_ref[...] + cw[3:4] * x
    a, u = _lru_gates(xc, wa_ref, ba_ref, wx_ref, bx_ref, lam_ref)
    h = a * h0_ref[...] + u
    h_ref[...] = h
    y_ref[...] = h * _gelu(gate_ref[...])


def lru_step(gate, x, conv_state, h0, conv_w, conv_b, w_a, b_a, w_x, b_x, lam):
    n, c = x.shape
    y, h = pl.pallas_call(
        _lru_step_kernel,
        out_shape=[jax.ShapeDtypeStruct((n, c), F32), jax.ShapeDtypeStruct((n, c), F32)],
        name="lru_step",
    )(gate, x, conv_state[:, 0], conv_state[:, 1], conv_state[:, 2], h0, conv_w, conv_b.reshape(1, c),
      _block_diag(w_a).astype(BF16), b_a.reshape(1, c), _block_diag(w_x).astype(BF16), b_x.reshape(1, c),
      lam.reshape(1, c))
    new_conv = jnp.stack([conv_state[:, 1], conv_state[:, 2], x], axis=1)
    return y, new_conv, h


N_BUCKETS = 32
T5_MAX_EXACT = 16
T5_MAX_DISTANCE = 128
NEG = -1e30
LOG2E = math.log2(math.e)
ATT_BLOCK = 256
MOBA_TOPK = 3


def _t5_bucket_np(rel):
    n = np.maximum(rel, 0)
    nf = np.maximum(n, 1).astype(np.float32)
    large = T5_MAX_EXACT + (np.log(nf / np.float32(T5_MAX_EXACT)) / np.float32(math.log(T5_MAX_DISTANCE / T5_MAX_EXACT))
                            * np.float32(N_BUCKETS - T5_MAX_EXACT)).astype(np.int32)
    large = np.minimum(large, N_BUCKETS - 1)
    return np.where(n < T5_MAX_EXACT, n, large).astype(np.int32)


def _prompt_bucket_table():
    r = np.arange(ATT_BLOCK)[:, None]
    c = np.arange(ATT_BLOCK)[None, :]
    tabs = []
    for o in range(2):
        rel = o * ATT_BLOCK + r - c
        tabs.append(np.where(rel >= 0, _t5_bucket_np(rel), -1))
    return np.stack(tabs).astype(np.int32)


def _bias_from_buckets(bucket, t5_ref, col):
    out = jnp.full(bucket.shape, NEG, F32)
    for b in range(N_BUCKETS):
        out = jnp.where(bucket == b, t5_ref[b, col], out)
    return out


def _attn_kernel(*refs, mode, nq, lam_init):
    if mode == "moba":
        (t5_ref, q_ref, kb, vb, bk_ref, kmean, o_ref, bias, qs_s, m_s, acc_s, sel_s, va) = refs
    else:
        (t5_ref, lam_ref, q_ref, kb, vb, bk_ref, sub_ref, o_ref, bias, qs_s, m_s, acc_s, l_s) = refs
    g = pl.program_id(1)
    qi = pl.program_id(2)
    blk = ATT_BLOCK
    lane = lax.broadcasted_iota(jnp.int32, (blk, 128), 1)
    upper = lane >= 64

    @pl.when(qi == 0)
    def _():
        if mode == "moba":
            def fill(j, carry):
                rows = pl.ds(pl.multiple_of(j * blk, blk), blk)
                v = vb[rows, :]
                one = jnp.ones_like(v)
                va[0, rows, :] = jnp.where(upper, one, v)
                va[1, rows, :] = jnp.where(upper, v, one)
                return carry
            lax.fori_loop(0, nq, fill, 0)
        for m in range(2):
            col = 2 * g + m
            bias[m, 0] = _bias_from_buckets(bk_ref[0], t5_ref, col) * LOG2E
            bias[m, 1] = _bias_from_buckets(bk_ref[1], t5_ref, col) * LOG2E
            bias[m, 2] = jnp.full((blk, blk), t5_ref[N_BUCKETS - 1, col] * LOG2E, F32)

    for h2 in range(2):
        q = q_ref[h2 * blk:(h2 + 1) * blk, :]
        qb = 2 * qi + h2
        for m in range(2):
            c = 2 * h2 + m
            qm = jnp.where(upper, q, 0.0) if m else jnp.where(upper, 0.0, q)
            qs_s[c] = (qm * (0.125 * LOG2E)).astype(BF16)
            m_s[c] = jnp.full((blk, 128), NEG, F32)
            acc_s[c] = jnp.zeros((blk, 128), F32)
            if mode != "moba":
                l_s[c] = jnp.zeros((blk, 128), F32)
            if mode == "moba":
                nb = kmean.shape[0]
                gate = _nt(kmean[...], qm)
                bi = lax.broadcasted_iota(jnp.int32, (nb, blk), 0)
                gt = jnp.where(bi < qb, gate, -jnp.inf)
                sel = jnp.zeros((nb, blk), F32)
                for _ in range(MOBA_TOPK):
                    mx = jnp.max(gt, axis=0, keepdims=True)
                    cand = jnp.where((gt == mx) & (mx > -jnp.inf), bi, nb)
                    first = jnp.min(cand, axis=0, keepdims=True)
                    pick = bi == first
                    sel = jnp.where(pick, 1.0, sel)
                    gt = jnp.where(pick, -jnp.inf, gt)
                sel = jnp.concatenate([sel, jnp.zeros((128 - nb, blk), F32)], axis=0)
                sel_s[c] = sel.T.astype(BF16)

    def chain(c, j, dd, masked):
        m = c % 2
        rows = pl.ds(pl.multiple_of(j * blk, blk), blk)
        s = lax.dot_general(qs_s[c], kb[rows, :], NT, preferred_element_type=F32)
        if masked:
            oh = jnp.where(lax.broadcasted_iota(jnp.int32, (128, 128), 0) == j, 1.0, 0.0).astype(BF16)
            hit = _dot(sel_s[c], oh)
        yield
        s = s + bias[m, dd]
        if masked:
            col = (hit - 1.0) * (-NEG)
            s = s + jnp.concatenate([col, col], axis=1)
        mp = m_s[c]
        mn = jnp.maximum(mp, jnp.max(s, axis=1, keepdims=True))
        yield
        alpha = jnp.exp2(mp - mn)
        p = jnp.exp2(s - jnp.concatenate([mn, mn], axis=1))
        if mode == "moba":
            pv = _dot(p.astype(BF16), va[m, rows, :])
        else:
            pv = _dot(p.astype(BF16), vb[rows, :])
        yield
        if mode != "moba":
            l_s[c] = alpha * l_s[c] + jnp.sum(p, axis=1, keepdims=True)
        acc_s[c] = alpha * acc_s[c] + pv
        m_s[c] = mn

    _round_robin([chain(2 + m, 2 * qi + 1, 0, False) for m in range(2)])
    masked = mode == "moba"
    _round_robin([chain(m, 2 * qi, 0, False) for m in range(2)]
                 + [chain(2 + m, 2 * qi, 1, masked) for m in range(2)])

    def body(d, carry):
        _round_robin([chain(c, 2 * qi - d, jnp.minimum(d + c // 2, 2), masked) for c in range(4)])
        return carry
    lax.fori_loop(1, 2 * qi + 1, body, 0)

    for h2 in range(2):
        a0, a1 = acc_s[2 * h2], acc_s[2 * h2 + 1]
        if mode == "moba":
            out = jnp.where(upper, a1 / pltpu.roll(a1, 64, 1), a0 / pltpu.roll(a0, 64, 1))
        else:
            att = a0 / l_s[2 * h2] - lam_ref[0] * (a1 / l_s[2 * h2 + 1])
            out = _rms(att, sub_ref[...], 1e-5) * (1.0 - lam_init)
        o_ref[h2 * blk:(h2 + 1) * blk, :] = out


def attn_prompt(q, k, v, t5_bias, batch, mode, kmean=None, lam=None, subln_w=None, lam_init=0.0):
    n, c = q.shape
    t = n // batch
    nq = t // ATT_BLOCK
    ng = c // 128
    blk = ATT_BLOCK
    smem = pl.BlockSpec(memory_space=pltpu.SMEM)
    nsteps = nq // 2
    qspec = pl.BlockSpec((2 * blk, 128), lambda b, g, i: (b * nsteps + i, g))
    kvspec = pl.BlockSpec((t, 128), lambda b, g, i: (b, g))
    bkspec = pl.BlockSpec((2, blk, blk), lambda b, g, i: (0, 0, 0))
    scratch = [pltpu.VMEM((2, 3, blk, blk), F32),
               pltpu.VMEM((4, blk, 128), BF16)] + [pltpu.VMEM((4, blk, 128), F32)] * 2
    bk = jnp.asarray(_prompt_bucket_table())
    if mode == "moba":
        in_specs = [smem, qspec, kvspec, kvspec, bkspec, pl.BlockSpec((nq, 128), lambda b, g, i: (b, g))]
        args = (t5_bias, q, k, v, bk, kmean)
        scratch += [pltpu.VMEM((4, blk, 128), BF16), pltpu.VMEM((2, t, 128), BF16)]
    else:
        scratch += [pltpu.VMEM((4, blk, 128), F32)]
        in_specs = [smem, smem, qspec, kvspec, kvspec, bkspec, pl.BlockSpec((1, 128), lambda b, g, i: (0, 0))]
        args = (t5_bias, lam.reshape(1), q, k, v, bk, subln_w.reshape(1, 128))
    return pl.pallas_call(
        functools.partial(_attn_kernel, mode=mode, nq=nq, lam_init=lam_init),
        grid=(batch, ng, nsteps),
        in_specs=in_specs,
        out_specs=qspec,
        out_shape=jax.ShapeDtypeStruct((n, c), F32),
        scratch_shapes=scratch,
        compiler_params=_cp("arbitrary", "arbitrary", "arbitrary"),
        name="attn_" + mode,
    )(*args)


RWKV_COLS = 1792
RWKV_LN_EPS = 64e-5


def _seg_ones(n):
    r = lax.broadcasted_iota(jnp.int32, (n, n), 0) // 64
    c = lax.broadcasted_iota(jnp.int32, (n, n), 1) // 64
    return jnp.where(r == c, 1.0, 0.0).astype(F32)


def _rwkv_prep_math(z, z_prev, mu_ref, w0_ref, w2_ref, a0_ref, a2_ref, g2_ref, kk_ref, ka_ref, outs):
    r_ref, e_ref, kkn_ref, ab_ref, k2_ref, v_ref, gate_ref = outs
    g = GROUP
    zs = z + mu_ref[...] * (z_prev - z)
    k = zs[:, g:2 * g]
    lora = zs[:, 3 * g:3 * g + 128]
    wlin = w0_ref[...] + _dot(jnp.tanh(lora).astype(BF16), w2_ref[...])
    softplus_neg = jnp.maximum(-wlin, 0.0) + jnp.log1p(jnp.exp(-jnp.abs(wlin)))
    a = _sigmoid(a0_ref[...] + _dot(lora.astype(BF16), a2_ref[...]))
    kk = k * kk_ref[...]
    norm = jnp.sqrt(_seg_sum(kk * kk, g))
    kk = kk / jnp.maximum(norm, 1e-12)
    r_ref[...] = zs[:, :g]
    e_ref[...] = jnp.exp(-softplus_neg - 0.5)
    kkn_ref[...] = kk
    ab_ref[...] = kk * a
    k2_ref[...] = k * (1.0 + (a - 1.0) * ka_ref[...])
    v_ref[...] = zs[:, 2 * g:3 * g]
    gate_ref[...] = _dot(_sigmoid(zs[:, 3 * g + 128:]).astype(BF16), g2_ref[...])


def _rwkv_prep_kernel(z_ref, mu_ref, w0_ref, w2_ref, a0_ref, a2_ref, g2_ref, kk_ref, ka_ref, *rest):
    outs, zcarry = rest[:7], rest[7]
    t = pl.program_id(1)

    @pl.when(t == 0)
    def _():
        zcarry[...] = jnp.zeros_like(zcarry)

    z = z_ref[...]
    z_prev = _shift_rows(z, 1, [zcarry[7:8, :]])
    zcarry[...] = z[z.shape[0] - 8:, :]
    _rwkv_prep_math(z, z_prev, mu_ref, w0_ref, w2_ref, a0_ref, a2_ref, g2_ref, kk_ref, ka_ref, outs)


def _rwkv_prep_step_kernel(z_ref, zp_ref, mu_ref, w0_ref, w2_ref, a0_ref, a2_ref, g2_ref, kk_ref, ka_ref, *outs):
    _rwkv_prep_math(z_ref[...], zp_ref[...], mu_ref, w0_ref, w2_ref, a0_ref, a2_ref, g2_ref, kk_ref, ka_ref, outs)


def _rwkv_prep_params(mu, w0, w2, a0, a2, g2, k_k, k_a):
    g = GROUP
    zero = jnp.zeros_like(w2)
    return (mu.reshape(1, RWKV_COLS), w0.reshape(1, g), jnp.concatenate([w2, zero], 0).astype(BF16),
            a0.reshape(1, g), jnp.concatenate([zero, a2], 0).astype(BF16), g2.astype(BF16),
            k_k.reshape(1, g), k_a.reshape(1, g))


def rwkv_prep(z, batch, params, shift=None, tl=256):
    n = z.shape[0]
    g = GROUP
    out_shape = [jax.ShapeDtypeStruct((n, g), F32)] * 7
    if shift is not None:
        return pl.pallas_call(_rwkv_prep_step_kernel, out_shape=out_shape, name="rwkv_prep_step")(z, shift, *params)
    nt = n // batch // tl
    row = lambda b, t: (b * nt + t, 0)
    const = lambda b, t: (0, 0)
    vec = pl.BlockSpec((1, g), const)
    lora = pl.BlockSpec((128, g), const)
    return pl.pallas_call(
        _rwkv_prep_kernel,
        grid=(batch, nt),
        in_specs=[pl.BlockSpec((tl, RWKV_COLS), row), pl.BlockSpec((1, RWKV_COLS), const),
                  vec, lora, vec, lora, lora, vec, vec],
        out_specs=[pl.BlockSpec((tl, g), row)] * 7,
        out_shape=out_shape,
        scratch_shapes=[pltpu.VMEM((8, RWKV_COLS), F32)],
        compiler_params=_cp("arbitrary", "arbitrary"),
        name="rwkv_prep",
    )(z, *params)


NN = (((1,), (0,)), ((), ()))
NT = (((1,), (1,)), ((), ()))
TN = (((0,), (0,)), ((), ()))


def _nt(a, b):
    return lax.dot_general(a, b, NT, precision=HI, preferred_element_type=F32)


def _split(x):
    hi = x.astype(BF16)
    return hi, (x - hi.astype(F32)).astype(BF16)


def _mm3(a, b, dims):
    ah, al = a if isinstance(a, tuple) else _split(a)
    bh, bl = b if isinstance(b, tuple) else _split(b)
    dg = functools.partial(lax.dot_general, dimension_numbers=dims, preferred_element_type=F32)
    return dg(ah, bh) + (dg(ah, bl) + dg(al, bh))


def _seg_sum(x, n):
    ones = _seg_ones(n).astype(BF16)
    hi, lo = _split(x)
    lo2 = (x - hi.astype(F32) - lo.astype(F32)).astype(BF16)
    return _dot(hi, ones) + (_dot(lo, ones) + _dot(lo2, ones))


def _rwkv_chunk_pair(r, e, kk, ab, k2, v, c):
    shape = (c, 128)
    upper = lax.broadcasted_iota(jnp.int32, shape, 1) >= 64
    row = lax.broadcasted_iota(jnp.int32, shape, 0)
    cum = e
    d = 1
    while d < c:
        cum = cum + jnp.where(row < d, 0.0, pltpu.roll(cum, d, 0))
        d *= 2
    g_inv = jnp.exp(cum)
    at = -kk * jnp.exp(e - cum)
    bt = ab * g_inv
    kt = k2 * g_inv
    rt = r * jnp.exp(-cum)
    g_end = jnp.exp(-cum[c - 1:c, :])

    def stack(x):
        return jnp.concatenate([jnp.where(upper, 0.0, x), jnp.where(upper, x, 0.0)], axis=0)

    def fold(x):
        return x[:c, :] + x[c:, :]

    a_st, r_st, v_st = _split(stack(at)), _split(stack(rt)), _split(stack(v))
    b2 = _split(jnp.concatenate([bt, bt], axis=0))
    k2s = _split(jnp.concatenate([kt, kt], axis=0))
    ri = lax.broadcasted_iota(jnp.int32, (2 * c, 2 * c), 0)
    ci = lax.broadcasted_iota(jnp.int32, (2 * c, 2 * c), 1)
    same = (ri >= c) == (ci >= c)
    strict = same & (ri > ci)
    incl = same & (ri >= ci)
    lab = jnp.where(strict, _mm3(a_st, b2, NT), 0.0)
    lak = jnp.where(strict, _mm3(a_st, k2s, NT), 0.0)
    rb = _split(jnp.where(incl, _mm3(r_st, b2, NT), 0.0))
    rk = jnp.where(incl, _mm3(r_st, k2s, NT), 0.0)
    x = jnp.where(ri == ci, 1.0, 0.0) + lab
    p = lab
    lv = _mm3(lak, v_st, NN)
    yield
    n = 2
    while n < c:
        ps = _split(p)
        p = _mm3(ps, ps, NN)
        x = x + _mm3(x, p, NN)
        n *= 2
        yield
    xs = _split(x)
    pa_st = _mm3(xs, a_st, NN)
    q_st = _mm3(xs, lv, NN)
    yield
    pa = fold(pa_st)
    q = fold(q_st)
    y1 = rt + fold(_mm3(rb, pa_st, NN))
    y0 = fold(_mm3(rb, q_st, NN) + _mm3(rk, v_st, NN))
    bg = _split(bt * g_end)
    r2 = lax.broadcasted_iota(jnp.int32, (128, 128), 0)
    c2 = lax.broadcasted_iota(jnp.int32, (128, 128), 1)
    same_head = (r2 >= 64) == (c2 >= 64)
    m = jnp.where(r2 == c2, g_end, 0.0) + jnp.where(same_head, _mm3(bg, pa, TN), 0.0)
    nn = jnp.where(same_head, _mm3(bg, q, TN) + _mm3(kt * g_end, v, TN), 0.0)
    return m, nn, y1, y0


def _rwkv_chunk_kernel(r_ref, e_ref, kk_ref, ab_ref, k2_ref, v_ref, m_ref, n_ref, y1_ref, y0_ref, *, c):
    lanes = [slice(g * 128, (g + 1) * 128) for g in range(GROUP // 128)]
    chains = [_rwkv_chunk_pair(r_ref[:, ls], e_ref[:, ls], kk_ref[:, ls], ab_ref[:, ls], k2_ref[:, ls],
                               v_ref[:, ls], c) for ls in lanes]
    for g, (ls, (m, nn, y1, y0)) in enumerate(zip(lanes, _round_robin(chains))):
        m_ref[0, g] = m
        n_ref[0, g] = nn
        y1_ref[:, ls] = y1
        y0_ref[:, ls] = y0


def _rwkv_scan_kernel(m_ref, n_ref, y1_ref, y0_ref, r_ref, k2_ref, v_ref, gate_ref, h0_ref,
                      lnw_ref, lnb_ref, rk_ref, y_ref, hout_ref, h):
    t = pl.program_id(1)

    @pl.when(t == 0)
    def _():
        h[...] = h0_ref[0]

    def pair(g):
        ls = slice(g * 128, (g + 1) * 128)
        hg = _split(h[g])
        h[g] = _mm3(m_ref[0, g], hg, NN) + n_ref[0, g]
        y = _mm3(y1_ref[:, ls], hg, NN) + y0_ref[:, ls]
        bonus = _seg_sum(r_ref[:, ls] * k2_ref[:, ls] * rk_ref[:, ls], 128) * v_ref[:, ls]
        yield
        mean = _seg_sum(y, 128) * (1.0 / 64)
        yield
        yc = y - mean
        var = _seg_sum(yc * yc, 128) * (1.0 / 64)
        yield
        yn = yc * lax.rsqrt(var + RWKV_LN_EPS) * lnw_ref[:, ls] + lnb_ref[:, ls]
        y_ref[:, ls] = (yn + bonus) * gate_ref[:, ls]

    _round_robin([pair(g) for g in range(GROUP // 128)])
    hout_ref[0] = h[...]


def rwkv_scan(r, e, kk, ab, k2, v, gate, s0, ln_w, ln_b, r_k, batch, c):
    n, g = r.shape
    nc = n // batch // c
    npair = g // 128
    row = lambda b, t: (b * nc + t, 0)
    blk = pl.BlockSpec((c, g), row)
    mat = pl.BlockSpec((1, npair, 128, 128), lambda b, t: (b * nc + t, 0, 0, 0))
    m, nn, y1, y0 = pl.pallas_call(
        functools.partial(_rwkv_chunk_kernel, c=c),
        grid=(batch, nc),
        in_specs=[blk] * 6,
        out_specs=[mat, mat, blk, blk],
        out_shape=[jax.ShapeDtypeStruct((batch * nc, npair, 128, 128), F32)] * 2
        + [jax.ShapeDtypeStruct((n, g), F32)] * 2,
        compiler_params=_cp("arbitrary", "arbitrary"),
        name="rwkv_chunk",
    )(r, e, kk, ab, k2, v)
    st = jnp.swapaxes(s0, -1, -2).reshape(batch, npair, 2, 64, 64)
    eye2 = jnp.eye(2, dtype=F32)
    h0 = (st[:, :, :, :, None, :] * eye2[None, None, :, None, :, None]).reshape(batch, npair, 128, 128)
    state = pl.BlockSpec((1, npair, 128, 128), lambda b, t: (b, 0, 0, 0))
    vec = pl.BlockSpec((1, g), lambda b, t: (0, 0))
    y, hout = pl.pallas_call(
        _rwkv_scan_kernel,
        grid=(batch, nc),
        in_specs=[mat, mat, blk, blk, blk, blk, blk, blk, state, vec, vec, vec],
        out_specs=[blk, state],
        out_shape=[jax.ShapeDtypeStruct((n, g), F32), jax.ShapeDtypeStruct((batch, npair, 128, 128), F32)],
        scratch_shapes=[pltpu.VMEM((npair, 128, 128), F32)],
        compiler_params=_cp("arbitrary", "arbitrary"),
        name="rwkv_scan",
    )(m, nn, y1, y0, r, k2, v, gate, h0, ln_w.reshape(1, g), ln_b.reshape(1, g), r_k.reshape(1, g))
    hb = hout.reshape(batch, npair, 2, 64, 2, 64)
    s_last = jnp.stack([hb[:, :, 0, :, 0, :], hb[:, :, 1, :, 1, :]], axis=2).reshape(batch, 2 * npair, 64, 64)
    return y, jnp.swapaxes(s_last, -1, -2)


PAGES_PER_STEP = 16


def _past_bucket_row(first_pos, width, q_pos):
    rel = q_pos - (first_pos + np.arange(width))
    return _t5_bucket_np(rel).astype(np.int32).reshape(1, width)


def _bias_rows(bucket_row, t5t_ref):
    out = jnp.zeros((8, bucket_row.shape[1]), F32)
    for b in range(N_BUCKETS):
        out = jnp.where(bucket_row == b, t5t_ref[:, b:b + 1], out)
    return out


def _page_logits(qcol, kt):
    return jnp.sum(qcol * kt, axis=1)


def _moba_scan_kernel(pt_ref, q_ref, *refs, n_steps, n_blocks):
    pages = refs[:PAGES_PER_STEP]
    lg_ref, idx_ref, gate_s = refs[PAGES_PER_STEP:]
    s = pl.program_id(1)

    @pl.when(s == 0)
    def _():
        gate_s[...] = jnp.zeros_like(gate_s)

    qcol = q_ref[0]
    lane = lax.broadcasted_iota(jnp.int32, (8, 128), 1)
    gate = gate_s[...]
    for i, pg in enumerate(pages):
        lg = _page_logits(qcol, pg[0])
        blk, half = divmod(i, 2)
        lg_ref[0, :, blk, half * PAGE:(half + 1) * PAGE] = lg * 0.125
        gate = gate + jnp.where(lane == s * (PAGES_PER_STEP // 2) + blk, jnp.sum(lg, axis=1, keepdims=True), 0.0)
    gate_s[...] = gate

    @pl.when(s == n_steps - 1)
    def _():
        gt = jnp.where(lane < n_blocks, gate * (1.0 / ATT_BLOCK), -jnp.inf)
        out = jnp.zeros((8, 128), jnp.int32)
        for j in range(MOBA_TOPK):
            mx = jnp.max(gt, axis=1, keepdims=True)
            first = jnp.min(jnp.where(gt == mx, lane, 128), axis=1, keepdims=True)
            out = jnp.where(lane == j, first, out)
            gt = jnp.where(lane == first, -jnp.inf, gt)
        idx_ref[0] = out


def _moba_gather_kernel(idx_ref, pt_ref, t5_ref, q_ref, kn_ref, vn_ref, lg_ref, bk_ref, *refs, n_blocks):
    vpages = refs[:2 * MOBA_TOPK]
    o_ref = refs[2 * MOBA_TOPK]
    b = pl.program_id(0)
    h = pl.program_id(1)
    q = q_ref[0, 0]
    own = jnp.sum(q * kn_ref[0, 0], axis=1, keepdims=True) * 0.125 + t5_ref[0, h]
    far = t5_ref[N_BUCKETS - 1, h]
    near = far
    for bkt in range(N_BUCKETS):
        near = jnp.where(bk_ref[...] == bkt, t5_ref[bkt, h], near)
    logits = []
    for s in range(MOBA_TOPK):
        blk = idx_ref[b, h * MOBA_TOPK + s]
        lg = lg_ref[0, 0, pl.ds(blk, 1), :]
        logits.append(lg + jnp.where(blk == n_blocks - 1, near, far))
    mx = own
    for lg in logits:
        mx = jnp.maximum(mx, jnp.max(lg, axis=1, keepdims=True))
    p_own = jnp.exp(own - mx)
    den = p_own
    acc = p_own * vn_ref[0, 0]
    for s, lg in enumerate(logits):
        p = jnp.exp(lg - mx)
        den = den + jnp.sum(p, axis=1, keepdims=True)
        for pg in range(2):
            acc = acc + _nt(p[:, pg * PAGE:(pg + 1) * PAGE], vpages[s * 2 + pg][0, 0])
    o_ref[0, 0] = acc / den


def moba_step(q, k_new, v_new, cache_k, cache_v, page_table, t5_bias):
    nb, c = q.shape
    nh = c // 64
    n_pages = page_table.shape[1]
    n_steps = n_pages // PAGES_PER_STEP
    n_blocks = n_pages * PAGE // ATT_BLOCK
    bps = PAGES_PER_STEP // 2
    qcol = jnp.broadcast_to(q.reshape(nb, nh, 64, 1), (nb, nh, 64, PAGE))

    def page_spec(i):
        return pl.BlockSpec((1, nh, 64, PAGE), lambda b, s, pt: (pt[b, s * PAGES_PER_STEP + i], 0, 0, 0))

    logits, idx = pl.pallas_call(
        functools.partial(_moba_scan_kernel, n_steps=n_steps, n_blocks=n_blocks),
        grid_spec=pltpu.PrefetchScalarGridSpec(
            num_scalar_prefetch=1,
            grid=(nb, n_steps),
            in_specs=[pl.BlockSpec((1, nh, 64, PAGE), lambda b, s, pt: (b, 0, 0, 0))]
            + [page_spec(i) for i in range(PAGES_PER_STEP)],
            out_specs=[pl.BlockSpec((1, nh, bps, ATT_BLOCK), lambda b, s, pt: (b, 0, s, 0)),
                       pl.BlockSpec((1, nh, 128), lambda b, s, pt: (b, 0, 0))],
            scratch_shapes=[pltpu.VMEM((nh, 128), F32)]),
        out_shape=[jax.ShapeDtypeStruct((nb, nh, n_blocks, ATT_BLOCK), F32),
                   jax.ShapeDtypeStruct((nb, nh, 128), jnp.int32)],
        compiler_params=_cp("arbitrary", "arbitrary"),
        name="moba_scan",
    )(page_table, qcol, *([cache_k] * PAGES_PER_STEP))
    sel = idx[:, :, :MOBA_TOPK].reshape(nb, nh * MOBA_TOPK)
    bucket = jnp.asarray(_past_bucket_row((n_blocks - 1) * ATT_BLOCK, ATT_BLOCK, n_pages * PAGE))

    def vpage_spec(s, pg):
        def imap(b, h, sel, pt):
            return (pt[b, 2 * sel[b, h * MOBA_TOPK + s] + pg], h, 0, 0)
        return pl.BlockSpec((1, 1, 64, PAGE), imap)

    row = pl.BlockSpec((1, 1, 1, 64), lambda b, h, sel, pt: (b, h, 0, 0))
    out = pl.pallas_call(
        functools.partial(_moba_gather_kernel, n_blocks=n_blocks),
        grid_spec=pltpu.PrefetchScalarGridSpec(
            num_scalar_prefetch=2,
            grid=(nb, nh),
            in_specs=[pl.BlockSpec(memory_space=pltpu.SMEM), row, row, row,
                      pl.BlockSpec((1, 1, n_blocks, ATT_BLOCK), lambda b, h, sel, pt: (b, h, 0, 0)),
                      pl.BlockSpec((1, ATT_BLOCK), lambda b, h, sel, pt: (0, 0))]
            + [vpage_spec(s, pg) for s in range(MOBA_TOPK) for pg in range(2)],
            out_specs=row),
        out_shape=jax.ShapeDtypeStruct((nb, nh, 1, 64), F32),
        compiler_params=_cp("arbitrary", "arbitrary"),
        name="moba_gather",
    )(sel, page_table, t5_bias, q.reshape(nb, nh, 1, 64), k_new.reshape(nb, nh, 1, 64),
      v_new.reshape(nb, nh, 1, 64), logits, bucket, *([cache_v] * (2 * MOBA_TOPK)))
    return out.reshape(nb, c)


def _diff_step_kernel(pt_ref, lam_ref, qcol_ref, q_ref, kn_ref, vn_ref, t5t_ref, bk_ref, sub_ref, spread_ref,
                      *refs, n_steps, lam_init):
    kpages = refs[:PAGES_PER_STEP]
    vpages = refs[PAGES_PER_STEP:2 * PAGES_PER_STEP]
    o_ref, m_s, l_s, acc_s = refs[2 * PAGES_PER_STEP:]
    s = pl.program_id(1)
    nh = vn_ref.shape[1]
    head_of_row = lax.broadcasted_iota(jnp.int32, (2 * nh, 128), 0) // 2
    own_head = (lax.broadcasted_iota(jnp.int32, (2 * nh, nh * PAGE), 1) % nh
                == lax.broadcasted_iota(jnp.int32, (2 * nh, nh * PAGE), 0) // 2)

    def per_head(rows):
        out = jnp.broadcast_to(rows[0], (2 * nh, 128))
        for h in range(1, nh):
            out = jnp.where(head_of_row == h, rows[h], out)
        return out

    @pl.when(s == 0)
    def _():
        m_s[...] = jnp.sum(q_ref[0] * kn_ref[0], axis=1, keepdims=True) * 0.125 + t5t_ref[:, 0:1]
        l_s[...] = jnp.ones_like(l_s)
        vn = vn_ref[0]
        acc_s[...] = per_head([vn[h:h + 1, :] for h in range(nh)])

    far = t5t_ref[:, N_BUCKETS - 1:N_BUCKETS]
    near = _bias_rows(bk_ref[...], t5t_ref)
    qcol = qcol_ref[0]
    m, l, acc = m_s[...], l_s[...], acc_s[...]
    for i in range(PAGES_PER_STEP):
        lg = _page_logits(qcol, kpages[i][0]) * 0.125
        if i == PAGES_PER_STEP - 1:
            lg = lg + jnp.where(s == n_steps - 1, near, far)
        else:
            lg = lg + far
        mn = jnp.maximum(m, jnp.max(lg, axis=1, keepdims=True))
        alpha = jnp.exp(m - mn)
        p = jnp.exp(lg - mn)
        l = alpha * l + jnp.sum(p, axis=1, keepdims=True)
        p_rows = jnp.where(own_head, _dot(p.astype(BF16), spread_ref[...]), 0.0).astype(BF16)
        acc = alpha * acc + _dot(p_rows, vpages[i][0].astype(BF16))
        m = mn
    m_s[...], l_s[...], acc_s[...] = m, l, acc

    @pl.when(s == n_steps - 1)
    def _():
        a = acc / l
        for h in range(nh):
            att = a[2 * h:2 * h + 1, :] - lam_ref[0] * a[2 * h + 1:2 * h + 2, :]
            o_ref[0, h:h + 1, :] = _rms(att, sub_ref[...], 1e-5) * (1.0 - lam_init)


def diff_step(q, k_new, v_new, cache_k, cache_v, page_table, t5_bias, lam, subln_w, lam_init):
    nb, c = q.shape
    nm = c // 64
    nh = nm // 2
    n_pages = page_table.shape[1]
    n_steps = n_pages // PAGES_PER_STEP
    bucket = jnp.asarray(_past_bucket_row((n_pages - 1) * PAGE, PAGE, n_pages * PAGE))
    qcol = jnp.broadcast_to(q.reshape(nb, nm, 64, 1), (nb, nm, 64, PAGE))

    def kpage_spec(i):
        return pl.BlockSpec((1, nm, 64, PAGE), lambda b, s, pt: (pt[b, s * PAGES_PER_STEP + i], 0, 0, 0))

    cache_v = cache_v.reshape(cache_v.shape[0], PAGE * nh, 128)
    spread = jnp.asarray(np.repeat(np.eye(PAGE, dtype=np.float32), nh, axis=1), BF16)

    def vpage_spec(i):
        return pl.BlockSpec((1, PAGE * nh, 128), lambda b, s, pt: (pt[b, s * PAGES_PER_STEP + i], 0, 0))

    maps = pl.BlockSpec((1, nm, 64), lambda b, s, pt: (b, 0, 0))
    heads = pl.BlockSpec((1, nh, 128), lambda b, s, pt: (b, 0, 0))
    out = pl.pallas_call(
        functools.partial(_diff_step_kernel, n_steps=n_steps, lam_init=lam_init),
        grid_spec=pltpu.PrefetchScalarGridSpec(
            num_scalar_prefetch=1,
            grid=(nb, n_steps),
            in_specs=[pl.BlockSpec(memory_space=pltpu.SMEM),
                      pl.BlockSpec((1, nm, 64, PAGE), lambda b, s, pt: (b, 0, 0, 0)), maps, maps, heads,
                      pl.BlockSpec((nm, N_BUCKETS), lambda b, s, pt: (0, 0)),
                      pl.BlockSpec((1, PAGE), lambda b, s, pt: (0, 0)),
                      pl.BlockSpec((1, 128), lambda b, s, pt: (0, 0)),
                      pl.BlockSpec((PAGE, PAGE * nh), lambda b, s, pt: (0, 0))]
            + [kpage_spec(i) for i in range(PAGES_PER_STEP)] + [vpage_spec(i) for i in range(PAGES_PER_STEP)],
            out_specs=heads,
            scratch_shapes=[pltpu.VMEM((nm, 1), F32), pltpu.VMEM((nm, 1), F32), pltpu.VMEM((nm, 128), F32)]),
        out_shape=jax.ShapeDtypeStruct((nb, nh, 128), F32),
        compiler_params=_cp("arbitrary", "arbitrary"),
        name="diff_step",
    )(page_table, lam.reshape(1), qcol, q.reshape(nb, nm, 64), k_new.reshape(nb, nm, 64), v_new.reshape(nb, nh, 128),
      t5_bias.T, bucket, subln_w.reshape(1, 128), spread, *([cache_k] * PAGES_PER_STEP),
      *([cache_v] * PAGES_PER_STEP))
    return out.reshape(nb, c)


def rwkv_mix(z, shift0, s0, mu, w0, w2, a0, a2, g2, k_k, k_a, r_k, ln_w, ln_b, batch, step):
    params = _rwkv_prep_params(mu, w0, w2, a0, a2, g2, k_k, k_a)
    if not step:
        outs = rwkv_prep(z, batch, params)
        r, e, kk, ab, k2, v, gate = outs
        return rwkv_scan(r, e, kk, ab, k2, v, gate, s0, ln_w, ln_b, r_k, batch, 64)
    outs = rwkv_prep(z, batch, params, shift=shift0)
    r, e, kk, ab, k2, v, gate = (jnp.pad(o[:, None, :], ((0, 0), (0, 7), (0, 0))).reshape(batch * 8, GROUP)
                                 for o in outs)
    y, s_last = rwkv_scan(r, e, kk, ab, k2, v, gate, s0, ln_w, ln_b, r_k, batch, 8)
    return y.reshape(batch, 8, GROUP)[:, 0], s_last


def _trunk(x, p, batch, step, st, W):
    depth = p.shape[0]
    outs = {k: [] for k in ("moba_k", "moba_v", "lru_conv", "lru_h", "diff_k", "diff_v",
                            "rwkv_shift", "rwkv_s", "ffn_conv")}
    t = x.shape[0] // batch
    for i in range(depth):
        j = i // 2
        g = GROUP
        n_pp = t // PAGE
        if i % 2 == 0:
            lru_w = (W["lru_conv_w"][j], W["lru_conv_b"][j], W["lru_w_a"][j], W["lru_b_a"][j],
                     W["lru_w_x"][j], W["lru_b_x"][j], W["lru_lambda"][j])
            cols = [("f32", c * g, g) for c in range(3)]
            if step:
                gate, xin, q, k, v = norm_linear(x, W["norm_mix_pre"][i], W["even_w_in"][j],
                                                 cols + [("f32", 3 * g, g), ("f32", 4 * g, g)])
                ya, conv, h_last = lru_step(gate, xin, st["lru_conv"][j], st["lru_h"][j], *lru_w)
                yb = moba_step(q, k, v, st["moba_k"][j], st["moba_v"][j], st["page_table"], W["t5_bias"])
                k_out, v_out = k.reshape(batch, t, 8, 64), v.reshape(batch, t, 8, 64)
            else:
                gate, xin, q, kb, vb, kt, vt, kmean = norm_linear(
                    x, W["norm_mix_pre"][i], W["even_w_in"][j],
                    cols + [("bf16", 3 * g, g), ("bf16", 4 * g, g), ("pages", 3 * g, g), ("pages", 4 * g, g),
                            ("blockmean", 3 * g, g)])
                ya, conv, h_last = lru_prompt(gate, xin, *lru_w, batch)
                blocks_per_tile = x.shape[0] // kmean.shape[0] // ATT_BLOCK
                kmean = kmean[:, :blocks_per_tile].reshape(-1, g)
                yb = attn_prompt(q, kb, vb, W["t5_bias"], batch, "moba", kmean=kmean)
                k_out = jnp.transpose(kt.reshape(batch, n_pp, 8, 64, PAGE), (0, 1, 4, 2, 3))
                v_out = jnp.transpose(vt.reshape(batch, n_pp, 8, 64, PAGE), (0, 1, 4, 2, 3))
            outs["moba_k"].append(k_out)
            outs["moba_v"].append(v_out)
            outs["lru_conv"].append(conv)
            outs["lru_h"].append(h_last)
            w_out = W["even_w_out"][j]
        else:
            rw = (W["rwkv_mu"][j], W["rwkv_w0"][j], W["rwkv_w2"][j], W["rwkv_a0"][j], W["rwkv_a2"][j],
                  W["rwkv_g2"][j], W["rwkv_k_k"][j], W["rwkv_k_a"][j], W["rwkv_r_k"][j],
                  W["rwkv_ln_w"][j], W["rwkv_ln_b"][j])
            lam_init = 0.8 - 0.6 * math.exp(-0.3 * i)
            lf = W["diff_lambda"][j]
            lam = jnp.exp(jnp.sum(lf[0] * lf[1])) - jnp.exp(jnp.sum(lf[2] * lf[3])) + lam_init
            o = RWKV_COLS
            cols = [("f32", 0, o), ("f32", o, g)]
            if step:
                z, q, k, v = norm_linear(x, W["norm_mix_pre"][i], W["odd_w_in"][j],
                                         cols + [("f32", o + g, g), ("f32", o + 2 * g, g)])
                ya, s_last = rwkv_mix(z, st["rwkv_shift"][j], st["rwkv_s"][j], *rw, batch, True)
                yb = diff_step(q, k, v, st["diff_k"][j], st["diff_v"][j], st["page_table"], W["t5_bias"],
                               lam, W["diff_subln_w"][j], lam_init)
                shift = z
                k_out = k.reshape(batch, t, 4, 2, 64)
            else:
                z, q, kb, vb, kt, v = norm_linear(
                    x, W["norm_mix_pre"][i], W["odd_w_in"][j],
                    cols + [("bf16", o + g, g), ("bf16", o + 2 * g, g), ("pages", o + g, g), ("f32", o + 2 * g, g)])
                s0 = jnp.zeros((batch, 8, 64, 64), F32)
                ya, s_last = rwkv_mix(z, None, s0, *rw, batch, False)
                yb = attn_prompt(q, kb, vb, W["t5_bias"], batch, "diff", lam=lam, subln_w=W["diff_subln_w"][j],
                                 lam_init=lam_init)
                shift = z.reshape(batch, t, RWKV_COLS)[:, t - 1]
                k_out = jnp.transpose(kt.reshape(batch, n_pp, 4, 2, 64, PAGE), (0, 1, 5, 2, 3, 4))
            outs["diff_k"].append(k_out)
            outs["diff_v"].append(v.reshape(k_out.shape[:-3] + (4, 128)))
            outs["rwkv_shift"].append(shift)
            outs["rwkv_s"].append(s_last)
            w_out = W["odd_w_out"][j]
        x = out_proj(ya, yb, w_out, x, W["norm_mix_post"][i])
        ffn_w = (W["norm_ffn_pre"][i], W["ffn_w_up"][i], W["ffn_conv_w"][i], W["ffn_conv_b"][i], W["ffn_w_down"][i],
                 W["norm_ffn_post"][i], W["ple_w_gate"][i], W["ple_w_proj"][i])
        if step:
            x, fbuf = ffn_step(x, p[i], *ffn_w, st["ffn_conv"][i])
        else:
            x, fbuf = ffn_prompt(x, p[i], *ffn_w, batch)
        outs["ffn_conv"].append(fbuf)
    return x, {k: jnp.stack(v) for k, v in outs.items()}


def kernel(x_prompt, x_sample, cache_moba_k, cache_moba_v, state_lru_conv, state_lru_h, cache_diff_k, cache_diff_v, state_rwkv_shift, state_rwkv, state_ffn_conv, page_table, p_prompt, p_sample, t5_bias, norm_mix_pre, norm_mix_post, norm_ffn_pre, norm_ffn_post, even_w_in, even_w_out, lru_conv_w, lru_conv_b, lru_w_a, lru_b_a, lru_w_x, lru_b_x, lru_lambda, odd_w_in, odd_w_out, rwkv_mu, rwkv_w0, rwkv_w2, rwkv_a0, rwkv_a2, rwkv_g2, rwkv_k_k, rwkv_k_a, rwkv_r_k, rwkv_ln_w, rwkv_ln_b, diff_lambda, diff_subln_w, ffn_w_up, ffn_conv_w, ffn_conv_b, ffn_w_down, ple_w_proj, ple_w_gate):
    W = dict(t5_bias=t5_bias, norm_mix_pre=norm_mix_pre, norm_mix_post=norm_mix_post,
             norm_ffn_pre=norm_ffn_pre, norm_ffn_post=norm_ffn_post,
             even_w_in=even_w_in, even_w_out=even_w_out, lru_conv_w=lru_conv_w, lru_conv_b=lru_conv_b,
             lru_w_a=lru_w_a, lru_b_a=lru_b_a, lru_w_x=lru_w_x, lru_b_x=lru_b_x, lru_lambda=lru_lambda,
             odd_w_in=odd_w_in, odd_w_out=odd_w_out, rwkv_mu=rwkv_mu, rwkv_w0=rwkv_w0, rwkv_w2=rwkv_w2,
             rwkv_a0=rwkv_a0, rwkv_a2=rwkv_a2, rwkv_g2=rwkv_g2, rwkv_k_k=rwkv_k_k, rwkv_k_a=rwkv_k_a,
             rwkv_r_k=rwkv_r_k, rwkv_ln_w=rwkv_ln_w, rwkv_ln_b=rwkv_ln_b,
             diff_lambda=diff_lambda, diff_subln_w=diff_subln_w,
             ffn_w_up=ffn_w_up, ffn_conv_w=ffn_conv_w, ffn_conv_b=ffn_conv_b, ffn_w_down=ffn_w_down,
             ple_w_proj=ple_w_proj, ple_w_gate=ple_w_gate)
    bp, tp, d = x_prompt.shape
    bs, ts, _ = x_sample.shape
    depth = p_prompt.shape[0]
    n_pp = tp // PAGE
    assert ts == 1, "the sample group is a single-token step"

    yp, P = _trunk(x_prompt.reshape(bp * tp, d), p_prompt.reshape(depth, bp * tp, -1), bp, False, None, W)

    pool = cache_moba_k.shape[1]
    st = dict(moba_k=jnp.transpose(cache_moba_k, (0, 1, 3, 4, 2)), moba_v=jnp.transpose(cache_moba_v, (0, 1, 3, 4, 2)),
              diff_k=jnp.transpose(cache_diff_k, (0, 1, 3, 4, 5, 2)).reshape(-1, pool, 8, 64, PAGE),
              diff_v=cache_diff_v,
              lru_conv=state_lru_conv, lru_h=state_lru_h, rwkv_shift=state_rwkv_shift, rwkv_s=state_rwkv,
              ffn_conv=state_ffn_conv, page_table=page_table)
    ys, S = _trunk(x_sample.reshape(bs * ts, d), p_sample.reshape(depth, bs * ts, -1), bs, True, st, W)

    return (yp.reshape(bp, tp, d), ys.reshape(bs, ts, d),
            P["moba_k"], P["moba_v"], S["moba_k"], S["moba_v"],
            P["lru_conv"], S["lru_conv"], P["lru_h"], S["lru_h"],
            P["diff_k"], P["diff_v"], S["diff_k"], S["diff_v"],
            P["rwkv_shift"], S["rwkv_shift"], P["rwkv_s"], S["rwkv_s"],
            P["ffn_conv"], S["ffn_conv"])
```

```python
import functools
import math

import numpy as np
import jax
import jax.numpy as jnp
from jax import lax
from jax.experimental import pallas as pl
from jax.experimental.pallas import tpu as pltpu

F32 = jnp.float32
BF16 = jnp.bfloat16
HI = lax.Precision.HIGHEST

D_MODEL = 1024
GROUP = 512
PAGE = 128
VMEM_LIMIT = 56 * 1024 * 1024


def _cp(*sem):
    return pltpu.CompilerParams(dimension_semantics=sem, vmem_limit_bytes=VMEM_LIMIT)


def _rms(x, g, eps):
    return x * lax.rsqrt(jnp.mean(x * x, axis=-1, keepdims=True) + eps) * g


def _gelu(x):
    return 0.5 * x * (1.0 + jnp.tanh(math.sqrt(2.0 / math.pi) * (x + 0.044715 * (x * x * x))))


def _sigmoid(x):
    return 1.0 / (1.0 + jnp.exp(-x))


def _dot(a, b):
    return jnp.dot(a, b, preferred_element_type=F32)


def _round_robin(chains):
    done = {}
    while len(done) < len(chains):
        for i, chain in enumerate(chains):
            if i not in done:
                try:
                    next(chain)
                except StopIteration as stop:
                    done[i] = stop.value
    return [done[i] for i in range(len(chains))]


def _shift_rows(x, d, fill):
    r = pltpu.roll(x, d, 0)
    row = lax.broadcasted_iota(jnp.int32, x.shape, 0)
    for i in range(d):
        r = jnp.where(row == i, fill[i], r)
    return r


def _norm_linear_kernel(x_ref, g_ref, w_ref, *out_refs, outs, tm):
    h = _rms(x_ref[...], g_ref[...], 1e-6).astype(BF16)
    ys = {}
    for o_ref, (kind, off, n) in zip(out_refs, outs):
        if (off, n) not in ys:
            ys[(off, n)] = _dot(h, w_ref[:, off:off + n])
        y = ys[(off, n)]
        if kind == "f32":
            o_ref[...] = y
        elif kind == "bf16":
            o_ref[...] = y.astype(BF16)
        elif kind == "pages":
            for pg in range(tm // PAGE):
                o_ref[pg] = y[pg * PAGE:(pg + 1) * PAGE, :].T
        elif kind == "blockmean":
            o_ref[0] = jnp.zeros((8, n), F32)
            for bi in range(tm // ATT_BLOCK):
                o_ref[0, bi:bi + 1, :] = jnp.sum(y[bi * ATT_BLOCK:(bi + 1) * ATT_BLOCK, :], axis=0,
                                                 keepdims=True) * (1.0 / ATT_BLOCK)


def norm_linear(x, g, w, outs):
    n, d = x.shape
    m = w.shape[1]
    tm = min(n, 512)
    specs, shapes = [], []
    for kind, _, s in outs:
        if kind in ("f32", "bf16"):
            specs.append(pl.BlockSpec((tm, s), lambda i: (i, 0)))
            shapes.append(jax.ShapeDtypeStruct((n, s), F32 if kind == "f32" else BF16))
        elif kind == "pages":
            specs.append(pl.BlockSpec((tm // PAGE, s, PAGE), lambda i: (i, 0, 0)))
            shapes.append(jax.ShapeDtypeStruct((n // PAGE, s, PAGE), F32))
        else:
            specs.append(pl.BlockSpec((1, 8, s), lambda i: (i, 0, 0)))
            shapes.append(jax.ShapeDtypeStruct((n // tm, 8, s), F32))
    return pl.pallas_call(
        functools.partial(_norm_linear_kernel, outs=outs, tm=tm),
        grid=(n // tm,),
        in_specs=[pl.BlockSpec((tm, d), lambda i: (i, 0)),
                  pl.BlockSpec((1, d), lambda i: (0, 0)),
                  pl.BlockSpec((d, m), lambda i: (0, 0))],
        out_specs=specs,
        out_shape=shapes,
        compiler_params=_cp("arbitrary"),
        name="norm_linear",
    )(x, g.reshape(1, d), w.astype(BF16))


def _out_proj_kernel(a_ref, b_ref, w_ref, x_ref, g_ref, o_ref):
    y = _dot(a_ref[...].astype(BF16), w_ref[:GROUP, :]) + _dot(b_ref[...].astype(BF16), w_ref[GROUP:, :])
    o_ref[...] = x_ref[...] + _rms(y, g_ref[...], 1e-6)


def out_proj(a, b, w, x, g):
    n, d = x.shape
    tm = min(n, 512)
    return pl.pallas_call(
        _out_proj_kernel,
        grid=(n // tm,),
        in_specs=[pl.BlockSpec((tm, GROUP), lambda i: (i, 0)),
                  pl.BlockSpec((tm, GROUP), lambda i: (i, 0)),
                  pl.BlockSpec((2 * GROUP, d), lambda i: (0, 0)),
                  pl.BlockSpec((tm, d), lambda i: (i, 0)),
                  pl.BlockSpec((1, d), lambda i: (0, 0))],
        out_specs=pl.BlockSpec((tm, d), lambda i: (i, 0)),
        out_shape=jax.ShapeDtypeStruct((n, d), F32),
        compiler_params=_cp("arbitrary"),
        name="out_proj",
    )(a, b, w.astype(BF16), x, g.reshape(1, d))


def _ffn_tail(acc, x, gpost, p, wg_ref, wp_ref):
    x1 = x + _rms(acc, gpost, 1e-6)
    gate = _sigmoid(_dot(x1.astype(BF16), wg_ref[...]))
    return x1 + gate * _dot(p.astype(BF16), wp_ref[...])


def _ffn_kernel(x_ref, p_ref, gpre_ref, wug_ref, wuv_ref, cwg_ref, cwv_ref, cbg_ref, cbv_ref, wd_ref,
                gpost_ref, wg_ref, wp_ref, o_ref, st_ref, hn, acc, carry, *, tm, f, nf):
    t = pl.program_id(1)
    j = pl.program_id(2)

    @pl.when(j == 0)
    def _():
        hn[...] = _rms(x_ref[...], gpre_ref[...], 1e-6).astype(BF16)
        acc[...] = jnp.zeros_like(acc)

    @pl.when(t == 0)
    def _():
        carry[j] = jnp.zeros((8, 2 * f), F32)

    h = hn[...]
    ug = _dot(h, wug_ref[...])
    uv = _dot(h, wuv_ref[...])
    prev = carry[j]

    def conv(u, pv, cw_ref, cb_ref):
        cw = cw_ref[...]
        u1 = _shift_rows(u, 1, [pv[7:8]])
        u2 = _shift_rows(u, 2, [pv[6:7], pv[7:8]])
        return cb_ref[...] + cw[0:1] * u2 + cw[1:2] * u1 + cw[2:3] * u

    cg = conv(ug, prev[:, :f], cwg_ref, cbg_ref)
    cv = conv(uv, prev[:, f:], cwv_ref, cbv_ref)
    carry[j] = jnp.concatenate([ug[tm - 8:, :], uv[tm - 8:, :]], axis=1)
    st_ref[0, 0, 0, 0:1, :] = ug[tm - 2:tm - 1, :]
    st_ref[0, 0, 0, 1:2, :] = uv[tm - 2:tm - 1, :]
    st_ref[0, 0, 1, 0:1, :] = ug[tm - 1:tm, :]
    st_ref[0, 0, 1, 1:2, :] = uv[tm - 1:tm, :]
    act = (_gelu(cg) * cv).astype(BF16)
    acc[...] += _dot(act, wd_ref[...])

    @pl.when(j == nf - 1)
    def _():
        o_ref[...] = _ffn_tail(acc[...], x_ref[...], gpost_ref[...], p_ref[...], wg_ref, wp_ref)


def ffn_prompt(x, p, gpre, w_up, conv_w, conv_b, w_down, gpost, w_gate, w_proj, batch, tm=1024, f=512):
    n, d = x.shape
    dff = w_down.shape[0]
    nf = dff // f
    nt = n // batch // tm
    pd = p.shape[1]
    w_up = w_up.astype(BF16)
    conv_b = conv_b.reshape(1, 2 * dff)
    row = lambda b, t, j: (b * nt + t, 0)
    const = lambda b, t, j: (0, 0)
    out, st = pl.pallas_call(
        functools.partial(_ffn_kernel, tm=tm, f=f, nf=nf),
        grid=(batch, nt, nf),
        in_specs=[pl.BlockSpec((tm, d), row),
                  pl.BlockSpec((tm, pd), row),
                  pl.BlockSpec((1, d), const),
                  pl.BlockSpec((d, f), lambda b, t, j: (0, j)),
                  pl.BlockSpec((d, f), lambda b, t, j: (0, nf + j)),
                  pl.BlockSpec((3, f), lambda b, t, j: (0, j)),
                  pl.BlockSpec((3, f), lambda b, t, j: (0, nf + j)),
                  pl.BlockSpec((1, f), lambda b, t, j: (0, j)),
                  pl.BlockSpec((1, f), lambda b, t, j: (0, nf + j)),
                  pl.BlockSpec((f, d), lambda b, t, j: (j, 0)),
                  pl.BlockSpec((1, d), const),
                  pl.BlockSpec((d, d), const),
                  pl.BlockSpec((pd, d), const)],
        out_specs=[pl.BlockSpec((tm, d), row),
                   pl.BlockSpec((1, 1, 2, 2, f), lambda b, t, j: (b, t, 0, 0, j))],
        out_shape=[jax.ShapeDtypeStruct((n, d), F32),
                   jax.ShapeDtypeStruct((batch, nt, 2, 2, dff), F32)],
        scratch_shapes=[pltpu.VMEM((tm, d), BF16), pltpu.VMEM((tm, d), F32), pltpu.VMEM((nf, 8, 2 * f), F32)],
        compiler_params=_cp("arbitrary", "arbitrary", "arbitrary"),
        name="ffn_prompt",
    )(x, p, gpre.reshape(1, d), w_up, w_up, conv_w, conv_w, conv_b, conv_b, w_down.astype(BF16),
      gpost.reshape(1, d), w_gate.astype(BF16), w_proj.astype(BF16))
    return out, st[:, nt - 1].reshape(batch, 2, 2 * dff)


def _ffn_step_kernel(x_ref, p_ref, gpre_ref, wug_ref, wuv_ref, cwg_ref, cwv_ref, cbg_ref, cbv_ref, wd_ref,
                     gpost_ref, wg_ref, wp_ref, s0g_ref, s0v_ref, s1g_ref, s1v_ref,
                     o_ref, ug_ref, uv_ref, acc, *, nf):
    j = pl.program_id(0)

    @pl.when(j == 0)
    def _():
        acc[...] = jnp.zeros_like(acc)

    h = _rms(x_ref[...], gpre_ref[...], 1e-6).astype(BF16)
    ug = _dot(h, wug_ref[...])
    uv = _dot(h, wuv_ref[...])
    ug_ref[...] = ug
    uv_ref[...] = uv
    cwg = cwg_ref[...]
    cwv = cwv_ref[...]
    cg = cbg_ref[...] + cwg[0:1] * s0g_ref[...] + cwg[1:2] * s1g_ref[...] + cwg[2:3] * ug
    cv = cbv_ref[...] + cwv[0:1] * s0v_ref[...] + cwv[1:2] * s1v_ref[...] + cwv[2:3] * uv
    acc[...] += _dot((_gelu(cg) * cv).astype(BF16), wd_ref[...])

    @pl.when(j == nf - 1)
    def _():
        o_ref[...] = _ffn_tail(acc[...], x_ref[...], gpost_ref[...], p_ref[...], wg_ref, wp_ref)


def ffn_step(x, p, gpre, w_up, conv_w, conv_b, w_down, gpost, w_gate, w_proj, state, f=512):
    n, d = x.shape
    dff = w_down.shape[0]
    nf = dff // f
    pd = p.shape[1]
    w_up = w_up.astype(BF16)
    conv_b = conv_b.reshape(1, 2 * dff)
    s0, s1 = state[:, 0, :], state[:, 1, :]
    const = lambda j: (0, 0)
    lo = lambda j: (0, j)
    hi = lambda j: (0, nf + j)
    out, ug, uv = pl.pallas_call(
        functools.partial(_ffn_step_kernel, nf=nf),
        grid=(nf,),
        in_specs=[pl.BlockSpec((n, d), const),
                  pl.BlockSpec((n, pd), const),
                  pl.BlockSpec((1, d), const),
                  pl.BlockSpec((d, f), lo), pl.BlockSpec((d, f), hi),
                  pl.BlockSpec((3, f), lo), pl.BlockSpec((3, f), hi),
                  pl.BlockSpec((1, f), lo), pl.BlockSpec((1, f), hi),
                  pl.BlockSpec((f, d), lambda j: (j, 0)),
                  pl.BlockSpec((1, d), const),
                  pl.BlockSpec((d, d), const),
                  pl.BlockSpec((pd, d), const),
                  pl.BlockSpec((n, f), lo), pl.BlockSpec((n, f), hi),
                  pl.BlockSpec((n, f), lo), pl.BlockSpec((n, f), hi)],
        out_specs=[pl.BlockSpec((n, d), const), pl.BlockSpec((n, f), lo), pl.BlockSpec((n, f), lo)],
        out_shape=[jax.ShapeDtypeStruct((n, d), F32), jax.ShapeDtypeStruct((n, dff), F32),
                   jax.ShapeDtypeStruct((n, dff), F32)],
        scratch_shapes=[pltpu.VMEM((n, d), F32)],
        compiler_params=_cp("arbitrary"),
        name="ffn_step",
    )(x, p, gpre.reshape(1, d), w_up, w_up, conv_w, conv_w, conv_b, conv_b, w_down.astype(BF16),
      gpost.reshape(1, d), w_gate.astype(BF16), w_proj.astype(BF16), s0, s0, s1, s1)
    new_state = jnp.stack([s1, jnp.concatenate([ug, uv], axis=1)], axis=1)
    return out, new_state


def _lru_gates(xc, wa_ref, ba_ref, wx_ref, bx_ref, lam_ref):
    xb = xc.astype(BF16)
    r = _sigmoid(_dot(xb, wa_ref[...]) + ba_ref[...])
    i = _sigmoid(_dot(xb, wx_ref[...]) + bx_ref[...])
    lam = lam_ref[...]
    softplus_neg = jnp.maximum(-lam, 0.0) + jnp.log1p(jnp.exp(-jnp.abs(lam)))
    log_a = -8.0 * softplus_neg * r
    a = jnp.exp(log_a)
    th = jnp.tanh(log_a)
    u = jnp.sqrt(-2.0 * th / (1.0 - th)) * (i * xc)
    return a, u


def _lru_kernel(gate_ref, x_ref, cw_ref, cb_ref, wa_ref, ba_ref, wx_ref, bx_ref, lam_ref,
                y_ref, conv_ref, hlast_ref, xcarry, hcarry, *, tl):
    t = pl.program_id(1)

    @pl.when(t == 0)
    def _():
        xcarry[...] = jnp.zeros_like(xcarry)
        hcarry[...] = jnp.zeros_like(hcarry)

    x = x_ref[...]
    c = xcarry[...]
    cw = cw_ref[...]
    xs1 = _shift_rows(x, 1, [c[7:8]])
    xs2 = _shift_rows(x, 2, [c[6:7], c[7:8]])
    xs3 = _shift_rows(x, 3, [c[5:6], c[6:7], c[7:8]])
    xc = cb_ref[...] + cw[0:1] * xs3 + cw[1:2] * xs2 + cw[2:3] * xs1 + cw[3:4] * x
    a, u = _lru_gates(xc, wa_ref, ba_ref, wx_ref, bx_ref, lam_ref)
    row = lax.broadcasted_iota(jnp.int32, a.shape, 0)
    d = 1
    while d < tl:
        a_s = jnp.where(row < d, 1.0, pltpu.roll(a, d, 0))
        u_s = jnp.where(row < d, 0.0, pltpu.roll(u, d, 0))
        u = u + a * u_s
        a = a * a_s
        d *= 2
    h = a * hcarry[...] + u
    y_ref[...] = h * _gelu(gate_ref[...])
    hcarry[...] = h[tl - 1:tl, :]
    hlast_ref[0] = h[tl - 1:tl, :]
    xcarry[...] = x[tl - 8:, :]
    conv_ref[0] = x[tl - 3:, :]


def _block_diag(w):
    h, a, b = w.shape
    eye = jnp.eye(h, dtype=w.dtype)
    return (eye[:, None, :, None] * w[:, :, None, :]).reshape(h * a, h * b)


def lru_prompt(gate, x, conv_w, conv_b, w_a, b_a, w_x, b_x, lam, batch, tl=256):
    n, c = x.shape
    nt = n // batch // tl
    row = lambda b, t: (b * nt + t, 0)
    const = lambda b, t: (0, 0)
    vec = pl.BlockSpec((1, c), const)
    y, conv, hlast = pl.pallas_call(
        functools.partial(_lru_kernel, tl=tl),
        grid=(batch, nt),
        in_specs=[pl.BlockSpec((tl, c), row), pl.BlockSpec((tl, c), row),
                  pl.BlockSpec((4, c), const), vec,
                  pl.BlockSpec((c, c), const), vec, pl.BlockSpec((c, c), const), vec, vec],
        out_specs=[pl.BlockSpec((tl, c), row),
                   pl.BlockSpec((1, 3, c), lambda b, t: (b, 0, 0)),
                   pl.BlockSpec((1, 1, c), lambda b, t: (b, 0, 0))],
        out_shape=[jax.ShapeDtypeStruct((n, c), F32), jax.ShapeDtypeStruct((batch, 3, c), F32),
                   jax.ShapeDtypeStruct((batch, 1, c), F32)],
        scratch_shapes=[pltpu.VMEM((8, c), F32), pltpu.VMEM((1, c), F32)],
        compiler_params=_cp("arbitrary", "arbitrary"),
        name="lru_prompt",
    )(gate, x, conv_w, conv_b.reshape(1, c), _block_diag(w_a).astype(BF16), b_a.reshape(1, c),
      _block_diag(w_x).astype(BF16), b_x.reshape(1, c), lam.reshape(1, c))
    return y, conv, hlast.reshape(batch, c)


def _lru_step_kernel(gate_ref, x_ref, s0_ref, s1_ref, s2_ref, h0_ref, cw_ref, cb_ref, wa_ref, ba_ref,
                     wx_ref, bx_ref, lam_ref, y_ref, h_ref):
    cw = cw_ref[...]
    x = x_ref[...]
    xc = cb_ref[...] + cw[0:1] * s0_ref[...] + cw[1:2] * s1_ref[...] + cw[2:3] * s2_ref[...] + cw[3:4] * x
    a, u = _lru_gates(xc, wa_ref, ba_ref, wx_ref, bx_ref, lam_ref)
    h = a * h0_ref[...] + u
    h_ref[...] = h
    y_ref[...] = h * _gelu(gate_ref[...])


def lru_step(gate, x, conv_state, h0, conv_w, conv_b, w_a, b_a, w_x, b_x, lam):
    n, c = x.shape
    y, h = pl.pallas_call(
        _lru_step_kernel,
        out_shape=[jax.ShapeDtypeStruct((n, c), F32), jax.ShapeDtypeStruct((n, c), F32)],
        name="lru_step",
    )(gate, x, conv_state[:, 0], conv_state[:, 1], conv_state[:, 2], h0, conv_w, conv_b.reshape(1, c),
      _block_diag(w_a).astype(BF16), b_a.reshape(1, c), _block_diag(w_x).astype(BF16), b_x.reshape(1, c),
      lam.reshape(1, c))
    new_conv = jnp.stack([conv_state[:, 1], conv_state[:, 2], x], axis=1)
    return y, new_conv, h


N_BUCKETS = 32
T5_MAX_EXACT = 16
T5_MAX_DISTANCE = 128
NEG = -1e30
LOG2E = math.log2(math.e)
ATT_BLOCK = 256
ATT_SUB_BLOCKS = 4
MOBA_TOPK = 3


def _t5_bucket_np(rel):
    n = np.maximum(rel, 0)
    nf = np.maximum(n, 1).astype(np.float32)
    large = T5_MAX_EXACT + (np.log(nf / np.float32(T5_MAX_EXACT)) / np.float32(math.log(T5_MAX_DISTANCE / T5_MAX_EXACT))
                            * np.float32(N_BUCKETS - T5_MAX_EXACT)).astype(np.int32)
    large = np.minimum(large, N_BUCKETS - 1)
    return np.where(n < T5_MAX_EXACT, n, large).astype(np.int32)


def _prompt_bucket_table():
    r = np.arange(ATT_BLOCK)[:, None]
    c = np.arange(ATT_BLOCK)[None, :]
    tabs = []
    for o in range(2):
        rel = o * ATT_BLOCK + r - c
        tabs.append(np.where(rel >= 0, _t5_bucket_np(rel), -1))
    return np.stack(tabs).astype(np.int32)


def _bias_from_buckets(bucket, t5_ref, col):
    out = jnp.full(bucket.shape, NEG, F32)
    for b in range(N_BUCKETS):
        out = jnp.where(bucket == b, t5_ref[b, col], out)
    return out


def _attn_kernel(*refs, mode, nq, nsub, lam_init):
    if mode == "moba":
        (t5_ref, q_ref, kb, vb, bk_ref, kmean, o_ref, bias, qs_s, m_s, acc_s, sel_s, va) = refs
    else:
        (t5_ref, lam_ref, q_ref, kb, vb, bk_ref, sub_ref, o_ref, bias, qs_s, m_s, acc_s, l_s) = refs
    g = pl.program_id(1)
    qi = pl.program_id(2)
    blk = ATT_BLOCK
    lane = lax.broadcasted_iota(jnp.int32, (blk, 128), 1)
    upper = lane >= 64

    @pl.when(qi == 0)
    def _():
        if mode == "moba":
            def fill(j, carry):
                rows = pl.ds(pl.multiple_of(j * blk, blk), blk)
                v = vb[rows, :]
                one = jnp.ones_like(v)
                va[0, rows, :] = jnp.where(upper, one, v)
                va[1, rows, :] = jnp.where(upper, v, one)
                return carry
            lax.fori_loop(0, nq, fill, 0)
        for m in range(2):
            col = 2 * g + m
            bias[m, 0] = _bias_from_buckets(bk_ref[0], t5_ref, col) * LOG2E
            bias[m, 1] = _bias_from_buckets(bk_ref[1], t5_ref, col) * LOG2E
            bias[m, 2] = jnp.full((blk, blk), t5_ref[N_BUCKETS - 1, col] * LOG2E, F32)
            bias[m, 3] = jnp.full((blk, blk), NEG * LOG2E, F32)

    for h2 in range(nsub):
        q = q_ref[h2 * blk:(h2 + 1) * blk, :]
        qb = nsub * qi + h2
        for m in range(2):
            c = 2 * h2 + m
            qm = jnp.where(upper, q, 0.0) if m else jnp.where(upper, 0.0, q)
            qs_s[c] = (qm * (0.125 * LOG2E)).astype(BF16)
            m_s[c] = jnp.full((blk, 128), NEG, F32)
            acc_s[c] = jnp.zeros((blk, 128), F32)
            if mode != "moba":
                l_s[c] = jnp.zeros((blk, 128), F32)
            if mode == "moba":
                nb = kmean.shape[0]
                gate = _nt(kmean[...], qm)
                bi = lax.broadcasted_iota(jnp.int32, (nb, blk), 0)
                gt = jnp.where(bi < qb, gate, -jnp.inf)
                sel = jnp.zeros((nb, blk), F32)
                for _ in range(MOBA_TOPK):
                    mx = jnp.max(gt, axis=0, keepdims=True)
                    cand = jnp.where((gt == mx) & (mx > -jnp.inf), bi, nb)
                    first = jnp.min(cand, axis=0, keepdims=True)
                    pick = bi == first
                    sel = jnp.where(pick, 1.0, sel)
                    gt = jnp.where(pick, -jnp.inf, gt)
                sel = jnp.where(bi == qb, 1.0, sel)
                sel = jnp.concatenate([sel, jnp.zeros((128 - nb, blk), F32)], axis=0)
                sel_s[c] = sel.T.astype(BF16)

    def chain(c, j, dd, masked):
        m = c % 2
        rows = pl.ds(pl.multiple_of(j * blk, blk), blk)
        s = lax.dot_general(qs_s[c], kb[rows, :], NT, preferred_element_type=F32)
        if masked:
            oh = jnp.where(lax.broadcasted_iota(jnp.int32, (128, 128), 0) == j, 1.0, 0.0).astype(BF16)
            hit = _dot(sel_s[c], oh)
        yield
        s = s + bias[m, dd]
        if masked:
            col = (hit - 1.0) * (-NEG)
            s = s + jnp.concatenate([col, col], axis=1)
        mp = m_s[c]
        mn = jnp.maximum(mp, jnp.max(s, axis=1, keepdims=True))
        yield
        alpha = jnp.exp2(mp - mn)
        p = jnp.exp2(s - jnp.concatenate([mn, mn], axis=1))
        if mode == "moba":
            pv = _dot(p.astype(BF16), va[m, rows, :])
        else:
            pv = _dot(p.astype(BF16), vb[rows, :])
        yield
        if mode != "moba":
            l_s[c] = alpha * l_s[c] + jnp.sum(p, axis=1, keepdims=True)
        acc_s[c] = alpha * acc_s[c] + pv
        m_s[c] = mn

    last = nsub * qi + nsub - 1

    def body(d, carry):
        def table(c):
            dist = d - (nsub - 1 - c // 2)
            return jnp.where(dist < 0, 3, jnp.minimum(dist, 2))
        _round_robin([chain(c, last - d, table(c), mode == "moba") for c in range(2 * nsub)])
        return carry
    lax.fori_loop(0, last + 1, body, 0)

    for h2 in range(nsub):
        a0, a1 = acc_s[2 * h2], acc_s[2 * h2 + 1]
        if mode == "moba":
            out = jnp.where(upper, a1 / pltpu.roll(a1, 64, 1), a0 / pltpu.roll(a0, 64, 1))
        else:
            att = a0 / l_s[2 * h2] - lam_ref[0] * (a1 / l_s[2 * h2 + 1])
            out = _rms(att, sub_ref[...], 1e-5) * (1.0 - lam_init)
        o_ref[h2 * blk:(h2 + 1) * blk, :] = out


def attn_prompt(q, k, v, t5_bias, batch, mode, kmean=None, lam=None, subln_w=None, lam_init=0.0):
    n, c = q.shape
    t = n // batch
    nq = t // ATT_BLOCK
    ng = c // 128
    blk = ATT_BLOCK
    smem = pl.BlockSpec(memory_space=pltpu.SMEM)
    nsub = ATT_SUB_BLOCKS if nq % ATT_SUB_BLOCKS == 0 else 2
    nsteps = nq // nsub
    nchain = 2 * nsub
    qspec = pl.BlockSpec((nsub * blk, 128), lambda b, g, i: (b * nsteps + i, g))
    kvspec = pl.BlockSpec((t, 128), lambda b, g, i: (b, g))
    bkspec = pl.BlockSpec((2, blk, blk), lambda b, g, i: (0, 0, 0))
    scratch = [pltpu.VMEM((2, 4, blk, blk), F32),
               pltpu.VMEM((nchain, blk, 128), BF16)] + [pltpu.VMEM((nchain, blk, 128), F32)] * 2
    bk = jnp.asarray(_prompt_bucket_table())
    if mode == "moba":
        in_specs = [smem, qspec, kvspec, kvspec, bkspec, pl.BlockSpec((nq, 128), lambda b, g, i: (b, g))]
        args = (t5_bias, q, k, v, bk, kmean)
        scratch += [pltpu.VMEM((nchain, blk, 128), BF16), pltpu.VMEM((2, t, 128), BF16)]
    else:
        scratch += [pltpu.VMEM((nchain, blk, 128), F32)]
        in_specs = [smem, smem, qspec, kvspec, kvspec, bkspec, pl.BlockSpec((1, 128), lambda b, g, i: (0, 0))]
        args = (t5_bias, lam.reshape(1), q, k, v, bk, subln_w.reshape(1, 128))
    return pl.pallas_call(
        functools.partial(_attn_kernel, mode=mode, nq=nq, nsub=nsub, lam_init=lam_init),
        grid=(batch, ng, nsteps),
        in_specs=in_specs,
        out_specs=qspec,
        out_shape=jax.ShapeDtypeStruct((n, c), F32),
        scratch_shapes=scratch,
        compiler_params=_cp("arbitrary", "arbitrary", "arbitrary"),
        name="attn_" + mode,
    )(*args)


RWKV_COLS = 1792
RWKV_LN_EPS = 64e-5


def _seg_ones(n):
    r = lax.broadcasted_iota(jnp.int32, (n, n), 0) // 64
    c = lax.broadcasted_iota(jnp.int32, (n, n), 1) // 64
    return jnp.where(r == c, 1.0, 0.0).astype(F32)


def _rwkv_prep_math(z, z_prev, mu_ref, w0_ref, w2_ref, a0_ref, a2_ref, g2_ref, kk_ref, ka_ref, outs):
    r_ref, e_ref, kkn_ref, ab_ref, k2_ref, v_ref, gate_ref = outs
    g = GROUP
    zs = z + mu_ref[...] * (z_prev - z)
    k = zs[:, g:2 * g]
    lora = zs[:, 3 * g:3 * g + 128]
    wlin = w0_ref[...] + _dot(jnp.tanh(lora).astype(BF16), w2_ref[...])
    softplus_neg = jnp.maximum(-wlin, 0.0) + jnp.log1p(jnp.exp(-jnp.abs(wlin)))
    a = _sigmoid(a0_ref[...] + _dot(lora.astype(BF16), a2_ref[...]))
    kk = k * kk_ref[...]
    norm = jnp.sqrt(_seg_sum(kk * kk, g))
    kk = kk / jnp.maximum(norm, 1e-12)
    r_ref[...] = zs[:, :g]
    e_ref[...] = jnp.exp(-softplus_neg - 0.5)
    kkn_ref[...] = kk
    ab_ref[...] = kk * a
    k2_ref[...] = k * (1.0 + (a - 1.0) * ka_ref[...])
    v_ref[...] = zs[:, 2 * g:3 * g]
    gate_ref[...] = _dot(_sigmoid(zs[:, 3 * g + 128:]).astype(BF16), g2_ref[...])


def _rwkv_prep_kernel(z_ref, mu_ref, w0_ref, w2_ref, a0_ref, a2_ref, g2_ref, kk_ref, ka_ref, *rest):
    outs, zcarry = rest[:7], rest[7]
    t = pl.program_id(1)

    @pl.when(t == 0)
    def _():
        zcarry[...] = jnp.zeros_like(zcarry)

    z = z_ref[...]
    z_prev = _shift_rows(z, 1, [zcarry[7:8, :]])
    zcarry[...] = z[z.shape[0] - 8:, :]
    _rwkv_prep_math(z, z_prev, mu_ref, w0_ref, w2_ref, a0_ref, a2_ref, g2_ref, kk_ref, ka_ref, outs)


def _rwkv_prep_step_kernel(z_ref, zp_ref, mu_ref, w0_ref, w2_ref, a0_ref, a2_ref, g2_ref, kk_ref, ka_ref, *outs):
    _rwkv_prep_math(z_ref[...], zp_ref[...], mu_ref, w0_ref, w2_ref, a0_ref, a2_ref, g2_ref, kk_ref, ka_ref, outs)


def _rwkv_prep_params(mu, w0, w2, a0, a2, g2, k_k, k_a):
    g = GROUP
    zero = jnp.zeros_like(w2)
    return (mu.reshape(1, RWKV_COLS), w0.reshape(1, g), jnp.concatenate([w2, zero], 0).astype(BF16),
            a0.reshape(1, g), jnp.concatenate([zero, a2], 0).astype(BF16), g2.astype(BF16),
            k_k.reshape(1, g), k_a.reshape(1, g))


def rwkv_prep(z, batch, params, shift=None, tl=256):
    n = z.shape[0]
    g = GROUP
    out_shape = [jax.ShapeDtypeStruct((n, g), F32)] * 7
    if shift is not None:
        return pl.pallas_call(_rwkv_prep_step_kernel, out_shape=out_shape, name="rwkv_prep_step")(z, shift, *params)
    nt = n // batch // tl
    row = lambda b, t: (b * nt + t, 0)
    const = lambda b, t: (0, 0)
    vec = pl.BlockSpec((1, g), const)
    lora = pl.BlockSpec((128, g), const)
    return pl.pallas_call(
        _rwkv_prep_kernel,
        grid=(batch, nt),
        in_specs=[pl.BlockSpec((tl, RWKV_COLS), row), pl.BlockSpec((1, RWKV_COLS), const),
                  vec, lora, vec, lora, lora, vec, vec],
        out_specs=[pl.BlockSpec((tl, g), row)] * 7,
        out_shape=out_shape,
        scratch_shapes=[pltpu.VMEM((8, RWKV_COLS), F32)],
        compiler_params=_cp("arbitrary", "arbitrary"),
        name="rwkv_prep",
    )(z, *params)


NN = (((1,), (0,)), ((), ()))
NT = (((1,), (1,)), ((), ()))
TN = (((0,), (0,)), ((), ()))


def _nt(a, b):
    return lax.dot_general(a, b, NT, precision=HI, preferred_element_type=F32)


def _split(x):
    hi = x.astype(BF16)
    return hi, (x - hi.astype(F32)).astype(BF16)


def _mm3(a, b, dims):
    ah, al = a if isinstance(a, tuple) else _split(a)
    bh, bl = b if isinstance(b, tuple) else _split(b)
    dg = functools.partial(lax.dot_general, dimension_numbers=dims, preferred_element_type=F32)
    return dg(ah, bh) + (dg(ah, bl) + dg(al, bh))


def _seg_sum(x, n):
    ones = _seg_ones(n).astype(BF16)
    hi, lo = _split(x)
    lo2 = (x - hi.astype(F32) - lo.astype(F32)).astype(BF16)
    return _dot(hi, ones) + (_dot(lo, ones) + _dot(lo2, ones))


def _rwkv_chunk_pair(r, e, kk, ab, k2, v, c):
    shape = (c, 128)
    upper = lax.broadcasted_iota(jnp.int32, shape, 1) >= 64
    row = lax.broadcasted_iota(jnp.int32, shape, 0)
    cum = e
    d = 1
    while d < c:
        cum = cum + jnp.where(row < d, 0.0, pltpu.roll(cum, d, 0))
        d *= 2
    g_inv = jnp.exp(cum)
    at = -kk * jnp.exp(e - cum)
    bt = ab * g_inv
    kt = k2 * g_inv
    rt = r * jnp.exp(-cum)
    g_end = jnp.exp(-cum[c - 1:c, :])

    def stack(x):
        return jnp.concatenate([jnp.where(upper, 0.0, x), jnp.where(upper, x, 0.0)], axis=0)

    def fold(x):
        return x[:c, :] + x[c:, :]

    a_st, r_st, v_st = _split(stack(at)), _split(stack(rt)), _split(stack(v))
    b2 = _split(jnp.concatenate([bt, bt], axis=0))
    k2s = _split(jnp.concatenate([kt, kt], axis=0))
    ri = lax.broadcasted_iota(jnp.int32, (2 * c, 2 * c), 0)
    ci = lax.broadcasted_iota(jnp.int32, (2 * c, 2 * c), 1)
    same = (ri >= c) == (ci >= c)
    strict = same & (ri > ci)
    incl = same & (ri >= ci)
    lab = jnp.where(strict, _mm3(a_st, b2, NT), 0.0)
    lak = jnp.where(strict, _mm3(a_st, k2s, NT), 0.0)
    rb = _split(jnp.where(incl, _mm3(r_st, b2, NT), 0.0))
    rk = jnp.where(incl, _mm3(r_st, k2s, NT), 0.0)
    x = jnp.where(ri == ci, 1.0, 0.0) + lab
    p = lab
    lv = _mm3(lak, v_st, NN)
    yield
    n = 2
    while n < c:
        ps = _split(p)
        p = _mm3(ps, ps, NN)
        x = x + _mm3(x, p, NN)
        n *= 2
        yield
    xs = _split(x)
    pa_st = _mm3(xs, a_st, NN)
    q_st = _mm3(xs, lv, NN)
    yield
    pa = fold(pa_st)
    q = fold(q_st)
    y1 = rt + fold(_mm3(rb, pa_st, NN))
    y0 = fold(_mm3(rb, q_st, NN) + _mm3(rk, v_st, NN))
    bg = _split(bt * g_end)
    r2 = lax.broadcasted_iota(jnp.int32, (128, 128), 0)
    c2 = lax.broadcasted_iota(jnp.int32, (128, 128), 1)
    same_head = (r2 >= 64) == (c2 >= 64)
    m = jnp.where(r2 == c2, g_end, 0.0) + jnp.where(same_head, _mm3(bg, pa, TN), 0.0)
    nn = jnp.where(same_head, _mm3(bg, q, TN) + _mm3(kt * g_end, v, TN), 0.0)
    return m, nn, y1, y0


def _rwkv_chunk_kernel(r_ref, e_ref, kk_ref, ab_ref, k2_ref, v_ref, m_ref, n_ref, y1_ref, y0_ref, *, c):
    lanes = [slice(g * 128, (g + 1) * 128) for g in range(GROUP // 128)]
    chains = [_rwkv_chunk_pair(r_ref[:, ls], e_ref[:, ls], kk_ref[:, ls], ab_ref[:, ls], k2_ref[:, ls],
                               v_ref[:, ls], c) for ls in lanes]
    for g, (ls, (m, nn, y1, y0)) in enumerate(zip(lanes, _round_robin(chains))):
        m_ref[0, g] = m
        n_ref[0, g] = nn
        y1_ref[:, ls] = y1
        y0_ref[:, ls] = y0


def _rwkv_scan_kernel(m_ref, n_ref, y1_ref, y0_ref, r_ref, k2_ref, v_ref, gate_ref, h0_ref,
                      lnw_ref, lnb_ref, rk_ref, y_ref, hout_ref, h):
    t = pl.program_id(1)

    @pl.when(t == 0)
    def _():
        h[...] = h0_ref[0]

    def pair(g):
        ls = slice(g * 128, (g + 1) * 128)
        hg = _split(h[g])
        h[g] = _mm3(m_ref[0, g], hg, NN) + n_ref[0, g]
        y = _mm3(y1_ref[:, ls], hg, NN) + y0_ref[:, ls]
        bonus = _seg_sum(r_ref[:, ls] * k2_ref[:, ls] * rk_ref[:, ls], 128) * v_ref[:, ls]
        yield
        mean = _seg_sum(y, 128) * (1.0 / 64)
        yield
        yc = y - mean
        var = _seg_sum(yc * yc, 128) * (1.0 / 64)
        yield
        yn = yc * lax.rsqrt(var + RWKV_LN_EPS) * lnw_ref[:, ls] + lnb_ref[:, ls]
        y_ref[:, ls] = (yn + bonus) * gate_ref[:, ls]

    _round_robin([pair(g) for g in range(GROUP // 128)])
    hout_ref[0] = h[...]


def rwkv_scan(r, e, kk, ab, k2, v, gate, s0, ln_w, ln_b, r_k, batch, c):
    n, g = r.shape
    nc = n // batch // c
    npair = g // 128
    row = lambda b, t: (b * nc + t, 0)
    blk = pl.BlockSpec((c, g), row)
    mat = pl.BlockSpec((1, npair, 128, 128), lambda b, t: (b * nc + t, 0, 0, 0))
    m, nn, y1, y0 = pl.pallas_call(
        functools.partial(_rwkv_chunk_kernel, c=c),
        grid=(batch, nc),
        in_specs=[blk] * 6,
        out_specs=[mat, mat, blk, blk],
        out_shape=[jax.ShapeDtypeStruct((batch * nc, npair, 128, 128), F32)] * 2
        + [jax.ShapeDtypeStruct((n, g), F32)] * 2,
        compiler_params=_cp("arbitrary", "arbitrary"),
        name="rwkv_chunk",
    )(r, e, kk, ab, k2, v)
    st = jnp.swapaxes(s0, -1, -2).reshape(batch, npair, 2, 64, 64)
    eye2 = jnp.eye(2, dtype=F32)
    h0 = (st[:, :, :, :, None, :] * eye2[None, None, :, None, :, None]).reshape(batch, npair, 128, 128)
    state = pl.BlockSpec((1, npair, 128, 128), lambda b, t: (b, 0, 0, 0))
    vec = pl.BlockSpec((1, g), lambda b, t: (0, 0))
    y, hout = pl.pallas_call(
        _rwkv_scan_kernel,
        grid=(batch, nc),
        in_specs=[mat, mat, blk, blk, blk, blk, blk, blk, state, vec, vec, vec],
        out_specs=[blk, state],
        out_shape=[jax.ShapeDtypeStruct((n, g), F32), jax.ShapeDtypeStruct((batch, npair, 128, 128), F32)],
        scratch_shapes=[pltpu.VMEM((npair, 128, 128), F32)],
        compiler_params=_cp("arbitrary", "arbitrary"),
        name="rwkv_scan",
    )(m, nn, y1, y0, r, k2, v, gate, h0, ln_w.reshape(1, g), ln_b.reshape(1, g), r_k.reshape(1, g))
    hb = hout.reshape(batch, npair, 2, 64, 2, 64)
    s_last = jnp.stack([hb[:, :, 0, :, 0, :], hb[:, :, 1, :, 1, :]], axis=2).reshape(batch, 2 * npair, 64, 64)
    return y, jnp.swapaxes(s_last, -1, -2)


PAGES_PER_STEP = 16


def _past_bucket_row(first_pos, width, q_pos):
    rel = q_pos - (first_pos + np.arange(width))
    return _t5_bucket_np(rel).astype(np.int32).reshape(1, width)


def _bias_rows(bucket_row, t5t_ref):
    out = jnp.zeros((8, bucket_row.shape[1]), F32)
    for b in range(N_BUCKETS):
        out = jnp.where(bucket_row == b, t5t_ref[:, b:b + 1], out)
    return out


def _page_logits(qcol, kt):
    return jnp.sum(qcol * kt, axis=1)


def _moba_scan_kernel(pt_ref, q_ref, *refs, n_steps, n_blocks):
    pages = refs[:PAGES_PER_STEP]
    lg_ref, idx_ref, gate_s = refs[PAGES_PER_STEP:]
    s = pl.program_id(1)

    @pl.when(s == 0)
    def _():
        gate_s[...] = jnp.zeros_like(gate_s)

    qcol = q_ref[0]
    lane = lax.broadcasted_iota(jnp.int32, (8, 128), 1)
    gate = gate_s[...]
    for i, pg in enumerate(pages):
        lg = _page_logits(qcol, pg[0])
        blk, half = divmod(i, 2)
        lg_ref[0, :, blk, half * PAGE:(half + 1) * PAGE] = lg * 0.125
        gate = gate + jnp.where(lane == s * (PAGES_PER_STEP // 2) + blk, jnp.sum(lg, axis=1, keepdims=True), 0.0)
    gate_s[...] = gate

    @pl.when(s == n_steps - 1)
    def _():
        gt = jnp.where(lane < n_blocks, gate * (1.0 / ATT_BLOCK), -jnp.inf)
        out = jnp.zeros((8, 128), jnp.int32)
        for j in range(MOBA_TOPK):
            mx = jnp.max(gt, axis=1, keepdims=True)
            first = jnp.min(jnp.where(gt == mx, lane, 128), axis=1, keepdims=True)
            out = jnp.where(lane == j, first, out)
            gt = jnp.where(lane == first, -jnp.inf, gt)
        idx_ref[0] = out


def _moba_gather_kernel(idx_ref, pt_ref, t5_ref, q_ref, kn_ref, vn_ref, lg_ref, bk_ref, *refs, n_blocks, nh):
    vpages = refs[:2 * MOBA_TOPK * nh]
    o_ref = refs[2 * MOBA_TOPK * nh]
    b = pl.program_id(0)
    for h in range(nh):
        q = q_ref[0, h:h + 1, :]
        own = jnp.sum(q * kn_ref[0, h:h + 1, :], axis=1, keepdims=True) * 0.125 + t5_ref[0, h]
        far = t5_ref[N_BUCKETS - 1, h]
        near = far
        for bkt in range(N_BUCKETS):
            near = jnp.where(bk_ref[...] == bkt, t5_ref[bkt, h], near)
        logits = []
        for s in range(MOBA_TOPK):
            blk = idx_ref[b, h * MOBA_TOPK + s]
            lg = lg_ref[0, h, pl.ds(blk, 1), :]
            logits.append(lg + jnp.where(blk == n_blocks - 1, near, far))
        mx = own
        for lg in logits:
            mx = jnp.maximum(mx, jnp.max(lg, axis=1, keepdims=True))
        p_own = jnp.exp(own - mx)
        den = p_own
        acc = p_own * vn_ref[0, h:h + 1, :]
        for s, lg in enumerate(logits):
            p = jnp.exp(lg - mx)
            den = den + jnp.sum(p, axis=1, keepdims=True)
            for pg in range(2):
                vt = vpages[(h * MOBA_TOPK + s) * 2 + pg][0, 0]
                acc = acc + _nt(p[:, pg * PAGE:(pg + 1) * PAGE], vt)
        o_ref[0, h:h + 1, :] = acc / den


def moba_step(q, k_new, v_new, cache_k, cache_v, page_table, t5_bias):
    nb, c = q.shape
    nh = c // 64
    n_pages = page_table.shape[1]
    n_steps = n_pages // PAGES_PER_STEP
    n_blocks = n_pages * PAGE // ATT_BLOCK
    bps = PAGES_PER_STEP // 2
    qcol = jnp.broadcast_to(q.reshape(nb, nh, 64, 1), (nb, nh, 64, PAGE))

    def page_spec(i):
        return pl.BlockSpec((1, nh, 64, PAGE), lambda b, s, pt: (pt[b, s * PAGES_PER_STEP + i], 0, 0, 0))

    logits, idx = pl.pallas_call(
        functools.partial(_moba_scan_kernel, n_steps=n_steps, n_blocks=n_blocks),
        grid_spec=pltpu.PrefetchScalarGridSpec(
            num_scalar_prefetch=1,
            grid=(nb, n_steps),
            in_specs=[pl.BlockSpec((1, nh, 64, PAGE), lambda b, s, pt: (b, 0, 0, 0))]
            + [page_spec(i) for i in range(PAGES_PER_STEP)],
            out_specs=[pl.BlockSpec((1, nh, bps, ATT_BLOCK), lambda b, s, pt: (b, 0, s, 0)),
                       pl.BlockSpec((1, nh, 128), lambda b, s, pt: (b, 0, 0))],
            scratch_shapes=[pltpu.VMEM((nh, 128), F32)]),
        out_shape=[jax.ShapeDtypeStruct((nb, nh, n_blocks, ATT_BLOCK), F32),
                   jax.ShapeDtypeStruct((nb, nh, 128), jnp.int32)],
        compiler_params=_cp("arbitrary", "arbitrary"),
        name="moba_scan",
    )(page_table, qcol, *([cache_k] * PAGES_PER_STEP))
    sel = idx[:, :, :MOBA_TOPK].reshape(nb, nh * MOBA_TOPK)
    bucket = jnp.asarray(_past_bucket_row((n_blocks - 1) * ATT_BLOCK, ATT_BLOCK, n_pages * PAGE))

    def vpage_spec(h, s, pg):
        def imap(b, sel, pt):
            return (pt[b, 2 * sel[b, h * MOBA_TOPK + s] + pg], h, 0, 0)
        return pl.BlockSpec((1, 1, 64, PAGE), imap)

    rows = pl.BlockSpec((1, nh, 64), lambda b, sel, pt: (b, 0, 0))
    out = pl.pallas_call(
        functools.partial(_moba_gather_kernel, n_blocks=n_blocks, nh=nh),
        grid_spec=pltpu.PrefetchScalarGridSpec(
            num_scalar_prefetch=2,
            grid=(nb,),
            in_specs=[pl.BlockSpec(memory_space=pltpu.SMEM), rows, rows, rows,
                      pl.BlockSpec((1, nh, n_blocks, ATT_BLOCK), lambda b, sel, pt: (b, 0, 0, 0)),
                      pl.BlockSpec((1, ATT_BLOCK), lambda b, sel, pt: (0, 0))]
            + [vpage_spec(h, s, pg) for h in range(nh) for s in range(MOBA_TOPK) for pg in range(2)],
            out_specs=rows),
        out_shape=jax.ShapeDtypeStruct((nb, nh, 64), F32),
        compiler_params=_cp("arbitrary"),
        name="moba_gather",
    )(sel, page_table, t5_bias, q.reshape(nb, nh, 64), k_new.reshape(nb, nh, 64),
      v_new.reshape(nb, nh, 64), logits, bucket, *([cache_v] * (2 * MOBA_TOPK * nh)))
    return out.reshape(nb, c)


def _diff_step_kernel(pt_ref, lam_ref, qcol_ref, q_ref, kn_ref, vn_ref, t5t_ref, bk_ref, sub_ref, spread_ref,
                      *refs, n_steps, lam_init):
    kpages = refs[:PAGES_PER_STEP]
    vpages = refs[PAGES_PER_STEP:2 * PAGES_PER_STEP]
    o_ref, m_s, l_s, acc_s = refs[2 * PAGES_PER_STEP:]
    s = pl.program_id(1)
    nh = vn_ref.shape[1]
    head_of_row = lax.broadcasted_iota(jnp.int32, (2 * nh, 128), 0) // 2
    own_head = (lax.broadcasted_iota(jnp.int32, (2 * nh, nh * PAGE), 1) % nh
                == lax.broadcasted_iota(jnp.int32, (2 * nh, nh * PAGE), 0) // 2)

    def per_head(rows):
        out = jnp.broadcast_to(rows[0], (2 * nh, 128))
        for h in range(1, nh):
            out = jnp.where(head_of_row == h, rows[h], out)
        return out

    @pl.when(s == 0)
    def _():
        m_s[...] = jnp.sum(q_ref[0] * kn_ref[0], axis=1, keepdims=True) * 0.125 + t5t_ref[:, 0:1]
        l_s[...] = jnp.ones_like(l_s)
        vn = vn_ref[0]
        acc_s[...] = per_head([vn[h:h + 1, :] for h in range(nh)])

    far = t5t_ref[:, N_BUCKETS - 1:N_BUCKETS]
    near = _bias_rows(bk_ref[...], t5t_ref)
    qcol = qcol_ref[0]
    m, l, acc = m_s[...], l_s[...], acc_s[...]
    for i in range(PAGES_PER_STEP):
        lg = _page_logits(qcol, kpages[i][0]) * 0.125
        if i == PAGES_PER_STEP - 1:
            lg = lg + jnp.where(s == n_steps - 1, near, far)
        else:
            lg = lg + far
        mn = jnp.maximum(m, jnp.max(lg, axis=1, keepdims=True))
        alpha = jnp.exp(m - mn)
        p = jnp.exp(lg - mn)
        l = alpha * l + jnp.sum(p, axis=1, keepdims=True)
        p_rows = jnp.where(own_head, _dot(p.astype(BF16), spread_ref[...]), 0.0).astype(BF16)
        acc = alpha * acc + _dot(p_rows, vpages[i][0].astype(BF16))
        m = mn
    m_s[...], l_s[...], acc_s[...] = m, l, acc

    @pl.when(s == n_steps - 1)
    def _():
        a = acc / l
        for h in range(nh):
            att = a[2 * h:2 * h + 1, :] - lam_ref[0] * a[2 * h + 1:2 * h + 2, :]
            o_ref[0, h:h + 1, :] = _rms(att, sub_ref[...], 1e-5) * (1.0 - lam_init)


def diff_step(q, k_new, v_new, cache_k, cache_v, page_table, t5_bias, lam, subln_w, lam_init):
    nb, c = q.shape
    nm = c // 64
    nh = nm // 2
    n_pages = page_table.shape[1]
    n_steps = n_pages // PAGES_PER_STEP
    bucket = jnp.asarray(_past_bucket_row((n_pages - 1) * PAGE, PAGE, n_pages * PAGE))
    qcol = jnp.broadcast_to(q.reshape(nb, nm, 64, 1), (nb, nm, 64, PAGE))

    def kpage_spec(i):
        return pl.BlockSpec((1, nm, 64, PAGE), lambda b, s, pt: (pt[b, s * PAGES_PER_STEP + i], 0, 0, 0))

    cache_v = cache_v.reshape(cache_v.shape[0], PAGE * nh, 128)
    spread = jnp.asarray(np.repeat(np.eye(PAGE, dtype=np.float32), nh, axis=1), BF16)

    def vpage_spec(i):
        return pl.BlockSpec((1, PAGE * nh, 128), lambda b, s, pt: (pt[b, s * PAGES_PER_STEP + i], 0, 0))

    maps = pl.BlockSpec((1, nm, 64), lambda b, s, pt: (b, 0, 0))
    heads = pl.BlockSpec((1, nh, 128), lambda b, s, pt: (b, 0, 0))
    out = pl.pallas_call(
        functools.partial(_diff_step_kernel, n_steps=n_steps, lam_init=lam_init),
        grid_spec=pltpu.PrefetchScalarGridSpec(
            num_scalar_prefetch=1,
            grid=(nb, n_steps),
            in_specs=[pl.BlockSpec(memory_space=pltpu.SMEM),
                      pl.BlockSpec((1, nm, 64, PAGE), lambda b, s, pt: (b, 0, 0, 0)), maps, maps, heads,
                      pl.BlockSpec((nm, N_BUCKETS), lambda b, s, pt: (0, 0)),
                      pl.BlockSpec((1, PAGE), lambda b, s, pt: (0, 0)),
                      pl.BlockSpec((1, 128), lambda b, s, pt: (0, 0)),
                      pl.BlockSpec((PAGE, PAGE * nh), lambda b, s, pt: (0, 0))]
            + [kpage_spec(i) for i in range(PAGES_PER_STEP)] + [vpage_spec(i) for i in range(PAGES_PER_STEP)],
            out_specs=heads,
            scratch_shapes=[pltpu.VMEM((nm, 1), F32), pltpu.VMEM((nm, 1), F32), pltpu.VMEM((nm, 128), F32)]),
        out_shape=jax.ShapeDtypeStruct((nb, nh, 128), F32),
        compiler_params=_cp("arbitrary", "arbitrary"),
        name="diff_step",
    )(page_table, lam.reshape(1), qcol, q.reshape(nb, nm, 64), k_new.reshape(nb, nm, 64), v_new.reshape(nb, nh, 128),
      t5_bias.T, bucket, subln_w.reshape(1, 128), spread, *([cache_k] * PAGES_PER_STEP),
      *([cache_v] * PAGES_PER_STEP))
    return out.reshape(nb, c)


def rwkv_mix(z, shift0, s0, mu, w0, w2, a0, a2, g2, k_k, k_a, r_k, ln_w, ln_b, batch, step):
    params = _rwkv_prep_params(mu, w0, w2, a0, a2, g2, k_k, k_a)
    if not step:
        outs = rwkv_prep(z, batch, params)
        r, e, kk, ab, k2, v, gate = outs
        return rwkv_scan(r, e, kk, ab, k2, v, gate, s0, ln_w, ln_b, r_k, batch, 64)
    outs = rwkv_prep(z, batch, params, shift=shift0)
    r, e, kk, ab, k2, v, gate = (jnp.pad(o[:, None, :], ((0, 0), (0, 7), (0, 0))).reshape(batch * 8, GROUP)
                                 for o in outs)
    y, s_last = rwkv_scan(r, e, kk, ab, k2, v, gate, s0, ln_w, ln_b, r_k, batch, 8)
    return y.reshape(batch, 8, GROUP)[:, 0], s_last


def _trunk(x, p, batch, step, st, W):
    depth = p.shape[0]
    outs = {k: [] for k in ("moba_k", "moba_v", "lru_conv", "lru_h", "diff_k", "diff_v",
                            "rwkv_shift", "rwkv_s", "ffn_conv")}
    t = x.shape[0] // batch
    for i in range(depth):
        j = i // 2
        g = GROUP
        n_pp = t // PAGE
        if i % 2 == 0:
            lru_w = (W["lru_conv_w"][j], W["lru_conv_b"][j], W["lru_w_a"][j], W["lru_b_a"][j],
                     W["lru_w_x"][j], W["lru_b_x"][j], W["lru_lambda"][j])
            cols = [("f32", c * g, g) for c in range(3)]
            if step:
                gate, xin, q, k, v = norm_linear(x, W["norm_mix_pre"][i], W["even_w_in"][j],
                                                 cols + [("f32", 3 * g, g), ("f32", 4 * g, g)])
                ya, conv, h_last = lru_step(gate, xin, st["lru_conv"][j], st["lru_h"][j], *lru_w)
                yb = moba_step(q, k, v, st["moba_k"][j], st["moba_v"][j], st["page_table"], W["t5_bias"])
                k_out, v_out = k.reshape(batch, t, 8, 64), v.reshape(batch, t, 8, 64)
            else:
                gate, xin, q, kb, vb, kt, vt, kmean = norm_linear(
                    x, W["norm_mix_pre"][i], W["even_w_in"][j],
                    cols + [("bf16", 3 * g, g), ("bf16", 4 * g, g), ("pages", 3 * g, g), ("pages", 4 * g, g),
                            ("blockmean", 3 * g, g)])
                ya, conv, h_last = lru_prompt(gate, xin, *lru_w, batch)
                blocks_per_tile = x.shape[0] // kmean.shape[0] // ATT_BLOCK
                kmean = kmean[:, :blocks_per_tile].reshape(-1, g)
                yb = attn_prompt(q, kb, vb, W["t5_bias"], batch, "moba", kmean=kmean)
                k_out = jnp.transpose(kt.reshape(batch, n_pp, 8, 64, PAGE), (0, 1, 4, 2, 3))
                v_out = jnp.transpose(vt.reshape(batch, n_pp, 8, 64, PAGE), (0, 1, 4, 2, 3))
            outs["moba_k"].append(k_out)
            outs["moba_v"].append(v_out)
            outs["lru_conv"].append(conv)
            outs["lru_h"].append(h_last)
            w_out = W["even_w_out"][j]
        else:
            rw = (W["rwkv_mu"][j], W["rwkv_w0"][j], W["rwkv_w2"][j], W["rwkv_a0"][j], W["rwkv_a2"][j],
                  W["rwkv_g2"][j], W["rwkv_k_k"][j], W["rwkv_k_a"][j], W["rwkv_r_k"][j],
                  W["rwkv_ln_w"][j], W["rwkv_ln_b"][j])
            lam_init = 0.8 - 0.6 * math.exp(-0.3 * i)
            lf = W["diff_lambda"][j]
            lam = jnp.exp(jnp.sum(lf[0] * lf[1])) - jnp.exp(jnp.sum(lf[2] * lf[3])) + lam_init
            o = RWKV_COLS
            cols = [("f32", 0, o), ("f32", o, g)]
            if step:
                z, q, k, v = norm_linear(x, W["norm_mix_pre"][i], W["odd_w_in"][j],
                                         cols + [("f32", o + g, g), ("f32", o + 2 * g, g)])
                ya, s_last = rwkv_mix(z, st["rwkv_shift"][j], st["rwkv_s"][j], *rw, batch, True)
                yb = diff_step(q, k, v, st["diff_k"][j], st["diff_v"][j], st["page_table"], W["t5_bias"],
                               lam, W["diff_subln_w"][j], lam_init)
                shift = z
                k_out = k.reshape(batch, t, 4, 2, 64)
            else:
                z, q, kb, vb, kt, v = norm_linear(
                    x, W["norm_mix_pre"][i], W["odd_w_in"][j],
                    cols + [("bf16", o + g, g), ("bf16", o + 2 * g, g), ("pages", o + g, g), ("f32", o + 2 * g, g)])
                s0 = jnp.zeros((batch, 8, 64, 64), F32)
                ya, s_last = rwkv_mix(z, None, s0, *rw, batch, False)
                yb = attn_prompt(q, kb, vb, W["t5_bias"], batch, "diff", lam=lam, subln_w=W["diff_subln_w"][j],
                                 lam_init=lam_init)
                shift = z.reshape(batch, t, RWKV_COLS)[:, t - 1]
                k_out = jnp.transpose(kt.reshape(batch, n_pp, 4, 2, 64, PAGE), (0, 1, 5, 2, 3, 4))
            outs["diff_k"].append(k_out)
            outs["diff_v"].append(v.reshape(k_out.shape[:-3] + (4, 128)))
            outs["rwkv_shift"].append(shift)
            outs["rwkv_s"].append(s_last)
            w_out = W["odd_w_out"][j]
        x = out_proj(ya, yb, w_out, x, W["norm_mix_post"][i])
        ffn_w = (W["norm_ffn_pre"][i], W["ffn_w_up"][i], W["ffn_conv_w"][i], W["ffn_conv_b"][i], W["ffn_w_down"][i],
                 W["norm_ffn_post"][i], W["ple_w_gate"][i], W["ple_w_proj"][i])
        if step:
            x, fbuf = ffn_step(x, p[i], *ffn_w, st["ffn_conv"][i])
        else:
            x, fbuf = ffn_prompt(x, p[i], *ffn_w, batch)
        outs["ffn_conv"].append(fbuf)
    return x, {k: jnp.stack(v) for k, v in outs.items()}


def kernel(x_prompt, x_sample, cache_moba_k, cache_moba_v, state_lru_conv, state_lru_h, cache_diff_k, cache_diff_v, state_rwkv_shift, state_rwkv, state_ffn_conv, page_table, p_prompt, p_sample, t5_bias, norm_mix_pre, norm_mix_post, norm_ffn_pre, norm_ffn_post, even_w_in, even_w_out, lru_conv_w, lru_conv_b, lru_w_a, lru_b_a, lru_w_x, lru_b_x, lru_lambda, odd_w_in, odd_w_out, rwkv_mu, rwkv_w0, rwkv_w2, rwkv_a0, rwkv_a2, rwkv_g2, rwkv_k_k, rwkv_k_a, rwkv_r_k, rwkv_ln_w, rwkv_ln_b, diff_lambda, diff_subln_w, ffn_w_up, ffn_conv_w, ffn_conv_b, ffn_w_down, ple_w_proj, ple_w_gate):
    W = dict(t5_bias=t5_bias, norm_mix_pre=norm_mix_pre, norm_mix_post=norm_mix_post,
             norm_ffn_pre=norm_ffn_pre, norm_ffn_post=norm_ffn_post,
             even_w_in=even_w_in, even_w_out=even_w_out, lru_conv_w=lru_conv_w, lru_conv_b=lru_conv_b,
             lru_w_a=lru_w_a, lru_b_a=lru_b_a, lru_w_x=lru_w_x, lru_b_x=lru_b_x, lru_lambda=lru_lambda,
             odd_w_in=odd_w_in, odd_w_out=odd_w_out, rwkv_mu=rwkv_mu, rwkv_w0=rwkv_w0, rwkv_w2=rwkv_w2,
             rwkv_a0=rwkv_a0, rwkv_a2=rwkv_a2, rwkv_g2=rwkv_g2, rwkv_k_k=rwkv_k_k, rwkv_k_a=rwkv_k_a,
             rwkv_r_k=rwkv_r_k, rwkv_ln_w=rwkv_ln_w, rwkv_ln_b=rwkv_ln_b,
             diff_lambda=diff_lambda, diff_subln_w=diff_subln_w,
             ffn_w_up=ffn_w_up, ffn_conv_w=ffn_conv_w, ffn_conv_b=ffn_conv_b, ffn_w_down=ffn_w_down,
             ple_w_proj=ple_w_proj, ple_w_gate=ple_w_gate)
    bp, tp, d = x_prompt.shape
    bs, ts, _ = x_sample.shape
    depth = p_prompt.shape[0]
    n_pp = tp // PAGE
    assert ts == 1, "the sample group is a single-token step"

    yp, P = _trunk(x_prompt.reshape(bp * tp, d), p_prompt.reshape(depth, bp * tp, -1), bp, False, None, W)

    pool = cache_moba_k.shape[1]
    st = dict(moba_k=jnp.transpose(cache_moba_k, (0, 1, 3, 4, 2)), moba_v=jnp.transpose(cache_moba_v, (0, 1, 3, 4, 2)),
              diff_k=jnp.transpose(cache_diff_k, (0, 1, 3, 4, 5, 2)).reshape(-1, pool, 8, 64, PAGE),
              diff_v=cache_diff_v,
              lru_conv=state_lru_conv, lru_h=state_lru_h, rwkv_shift=state_rwkv_shift, rwkv_s=state_rwkv,
              ffn_conv=state_ffn_conv, page_table=page_table)
    ys, S = _trunk(x_sample.reshape(bs * ts, d), p_sample.reshape(depth, bs * ts, -1), bs, True, st, W)

    return (yp.reshape(bp, tp, d), ys.reshape(bs, ts, d),
            P["moba_k"], P["moba_v"], S["moba_k"], S["moba_v"],
            P["lru_conv"], S["lru_conv"], P["lru_h"], S["lru_h"],
            P["diff_k"], P["diff_v"], S["diff_k"], S["diff_v"],
            P["rwkv_shift"], S["rwkv_shift"], P["rwkv_s"], S["rwkv_s"],
            P["ffn_conv"], S["ffn_conv"])
```

```python
import functools
import math

import numpy as np
import jax
import jax.numpy as jnp
from jax import lax
from jax.experimental import pallas as pl
from jax.experimental.pallas import tpu as pltpu

F32 = jnp.float32
BF16 = jnp.bfloat16
HI = lax.Precision.HIGHEST

D_MODEL = 1024
GROUP = 512
PAGE = 128
VMEM_LIMIT = 56 * 1024 * 1024


def _cp(*sem):
    return pltpu.CompilerParams(dimension_semantics=sem, vmem_limit_bytes=VMEM_LIMIT)


def _rms(x, g, eps):
    return x * lax.rsqrt(jnp.mean(x * x, axis=-1, keepdims=True) + eps) * g


def _gelu(x):
    return 0.5 * x * (1.0 + jnp.tanh(math.sqrt(2.0 / math.pi) * (x + 0.044715 * (x * x * x))))


def _sigmoid(x):
    return 1.0 / (1.0 + jnp.exp(-x))


def _dot(a, b):
    return jnp.dot(a, b, preferred_element_type=F32)


def _round_robin(chains):
    done = {}
    while len(done) < len(chains):
        for i, chain in enumerate(chains):
            if i not in done:
                try:
                    next(chain)
                except StopIteration as stop:
                    done[i] = stop.value
    return [done[i] for i in range(len(chains))]


def _shift_rows(x, d, fill):
    r = pltpu.roll(x, d, 0)
    row = lax.broadcasted_iota(jnp.int32, x.shape, 0)
    for i in range(d):
        r = jnp.where(row == i, fill[i], r)
    return r


def _norm_linear_kernel(x_ref, g_ref, w_ref, *out_refs, outs, tm):
    h = _rms(x_ref[...], g_ref[...], 1e-6).astype(BF16)
    ys = {}
    for o_ref, (kind, off, n) in zip(out_refs, outs):
        if (off, n) not in ys:
            ys[(off, n)] = _dot(h, w_ref[:, off:off + n])
        y = ys[(off, n)]
        if kind == "f32":
            o_ref[...] = y
        elif kind == "bf16":
            o_ref[...] = y.astype(BF16)
        elif kind == "pages":
            for pg in range(tm // PAGE):
                o_ref[pg] = y[pg * PAGE:(pg + 1) * PAGE, :].T
        elif kind == "blockmean":
            o_ref[0] = jnp.zeros((8, n), F32)
            for bi in range(tm // ATT_BLOCK):
                o_ref[0, bi:bi + 1, :] = jnp.sum(y[bi * ATT_BLOCK:(bi + 1) * ATT_BLOCK, :], axis=0,
                                                 keepdims=True) * (1.0 / ATT_BLOCK)


def norm_linear(x, g, w, outs):
    n, d = x.shape
    m = w.shape[1]
    tm = min(n, 512)
    specs, shapes = [], []
    for kind, _, s in outs:
        if kind in ("f32", "bf16"):
            specs.append(pl.BlockSpec((tm, s), lambda i: (i, 0)))
            shapes.append(jax.ShapeDtypeStruct((n, s), F32 if kind == "f32" else BF16))
        elif kind == "pages":
            specs.append(pl.BlockSpec((tm // PAGE, s, PAGE), lambda i: (i, 0, 0)))
            shapes.append(jax.ShapeDtypeStruct((n // PAGE, s, PAGE), F32))
        else:
            specs.append(pl.BlockSpec((1, 8, s), lambda i: (i, 0, 0)))
            shapes.append(jax.ShapeDtypeStruct((n // tm, 8, s), F32))
    return pl.pallas_call(
        functools.partial(_norm_linear_kernel, outs=outs, tm=tm),
        grid=(n // tm,),
        in_specs=[pl.BlockSpec((tm, d), lambda i: (i, 0)),
                  pl.BlockSpec((1, d), lambda i: (0, 0)),
                  pl.BlockSpec((d, m), lambda i: (0, 0))],
        out_specs=specs,
        out_shape=shapes,
        compiler_params=_cp("arbitrary"),
        name="norm_linear",
    )(x, g.reshape(1, d), w.astype(BF16))


def _out_proj_kernel(a_ref, b_ref, w_ref, x_ref, g_ref, o_ref):
    y = _dot(a_ref[...].astype(BF16), w_ref[:GROUP, :]) + _dot(b_ref[...].astype(BF16), w_ref[GROUP:, :])
    o_ref[...] = x_ref[...] + _rms(y, g_ref[...], 1e-6)


def out_proj(a, b, w, x, g):
    n, d = x.shape
    tm = min(n, 512)
    return pl.pallas_call(
        _out_proj_kernel,
        grid=(n // tm,),
        in_specs=[pl.BlockSpec((tm, GROUP), lambda i: (i, 0)),
                  pl.BlockSpec((tm, GROUP), lambda i: (i, 0)),
                  pl.BlockSpec((2 * GROUP, d), lambda i: (0, 0)),
                  pl.BlockSpec((tm, d), lambda i: (i, 0)),
                  pl.BlockSpec((1, d), lambda i: (0, 0))],
        out_specs=pl.BlockSpec((tm, d), lambda i: (i, 0)),
        out_shape=jax.ShapeDtypeStruct((n, d), F32),
        compiler_params=_cp("arbitrary"),
        name="out_proj",
    )(a, b, w.astype(BF16), x, g.reshape(1, d))


def _ffn_tail(acc, x, gpost, p, wg_ref, wp_ref):
    x1 = x + _rms(acc, gpost, 1e-6)
    gate = _sigmoid(_dot(x1.astype(BF16), wg_ref[...]))
    return x1 + gate * _dot(p.astype(BF16), wp_ref[...])


def _ffn_kernel(x_ref, p_ref, gpre_ref, wug_ref, wuv_ref, cwg_ref, cwv_ref, cbg_ref, cbv_ref, wd_ref,
                gpost_ref, wg_ref, wp_ref, o_ref, st_ref, hn, acc, carry, *, tm, f, nf):
    t = pl.program_id(1)
    j = pl.program_id(2)

    @pl.when(j == 0)
    def _():
        hn[...] = _rms(x_ref[...], gpre_ref[...], 1e-6).astype(BF16)
        acc[...] = jnp.zeros_like(acc)

    @pl.when(t == 0)
    def _():
        carry[j] = jnp.zeros((8, 2 * f), F32)

    h = hn[...]
    ug = _dot(h, wug_ref[...])
    uv = _dot(h, wuv_ref[...])
    prev = carry[j]

    def conv(u, pv, cw_ref, cb_ref):
        cw = cw_ref[...]
        u1 = _shift_rows(u, 1, [pv[7:8]])
        u2 = _shift_rows(u, 2, [pv[6:7], pv[7:8]])
        return cb_ref[...] + cw[0:1] * u2 + cw[1:2] * u1 + cw[2:3] * u

    cg = conv(ug, prev[:, :f], cwg_ref, cbg_ref)
    cv = conv(uv, prev[:, f:], cwv_ref, cbv_ref)
    carry[j] = jnp.concatenate([ug[tm - 8:, :], uv[tm - 8:, :]], axis=1)
    st_ref[0, 0, 0, 0:1, :] = ug[tm - 2:tm - 1, :]
    st_ref[0, 0, 0, 1:2, :] = uv[tm - 2:tm - 1, :]
    st_ref[0, 0, 1, 0:1, :] = ug[tm - 1:tm, :]
    st_ref[0, 0, 1, 1:2, :] = uv[tm - 1:tm, :]
    act = (_gelu(cg) * cv).astype(BF16)
    acc[...] += _dot(act, wd_ref[...])

    @pl.when(j == nf - 1)
    def _():
        o_ref[...] = _ffn_tail(acc[...], x_ref[...], gpost_ref[...], p_ref[...], wg_ref, wp_ref)


def ffn_prompt(x, p, gpre, w_up, conv_w, conv_b, w_down, gpost, w_gate, w_proj, batch, tm=1024, f=512):
    n, d = x.shape
    dff = w_down.shape[0]
    nf = dff // f
    nt = n // batch // tm
    pd = p.shape[1]
    w_up = w_up.astype(BF16)
    conv_b = conv_b.reshape(1, 2 * dff)
    row = lambda b, t, j: (b * nt + t, 0)
    const = lambda b, t, j: (0, 0)
    out, st = pl.pallas_call(
        functools.partial(_ffn_kernel, tm=tm, f=f, nf=nf),
        grid=(batch, nt, nf),
        in_specs=[pl.BlockSpec((tm, d), row),
                  pl.BlockSpec((tm, pd), row),
                  pl.BlockSpec((1, d), const),
                  pl.BlockSpec((d, f), lambda b, t, j: (0, j)),
                  pl.BlockSpec((d, f), lambda b, t, j: (0, nf + j)),
                  pl.BlockSpec((3, f), lambda b, t, j: (0, j)),
                  pl.BlockSpec((3, f), lambda b, t, j: (0, nf + j)),
                  pl.BlockSpec((1, f), lambda b, t, j: (0, j)),
                  pl.BlockSpec((1, f), lambda b, t, j: (0, nf + j)),
                  pl.BlockSpec((f, d), lambda b, t, j: (j, 0)),
                  pl.BlockSpec((1, d), const),
                  pl.BlockSpec((d, d), const),
                  pl.BlockSpec((pd, d), const)],
        out_specs=[pl.BlockSpec((tm, d), row),
                   pl.BlockSpec((1, 1, 2, 2, f), lambda b, t, j: (b, t, 0, 0, j))],
        out_shape=[jax.ShapeDtypeStruct((n, d), F32),
                   jax.ShapeDtypeStruct((batch, nt, 2, 2, dff), F32)],
        scratch_shapes=[pltpu.VMEM((tm, d), BF16), pltpu.VMEM((tm, d), F32), pltpu.VMEM((nf, 8, 2 * f), F32)],
        compiler_params=_cp("arbitrary", "arbitrary", "arbitrary"),
        name="ffn_prompt",
    )(x, p, gpre.reshape(1, d), w_up, w_up, conv_w, conv_w, conv_b, conv_b, w_down.astype(BF16),
      gpost.reshape(1, d), w_gate.astype(BF16), w_proj.astype(BF16))
    return out, st[:, nt - 1].reshape(batch, 2, 2 * dff)


def _ffn_step_kernel(x_ref, p_ref, gpre_ref, wug_ref, wuv_ref, cwg_ref, cwv_ref, cbg_ref, cbv_ref, wd_ref,
                     gpost_ref, wg_ref, wp_ref, s0g_ref, s0v_ref, s1g_ref, s1v_ref,
                     o_ref, ug_ref, uv_ref, acc, *, nf):
    j = pl.program_id(0)

    @pl.when(j == 0)
    def _():
        acc[...] = jnp.zeros_like(acc)

    h = _rms(x_ref[...], gpre_ref[...], 1e-6).astype(BF16)
    ug = _dot(h, wug_ref[...])
    uv = _dot(h, wuv_ref[...])
    ug_ref[...] = ug
    uv_ref[...] = uv
    cwg = cwg_ref[...]
    cwv = cwv_ref[...]
    cg = cbg_ref[...] + cwg[0:1] * s0g_ref[...] + cwg[1:2] * s1g_ref[...] + cwg[2:3] * ug
    cv = cbv_ref[...] + cwv[0:1] * s0v_ref[...] + cwv[1:2] * s1v_ref[...] + cwv[2:3] * uv
    acc[...] += _dot((_gelu(cg) * cv).astype(BF16), wd_ref[...])

    @pl.when(j == nf - 1)
    def _():
        o_ref[...] = _ffn_tail(acc[...], x_ref[...], gpost_ref[...], p_ref[...], wg_ref, wp_ref)


def ffn_step(x, p, gpre, w_up, conv_w, conv_b, w_down, gpost, w_gate, w_proj, state, f=512):
    n, d = x.shape
    dff = w_down.shape[0]
    nf = dff // f
    pd = p.shape[1]
    w_up = w_up.astype(BF16)
    conv_b = conv_b.reshape(1, 2 * dff)
    s0, s1 = state[:, 0, :], state[:, 1, :]
    const = lambda j: (0, 0)
    lo = lambda j: (0, j)
    hi = lambda j: (0, nf + j)
    out, ug, uv = pl.pallas_call(
        functools.partial(_ffn_step_kernel, nf=nf),
        grid=(nf,),
        in_specs=[pl.BlockSpec((n, d), const),
                  pl.BlockSpec((n, pd), const),
                  pl.BlockSpec((1, d), const),
                  pl.BlockSpec((d, f), lo), pl.BlockSpec((d, f), hi),
                  pl.BlockSpec((3, f), lo), pl.BlockSpec((3, f), hi),
                  pl.BlockSpec((1, f), lo), pl.BlockSpec((1, f), hi),
                  pl.BlockSpec((f, d), lambda j: (j, 0)),
                  pl.BlockSpec((1, d), const),
                  pl.BlockSpec((d, d), const),
                  pl.BlockSpec((pd, d), const),
                  pl.BlockSpec((n, f), lo), pl.BlockSpec((n, f), hi),
                  pl.BlockSpec((n, f), lo), pl.BlockSpec((n, f), hi)],
        out_specs=[pl.BlockSpec((n, d), const), pl.BlockSpec((n, f), lo), pl.BlockSpec((n, f), lo)],
        out_shape=[jax.ShapeDtypeStruct((n, d), F32), jax.ShapeDtypeStruct((n, dff), F32),
                   jax.ShapeDtypeStruct((n, dff), F32)],
        scratch_shapes=[pltpu.VMEM((n, d), F32)],
        compiler_params=_cp("arbitrary"),
        name="ffn_step",
    )(x, p, gpre.reshape(1, d), w_up, w_up, conv_w, conv_w, conv_b, conv_b, w_down.astype(BF16),
      gpost.reshape(1, d), w_gate.astype(BF16), w_proj.astype(BF16), s0, s0, s1, s1)
    new_state = jnp.stack([s1, jnp.concatenate([ug, uv], axis=1)], axis=1)
    return out, new_state


def _lru_gates(xc, wa_ref, ba_ref, wx_ref, bx_ref, lam_ref):
    xb = xc.astype(BF16)
    r = _sigmoid(_dot(xb, wa_ref[...]) + ba_ref[...])
    i = _sigmoid(_dot(xb, wx_ref[...]) + bx_ref[...])
    lam = lam_ref[...]
    softplus_neg = jnp.maximum(-lam, 0.0) + jnp.log1p(jnp.exp(-jnp.abs(lam)))
    log_a = -8.0 * softplus_neg * r
    a = jnp.exp(log_a)
    th = jnp.tanh(log_a)
    u = jnp.sqrt(-2.0 * th / (1.0 - th)) * (i * xc)
    return a, u


def _lru_kernel(gate_ref, x_ref, cw_ref, cb_ref, wa_ref, ba_ref, wx_ref, bx_ref, lam_ref,
                y_ref, conv_ref, hlast_ref, xcarry, hcarry, *, tl):
    t = pl.program_id(1)

    @pl.when(t == 0)
    def _():
        xcarry[...] = jnp.zeros_like(xcarry)
        hcarry[...] = jnp.zeros_like(hcarry)

    x = x_ref[...]
    c = xcarry[...]
    cw = cw_ref[...]
    xs1 = _shift_rows(x, 1, [c[7:8]])
    xs2 = _shift_rows(x, 2, [c[6:7], c[7:8]])
    xs3 = _shift_rows(x, 3, [c[5:6], c[6:7], c[7:8]])
    xc = cb_ref[...] + cw[0:1] * xs3 + cw[1:2] * xs2 + cw[2:3] * xs1 + cw[3:4] * x
    a, u = _lru_gates(xc, wa_ref, ba_ref, wx_ref, bx_ref, lam_ref)
    row = lax.broadcasted_iota(jnp.int32, a.shape, 0)
    d = 1
    while d < tl:
        a_s = jnp.where(row < d, 1.0, pltpu.roll(a, d, 0))
        u_s = jnp.where(row < d, 0.0, pltpu.roll(u, d, 0))
        u = u + a * u_s
        a = a * a_s
        d *= 2
    h = a * hcarry[...] + u
    y_ref[...] = h * _gelu(gate_ref[...])
    hcarry[...] = h[tl - 1:tl, :]
    hlast_ref[0] = h[tl - 1:tl, :]
    xcarry[...] = x[tl - 8:, :]
    conv_ref[0] = x[tl - 3:, :]


def _block_diag(w):
    h, a, b = w.shape
    eye = jnp.eye(h, dtype=w.dtype)
    return (eye[:, None, :, None] * w[:, :, None, :]).reshape(h * a, h * b)


def lru_prompt(gate, x, conv_w, conv_b, w_a, b_a, w_x, b_x, lam, batch, tl=256):
    n, c = x.shape
    nt = n // batch // tl
    row = lambda b, t: (b * nt + t, 0)
    const = lambda b, t: (0, 0)
    vec = pl.BlockSpec((1, c), const)
    y, conv, hlast = pl.pallas_call(
        functools.partial(_lru_kernel, tl=tl),
        grid=(batch, nt),
        in_specs=[pl.BlockSpec((tl, c), row), pl.BlockSpec((tl, c), row),
                  pl.BlockSpec((4, c), const), vec,
                  pl.BlockSpec((c, c), const), vec, pl.BlockSpec((c, c), const), vec, vec],
        out_specs=[pl.BlockSpec((tl, c), row),
                   pl.BlockSpec((1, 3, c), lambda b, t: (b, 0, 0)),
                   pl.BlockSpec((1, 1, c), lambda b, t: (b, 0, 0))],
        out_shape=[jax.ShapeDtypeStruct((n, c), F32), jax.ShapeDtypeStruct((batch, 3, c), F32),
                   jax.ShapeDtypeStruct((batch, 1, c), F32)],
        scratch_shapes=[pltpu.VMEM((8, c), F32), pltpu.VMEM((1, c), F32)],
        compiler_params=_cp("arbitrary", "arbitrary"),
        name="lru_prompt",
    )(gate, x, conv_w, conv_b.reshape(1, c), _block_diag(w_a).astype(BF16), b_a.reshape(1, c),
      _block_diag(w_x).astype(BF16), b_x.reshape(1, c), lam.reshape(1, c))
    return y, conv, hlast.reshape(batch, c)


def _lru_step_kernel(gate_ref, x_ref, s0_ref, s1_ref, s2_ref, h0_ref, cw_ref, cb_ref, wa_ref, ba_ref,
                     wx_ref, bx_ref, lam_ref, y_ref, h_ref):
    cw = cw_ref[...]
    x = x_ref[...]
    xc = cb_ref[...] + cw[0:1] * s0_ref[...] + cw[1:2] * s1_ref[...] + cw[2:3] * s2_ref[...] + cw[3:4] * x
    a, u = _lru_gates(xc, wa_ref, ba_ref, wx_ref, bx_ref, lam_ref)
    h = a * h0_ref[...] + u
    h_ref[...] = h
    y_ref[...] = h * _gelu(gate_ref[...])


def lru_step(gate, x, conv_state, h0, conv_w, conv_b, w_a, b_a, w_x, b_x, lam):
    n, c = x.shape
    y, h = pl.pallas_call(
        _lru_step_kernel,
        out_shape=[jax.ShapeDtypeStruct((n, c), F32), jax.ShapeDtypeStruct((n, c), F32)],
        name="lru_step",
    )(gate, x, conv_state[:, 0], conv_state[:, 1], conv_state[:, 2], h0, conv_w, conv_b.reshape(1, c),
      _block_diag(w_a).astype(BF16), b_a.reshape(1, c), _block_diag(w_x).astype(BF16), b_x.reshape(1, c),
      lam.reshape(1, c))
    new_conv = jnp.stack([conv_state[:, 1], conv_state[:, 2], x], axis=1)
    return y, new_conv, h


N_BUCKETS = 32
T5_MAX_EXACT = 16
T5_MAX_DISTANCE = 128
NEG = -1e30
LOG2E = math.log2(math.e)
ATT_BLOCK = 256
ATT_SUB_BLOCKS = 4
MOBA_TOPK = 3


def _t5_bucket_np(rel):
    n = np.maximum(rel, 0)
    nf = np.maximum(n, 1).astype(np.float32)
    large = T5_MAX_EXACT + (np.log(nf / np.float32(T5_MAX_EXACT)) / np.float32(math.log(T5_MAX_DISTANCE / T5_MAX_EXACT))
                            * np.float32(N_BUCKETS - T5_MAX_EXACT)).astype(np.int32)
    large = np.minimum(large, N_BUCKETS - 1)
    return np.where(n < T5_MAX_EXACT, n, large).astype(np.int32)


def _prompt_bucket_table():
    r = np.arange(ATT_BLOCK)[:, None]
    c = np.arange(ATT_BLOCK)[None, :]
    tabs = []
    for o in range(2):
        rel = o * ATT_BLOCK + r - c
        tabs.append(np.where(rel >= 0, _t5_bucket_np(rel), -1))
    return np.stack(tabs).astype(np.int32)


def _bias_from_buckets(bucket, t5_ref, col):
    out = jnp.full(bucket.shape, NEG, F32)
    for b in range(N_BUCKETS):
        out = jnp.where(bucket == b, t5_ref[b, col], out)
    return out


def _attn_kernel(*refs, mode, nq, nsub, peel, lam_init):
    if mode == "moba":
        (t5_ref, q_ref, kb, vb, bk_ref, kmean, o_ref, bias, qs_s, m_s, acc_s, sel_s, va) = refs
    else:
        (t5_ref, lam_ref, q_ref, kb, vb, bk_ref, sub_ref, o_ref, bias, qs_s, m_s, acc_s, l_s) = refs
    g = pl.program_id(1)
    qi = pl.program_id(2)
    blk = ATT_BLOCK
    lane = lax.broadcasted_iota(jnp.int32, (blk, 128), 1)
    upper = lane >= 64

    @pl.when(qi == 0)
    def _():
        if mode == "moba":
            def fill(j, carry):
                rows = pl.ds(pl.multiple_of(j * blk, blk), blk)
                v = vb[rows, :]
                one = jnp.ones_like(v)
                va[0, rows, :] = jnp.where(upper, one, v)
                va[1, rows, :] = jnp.where(upper, v, one)
                return carry
            lax.fori_loop(0, nq, fill, 0)
        for m in range(2):
            col = 2 * g + m
            bias[m, 0] = _bias_from_buckets(bk_ref[0], t5_ref, col) * LOG2E
            bias[m, 1] = _bias_from_buckets(bk_ref[1], t5_ref, col) * LOG2E
            bias[m, 2] = jnp.full((blk, blk), t5_ref[N_BUCKETS - 1, col] * LOG2E, F32)
            bias[m, 3] = jnp.full((blk, blk), NEG * LOG2E, F32)

    for h2 in range(nsub):
        q = q_ref[h2 * blk:(h2 + 1) * blk, :]
        qb = nsub * qi + h2
        for m in range(2):
            c = 2 * h2 + m
            qm = jnp.where(upper, q, 0.0) if m else jnp.where(upper, 0.0, q)
            qs_s[c] = (qm * (0.125 * LOG2E)).astype(BF16)
            m_s[c] = jnp.full((blk, 128), NEG, F32)
            acc_s[c] = jnp.zeros((blk, 128), F32)
            if mode != "moba":
                l_s[c] = jnp.zeros((blk, 128), F32)
            if mode == "moba":
                nb = kmean.shape[0]
                gate = _nt(kmean[...], qm)
                bi = lax.broadcasted_iota(jnp.int32, (nb, blk), 0)
                gt = jnp.where(bi < qb, gate, -jnp.inf)
                sel = jnp.zeros((nb, blk), F32)
                for _ in range(MOBA_TOPK):
                    mx = jnp.max(gt, axis=0, keepdims=True)
                    cand = jnp.where((gt == mx) & (mx > -jnp.inf), bi, nb)
                    first = jnp.min(cand, axis=0, keepdims=True)
                    pick = bi == first
                    sel = jnp.where(pick, 1.0, sel)
                    gt = jnp.where(pick, -jnp.inf, gt)
                sel = jnp.where(bi == qb, 1.0, sel)
                sel = jnp.concatenate([sel, jnp.zeros((128 - nb, blk), F32)], axis=0)
                sel_s[c] = sel.T.astype(BF16)

    def chain(c, j, dd, masked):
        m = c % 2
        rows = pl.ds(pl.multiple_of(j * blk, blk), blk)
        s = lax.dot_general(qs_s[c], kb[rows, :], NT, preferred_element_type=F32)
        if masked:
            oh = jnp.where(lax.broadcasted_iota(jnp.int32, (128, 128), 0) == j, 1.0, 0.0).astype(BF16)
            hit = _dot(sel_s[c], oh)
        yield
        s = s + bias[m, dd]
        if masked:
            col = (hit - 1.0) * (-NEG)
            s = s + jnp.concatenate([col, col], axis=1)
        mp = m_s[c]
        mn = jnp.maximum(mp, jnp.max(s, axis=1, keepdims=True))
        yield
        alpha = jnp.exp2(mp - mn)
        p = jnp.exp2(s - jnp.concatenate([mn, mn], axis=1))
        if mode == "moba":
            pv = _dot(p.astype(BF16), va[m, rows, :])
        else:
            pv = _dot(p.astype(BF16), vb[rows, :])
        yield
        if mode != "moba":
            l_s[c] = alpha * l_s[c] + jnp.sum(p, axis=1, keepdims=True)
        acc_s[c] = alpha * acc_s[c] + pv
        m_s[c] = mn

    last = nsub * qi + nsub - 1
    first = 0
    if peel:
        for d in range(nsub):
            dists = {h: d - (nsub - 1 - h) for h in range(nsub - 1 - d, nsub)}
            _round_robin([chain(2 * h + m, last - d, min(dist, 2), mode == "moba" and dist > 0)
                          for h, dist in dists.items() for m in range(2)])
        first = nsub

    def body(d, carry):
        def table(c):
            dist = d - (nsub - 1 - c // 2)
            return jnp.where(dist < 0, 3, jnp.minimum(dist, 2))
        _round_robin([chain(c, last - d, table(c), mode == "moba") for c in range(2 * nsub)])
        return carry
    lax.fori_loop(first, last + 1, body, 0)

    for h2 in range(nsub):
        a0, a1 = acc_s[2 * h2], acc_s[2 * h2 + 1]
        if mode == "moba":
            out = jnp.where(upper, a1 / pltpu.roll(a1, 64, 1), a0 / pltpu.roll(a0, 64, 1))
        else:
            att = a0 / l_s[2 * h2] - lam_ref[0] * (a1 / l_s[2 * h2 + 1])
            out = _rms(att, sub_ref[...], 1e-5) * (1.0 - lam_init)
        o_ref[h2 * blk:(h2 + 1) * blk, :] = out


def attn_prompt(q, k, v, t5_bias, batch, mode, kmean=None, lam=None, subln_w=None, lam_init=0.0):
    n, c = q.shape
    t = n // batch
    nq = t // ATT_BLOCK
    ng = c // 128
    blk = ATT_BLOCK
    smem = pl.BlockSpec(memory_space=pltpu.SMEM)
    peel = mode == "moba"
    nsub = 2 if peel or nq % ATT_SUB_BLOCKS else ATT_SUB_BLOCKS
    nsteps = nq // nsub
    nchain = 2 * nsub
    qspec = pl.BlockSpec((nsub * blk, 128), lambda b, g, i: (b * nsteps + i, g))
    kvspec = pl.BlockSpec((t, 128), lambda b, g, i: (b, g))
    bkspec = pl.BlockSpec((2, blk, blk), lambda b, g, i: (0, 0, 0))
    scratch = [pltpu.VMEM((2, 4, blk, blk), F32),
               pltpu.VMEM((nchain, blk, 128), BF16)] + [pltpu.VMEM((nchain, blk, 128), F32)] * 2
    bk = jnp.asarray(_prompt_bucket_table())
    if mode == "moba":
        in_specs = [smem, qspec, kvspec, kvspec, bkspec, pl.BlockSpec((nq, 128), lambda b, g, i: (b, g))]
        args = (t5_bias, q, k, v, bk, kmean)
        scratch += [pltpu.VMEM((nchain, blk, 128), BF16), pltpu.VMEM((2, t, 128), BF16)]
    else:
        scratch += [pltpu.VMEM((nchain, blk, 128), F32)]
        in_specs = [smem, smem, qspec, kvspec, kvspec, bkspec, pl.BlockSpec((1, 128), lambda b, g, i: (0, 0))]
        args = (t5_bias, lam.reshape(1), q, k, v, bk, subln_w.reshape(1, 128))
    return pl.pallas_call(
        functools.partial(_attn_kernel, mode=mode, nq=nq, nsub=nsub, peel=peel, lam_init=lam_init),
        grid=(batch, ng, nsteps),
        in_specs=in_specs,
        out_specs=qspec,
        out_shape=jax.ShapeDtypeStruct((n, c), F32),
        scratch_shapes=scratch,
        compiler_params=_cp("arbitrary", "arbitrary", "arbitrary"),
        name="attn_" + mode,
    )(*args)


RWKV_COLS = 1792
RWKV_LN_EPS = 64e-5


def _seg_ones(n):
    r = lax.broadcasted_iota(jnp.int32, (n, n), 0) // 64
    c = lax.broadcasted_iota(jnp.int32, (n, n), 1) // 64
    return jnp.where(r == c, 1.0, 0.0).astype(F32)


def _rwkv_prep_math(z, z_prev, mu_ref, w0_ref, w2_ref, a0_ref, a2_ref, g2_ref, kk_ref, ka_ref, outs):
    r_ref, e_ref, kkn_ref, ab_ref, k2_ref, v_ref, gate_ref = outs
    g = GROUP
    zs = z + mu_ref[...] * (z_prev - z)
    k = zs[:, g:2 * g]
    lora = zs[:, 3 * g:3 * g + 128]
    wlin = w0_ref[...] + _dot(jnp.tanh(lora).astype(BF16), w2_ref[...])
    softplus_neg = jnp.maximum(-wlin, 0.0) + jnp.log1p(jnp.exp(-jnp.abs(wlin)))
    a = _sigmoid(a0_ref[...] + _dot(lora.astype(BF16), a2_ref[...]))
    kk = k * kk_ref[...]
    norm = jnp.sqrt(_seg_sum(kk * kk, g))
    kk = kk / jnp.maximum(norm, 1e-12)
    r_ref[...] = zs[:, :g]
    e_ref[...] = jnp.exp(-softplus_neg - 0.5)
    kkn_ref[...] = kk
    ab_ref[...] = kk * a
    k2_ref[...] = k * (1.0 + (a - 1.0) * ka_ref[...])
    v_ref[...] = zs[:, 2 * g:3 * g]
    gate_ref[...] = _dot(_sigmoid(zs[:, 3 * g + 128:]).astype(BF16), g2_ref[...])


def _rwkv_prep_kernel(z_ref, mu_ref, w0_ref, w2_ref, a0_ref, a2_ref, g2_ref, kk_ref, ka_ref, *rest):
    outs, zcarry = rest[:7], rest[7]
    t = pl.program_id(1)

    @pl.when(t == 0)
    def _():
        zcarry[...] = jnp.zeros_like(zcarry)

    z = z_ref[...]
    z_prev = _shift_rows(z, 1, [zcarry[7:8, :]])
    zcarry[...] = z[z.shape[0] - 8:, :]
    _rwkv_prep_math(z, z_prev, mu_ref, w0_ref, w2_ref, a0_ref, a2_ref, g2_ref, kk_ref, ka_ref, outs)


def _rwkv_prep_step_kernel(z_ref, zp_ref, mu_ref, w0_ref, w2_ref, a0_ref, a2_ref, g2_ref, kk_ref, ka_ref, *outs):
    _rwkv_prep_math(z_ref[...], zp_ref[...], mu_ref, w0_ref, w2_ref, a0_ref, a2_ref, g2_ref, kk_ref, ka_ref, outs)


def _rwkv_prep_params(mu, w0, w2, a0, a2, g2, k_k, k_a):
    g = GROUP
    zero = jnp.zeros_like(w2)
    return (mu.reshape(1, RWKV_COLS), w0.reshape(1, g), jnp.concatenate([w2, zero], 0).astype(BF16),
            a0.reshape(1, g), jnp.concatenate([zero, a2], 0).astype(BF16), g2.astype(BF16),
            k_k.reshape(1, g), k_a.reshape(1, g))


def rwkv_prep(z, batch, params, shift=None, tl=256):
    n = z.shape[0]
    g = GROUP
    out_shape = [jax.ShapeDtypeStruct((n, g), F32)] * 7
    if shift is not None:
        return pl.pallas_call(_rwkv_prep_step_kernel, out_shape=out_shape, name="rwkv_prep_step")(z, shift, *params)
    nt = n // batch // tl
    row = lambda b, t: (b * nt + t, 0)
    const = lambda b, t: (0, 0)
    vec = pl.BlockSpec((1, g), const)
    lora = pl.BlockSpec((128, g), const)
    return pl.pallas_call(
        _rwkv_prep_kernel,
        grid=(batch, nt),
        in_specs=[pl.BlockSpec((tl, RWKV_COLS), row), pl.BlockSpec((1, RWKV_COLS), const),
                  vec, lora, vec, lora, lora, vec, vec],
        out_specs=[pl.BlockSpec((tl, g), row)] * 7,
        out_shape=out_shape,
        scratch_shapes=[pltpu.VMEM((8, RWKV_COLS), F32)],
        compiler_params=_cp("arbitrary", "arbitrary"),
        name="rwkv_prep",
    )(z, *params)


NN = (((1,), (0,)), ((), ()))
NT = (((1,), (1,)), ((), ()))
TN = (((0,), (0,)), ((), ()))


def _nt(a, b):
    return lax.dot_general(a, b, NT, precision=HI, preferred_element_type=F32)


def _split(x):
    hi = x.astype(BF16)
    return hi, (x - hi.astype(F32)).astype(BF16)


def _mm3(a, b, dims):
    ah, al = a if isinstance(a, tuple) else _split(a)
    bh, bl = b if isinstance(b, tuple) else _split(b)
    dg = functools.partial(lax.dot_general, dimension_numbers=dims, preferred_element_type=F32)
    return dg(ah, bh) + (dg(ah, bl) + dg(al, bh))


def _seg_sum(x, n):
    ones = _seg_ones(n).astype(BF16)
    hi, lo = _split(x)
    lo2 = (x - hi.astype(F32) - lo.astype(F32)).astype(BF16)
    return _dot(hi, ones) + (_dot(lo, ones) + _dot(lo2, ones))


def _rwkv_chunk_pair(r, e, kk, ab, k2, v, c):
    shape = (c, 128)
    upper = lax.broadcasted_iota(jnp.int32, shape, 1) >= 64
    row = lax.broadcasted_iota(jnp.int32, shape, 0)
    cum = e
    d = 1
    while d < c:
        cum = cum + jnp.where(row < d, 0.0, pltpu.roll(cum, d, 0))
        d *= 2
    g_inv = jnp.exp(cum)
    at = -kk * jnp.exp(e - cum)
    bt = ab * g_inv
    kt = k2 * g_inv
    rt = r * jnp.exp(-cum)
    g_end = jnp.exp(-cum[c - 1:c, :])

    def stack(x):
        return jnp.concatenate([jnp.where(upper, 0.0, x), jnp.where(upper, x, 0.0)], axis=0)

    def fold(x):
        return x[:c, :] + x[c:, :]

    a_st, r_st, v_st = _split(stack(at)), _split(stack(rt)), _split(stack(v))
    b2 = _split(jnp.concatenate([bt, bt], axis=0))
    k2s = _split(jnp.concatenate([kt, kt], axis=0))
    ri = lax.broadcasted_iota(jnp.int32, (2 * c, 2 * c), 0)
    ci = lax.broadcasted_iota(jnp.int32, (2 * c, 2 * c), 1)
    same = (ri >= c) == (ci >= c)
    strict = same & (ri > ci)
    incl = same & (ri >= ci)
    lab = jnp.where(strict, _mm3(a_st, b2, NT), 0.0)
    lak = jnp.where(strict, _mm3(a_st, k2s, NT), 0.0)
    rb = _split(jnp.where(incl, _mm3(r_st, b2, NT), 0.0))
    rk = jnp.where(incl, _mm3(r_st, k2s, NT), 0.0)
    x = jnp.where(ri == ci, 1.0, 0.0) + lab
    p = lab
    lv = _mm3(lak, v_st, NN)
    yield
    n = 2
    while n < c:
        ps = _split(p)
        p = _mm3(ps, ps, NN)
        x = x + _mm3(x, p, NN)
        n *= 2
        yield
    xs = _split(x)
    pa_st = _mm3(xs, a_st, NN)
    q_st = _mm3(xs, lv, NN)
    yield
    pa = fold(pa_st)
    q = fold(q_st)
    y1 = rt + fold(_mm3(rb, pa_st, NN))
    y0 = fold(_mm3(rb, q_st, NN) + _mm3(rk, v_st, NN))
    bg = _split(bt * g_end)
    r2 = lax.broadcasted_iota(jnp.int32, (128, 128), 0)
    c2 = lax.broadcasted_iota(jnp.int32, (128, 128), 1)
    same_head = (r2 >= 64) == (c2 >= 64)
    m = jnp.where(r2 == c2, g_end, 0.0) + jnp.where(same_head, _mm3(bg, pa, TN), 0.0)
    nn = jnp.where(same_head, _mm3(bg, q, TN) + _mm3(kt * g_end, v, TN), 0.0)
    return m, nn, y1, y0


def _rwkv_chunk_kernel(r_ref, e_ref, kk_ref, ab_ref, k2_ref, v_ref, m_ref, n_ref, y1_ref, y0_ref, *, c, per_step):
    work = [(k, g, slice(k * c, (k + 1) * c), slice(g * 128, (g + 1) * 128))
            for k in range(per_step) for g in range(GROUP // 128)]
    chains = [_rwkv_chunk_pair(r_ref[rs, ls], e_ref[rs, ls], kk_ref[rs, ls], ab_ref[rs, ls], k2_ref[rs, ls],
                               v_ref[rs, ls], c) for _, _, rs, ls in work]
    for (k, g, rs, ls), (m, nn, y1, y0) in zip(work, _round_robin(chains)):
        m_ref[k, g] = m
        n_ref[k, g] = nn
        y1_ref[rs, ls] = y1
        y0_ref[rs, ls] = y0


def _rwkv_scan_kernel(m_ref, n_ref, y1_ref, y0_ref, r_ref, k2_ref, v_ref, gate_ref, h0_ref,
                      lnw_ref, lnb_ref, rk_ref, y_ref, hout_ref, h):
    t = pl.program_id(1)

    @pl.when(t == 0)
    def _():
        h[...] = h0_ref[0]

    def pair(g):
        ls = slice(g * 128, (g + 1) * 128)
        hg = _split(h[g])
        h[g] = _mm3(m_ref[0, g], hg, NN) + n_ref[0, g]
        y = _mm3(y1_ref[:, ls], hg, NN) + y0_ref[:, ls]
        bonus = _seg_sum(r_ref[:, ls] * k2_ref[:, ls] * rk_ref[:, ls], 128) * v_ref[:, ls]
        yield
        mean = _seg_sum(y, 128) * (1.0 / 64)
        yield
        yc = y - mean
        var = _seg_sum(yc * yc, 128) * (1.0 / 64)
        yield
        yn = yc * lax.rsqrt(var + RWKV_LN_EPS) * lnw_ref[:, ls] + lnb_ref[:, ls]
        y_ref[:, ls] = (yn + bonus) * gate_ref[:, ls]

    _round_robin([pair(g) for g in range(GROUP // 128)])
    hout_ref[0] = h[...]


def rwkv_scan(r, e, kk, ab, k2, v, gate, s0, ln_w, ln_b, r_k, batch, c):
    n, g = r.shape
    nc = n // batch // c
    npair = g // 128
    row = lambda b, t: (b * nc + t, 0)
    blk = pl.BlockSpec((c, g), row)
    mat = pl.BlockSpec((1, npair, 128, 128), lambda b, t: (b * nc + t, 0, 0, 0))
    per_step = 2 if nc % 2 == 0 else 1
    nst = nc // per_step
    blk2 = pl.BlockSpec((per_step * c, g), lambda b, t: (b * nst + t, 0))
    mat2 = pl.BlockSpec((per_step, npair, 128, 128), lambda b, t: (b * nst + t, 0, 0, 0))
    m, nn, y1, y0 = pl.pallas_call(
        functools.partial(_rwkv_chunk_kernel, c=c, per_step=per_step),
        grid=(batch, nst),
        in_specs=[blk2] * 6,
        out_specs=[mat2, mat2, blk2, blk2],
        out_shape=[jax.ShapeDtypeStruct((batch * nc, npair, 128, 128), F32)] * 2
        + [jax.ShapeDtypeStruct((n, g), F32)] * 2,
        compiler_params=_cp("arbitrary", "arbitrary"),
        name="rwkv_chunk",
    )(r, e, kk, ab, k2, v)
    st = jnp.swapaxes(s0, -1, -2).reshape(batch, npair, 2, 64, 64)
    eye2 = jnp.eye(2, dtype=F32)
    h0 = (st[:, :, :, :, None, :] * eye2[None, None, :, None, :, None]).reshape(batch, npair, 128, 128)
    state = pl.BlockSpec((1, npair, 128, 128), lambda b, t: (b, 0, 0, 0))
    vec = pl.BlockSpec((1, g), lambda b, t: (0, 0))
    y, hout = pl.pallas_call(
        _rwkv_scan_kernel,
        grid=(batch, nc),
        in_specs=[mat, mat, blk, blk, blk, blk, blk, blk, state, vec, vec, vec],
        out_specs=[blk, state],
        out_shape=[jax.ShapeDtypeStruct((n, g), F32), jax.ShapeDtypeStruct((batch, npair, 128, 128), F32)],
        scratch_shapes=[pltpu.VMEM((npair, 128, 128), F32)],
        compiler_params=_cp("arbitrary", "arbitrary"),
        name="rwkv_scan",
    )(m, nn, y1, y0, r, k2, v, gate, h0, ln_w.reshape(1, g), ln_b.reshape(1, g), r_k.reshape(1, g))
    hb = hout.reshape(batch, npair, 2, 64, 2, 64)
    s_last = jnp.stack([hb[:, :, 0, :, 0, :], hb[:, :, 1, :, 1, :]], axis=2).reshape(batch, 2 * npair, 64, 64)
    return y, jnp.swapaxes(s_last, -1, -2)


PAGES_PER_STEP = 16


def _past_bucket_row(first_pos, width, q_pos):
    rel = q_pos - (first_pos + np.arange(width))
    return _t5_bucket_np(rel).astype(np.int32).reshape(1, width)


def _bias_rows(bucket_row, t5t_ref):
    out = jnp.zeros((8, bucket_row.shape[1]), F32)
    for b in range(N_BUCKETS):
        out = jnp.where(bucket_row == b, t5t_ref[:, b:b + 1], out)
    return out


def _page_logits(qcol, kt):
    return jnp.sum(qcol * kt, axis=1)


def _moba_scan_kernel(pt_ref, q_ref, *refs, n_steps, n_blocks):
    pages = refs[:PAGES_PER_STEP]
    lg_ref, idx_ref, gate_s = refs[PAGES_PER_STEP:]
    s = pl.program_id(1)

    @pl.when(s == 0)
    def _():
        gate_s[...] = jnp.zeros_like(gate_s)

    qcol = q_ref[0]
    lane = lax.broadcasted_iota(jnp.int32, (8, 128), 1)
    gate = gate_s[...]
    for i, pg in enumerate(pages):
        lg = _page_logits(qcol, pg[0])
        blk, half = divmod(i, 2)
        lg_ref[0, :, blk, half * PAGE:(half + 1) * PAGE] = lg * 0.125
        gate = gate + jnp.where(lane == s * (PAGES_PER_STEP // 2) + blk, jnp.sum(lg, axis=1, keepdims=True), 0.0)
    gate_s[...] = gate

    @pl.when(s == n_steps - 1)
    def _():
        gt = jnp.where(lane < n_blocks, gate * (1.0 / ATT_BLOCK), -jnp.inf)
        out = jnp.zeros((8, 128), jnp.int32)
        for j in range(MOBA_TOPK):
            mx = jnp.max(gt, axis=1, keepdims=True)
            first = jnp.min(jnp.where(gt == mx, lane, 128), axis=1, keepdims=True)
            out = jnp.where(lane == j, first, out)
            gt = jnp.where(lane == first, -jnp.inf, gt)
        idx_ref[0] = out


def _moba_gather_kernel(idx_ref, pt_ref, t5_ref, q_ref, kn_ref, vn_ref, lg_ref, bk_ref, *refs, n_blocks, nh):
    vpages = refs[:2 * MOBA_TOPK * nh]
    o_ref = refs[2 * MOBA_TOPK * nh]
    b = pl.program_id(0)
    for h in range(nh):
        q = q_ref[0, h:h + 1, :]
        own = jnp.sum(q * kn_ref[0, h:h + 1, :], axis=1, keepdims=True) * 0.125 + t5_ref[0, h]
        far = t5_ref[N_BUCKETS - 1, h]
        near = far
        for bkt in range(N_BUCKETS):
            near = jnp.where(bk_ref[...] == bkt, t5_ref[bkt, h], near)
        logits = []
        for s in range(MOBA_TOPK):
            blk = idx_ref[b, h * MOBA_TOPK + s]
            lg = lg_ref[0, h, pl.ds(blk, 1), :]
            logits.append(lg + jnp.where(blk == n_blocks - 1, near, far))
        mx = own
        for lg in logits:
            mx = jnp.maximum(mx, jnp.max(lg, axis=1, keepdims=True))
        p_own = jnp.exp(own - mx)
        den = p_own
        acc = p_own * vn_ref[0, h:h + 1, :]
        for s, lg in enumerate(logits):
            p = jnp.exp(lg - mx)
            den = den + jnp.sum(p, axis=1, keepdims=True)
            for pg in range(2):
                vt = vpages[(h * MOBA_TOPK + s) * 2 + pg][0, 0]
                acc = acc + _nt(p[:, pg * PAGE:(pg + 1) * PAGE], vt)
        o_ref[0, h:h + 1, :] = acc / den


def moba_step(q, k_new, v_new, cache_k, cache_v, page_table, t5_bias):
    nb, c = q.shape
    nh = c // 64
    n_pages = page_table.shape[1]
    n_steps = n_pages // PAGES_PER_STEP
    n_blocks = n_pages * PAGE // ATT_BLOCK
    bps = PAGES_PER_STEP // 2
    qcol = jnp.broadcast_to(q.reshape(nb, nh, 64, 1), (nb, nh, 64, PAGE))

    def page_spec(i):
        return pl.BlockSpec((1, nh, 64, PAGE), lambda b, s, pt: (pt[b, s * PAGES_PER_STEP + i], 0, 0, 0))

    logits, idx = pl.pallas_call(
        functools.partial(_moba_scan_kernel, n_steps=n_steps, n_blocks=n_blocks),
        grid_spec=pltpu.PrefetchScalarGridSpec(
            num_scalar_prefetch=1,
            grid=(nb, n_steps),
            in_specs=[pl.BlockSpec((1, nh, 64, PAGE), lambda b, s, pt: (b, 0, 0, 0))]
            + [page_spec(i) for i in range(PAGES_PER_STEP)],
            out_specs=[pl.BlockSpec((1, nh, bps, ATT_BLOCK), lambda b, s, pt: (b, 0, s, 0)),
                       pl.BlockSpec((1, nh, 128), lambda b, s, pt: (b, 0, 0))],
            scratch_shapes=[pltpu.VMEM((nh, 128), F32)]),
        out_shape=[jax.ShapeDtypeStruct((nb, nh, n_blocks, ATT_BLOCK), F32),
                   jax.ShapeDtypeStruct((nb, nh, 128), jnp.int32)],
        compiler_params=_cp("arbitrary", "arbitrary"),
        name="moba_scan",
    )(page_table, qcol, *([cache_k] * PAGES_PER_STEP))
    sel = idx[:, :, :MOBA_TOPK].reshape(nb, nh * MOBA_TOPK)
    bucket = jnp.asarray(_past_bucket_row((n_blocks - 1) * ATT_BLOCK, ATT_BLOCK, n_pages * PAGE))

    def vpage_spec(h, s, pg):
        def imap(b, sel, pt):
            return (pt[b, 2 * sel[b, h * MOBA_TOPK + s] + pg], h, 0, 0)
        return pl.BlockSpec((1, 1, 64, PAGE), imap)

    rows = pl.BlockSpec((1, nh, 64), lambda b, sel, pt: (b, 0, 0))
    out = pl.pallas_call(
        functools.partial(_moba_gather_kernel, n_blocks=n_blocks, nh=nh),
        grid_spec=pltpu.PrefetchScalarGridSpec(
            num_scalar_prefetch=2,
            grid=(nb,),
            in_specs=[pl.BlockSpec(memory_space=pltpu.SMEM), rows, rows, rows,
                      pl.BlockSpec((1, nh, n_blocks, ATT_BLOCK), lambda b, sel, pt: (b, 0, 0, 0)),
                      pl.BlockSpec((1, ATT_BLOCK), lambda b, sel, pt: (0, 0))]
            + [vpage_spec(h, s, pg) for h in range(nh) for s in range(MOBA_TOPK) for pg in range(2)],
            out_specs=rows),
        out_shape=jax.ShapeDtypeStruct((nb, nh, 64), F32),
        compiler_params=_cp("arbitrary"),
        name="moba_gather",
    )(sel, page_table, t5_bias, q.reshape(nb, nh, 64), k_new.reshape(nb, nh, 64),
      v_new.reshape(nb, nh, 64), logits, bucket, *([cache_v] * (2 * MOBA_TOPK * nh)))
    return out.reshape(nb, c)


def _diff_step_kernel(pt_ref, lam_ref, qcol_ref, q_ref, kn_ref, vn_ref, t5t_ref, bk_ref, sub_ref, spread_ref,
                      *refs, n_steps, lam_init):
    kpages = refs[:PAGES_PER_STEP]
    vpages = refs[PAGES_PER_STEP:2 * PAGES_PER_STEP]
    o_ref, m_s, l_s, acc_s = refs[2 * PAGES_PER_STEP:]
    s = pl.program_id(1)
    nh = vn_ref.shape[1]
    head_of_row = lax.broadcasted_iota(jnp.int32, (2 * nh, 128), 0) // 2
    own_head = (lax.broadcasted_iota(jnp.int32, (2 * nh, nh * PAGE), 1) % nh
                == lax.broadcasted_iota(jnp.int32, (2 * nh, nh * PAGE), 0) // 2)

    def per_head(rows):
        out = jnp.broadcast_to(rows[0], (2 * nh, 128))
        for h in range(1, nh):
            out = jnp.where(head_of_row == h, rows[h], out)
        return out

    @pl.when(s == 0)
    def _():
        m_s[...] = jnp.sum(q_ref[0] * kn_ref[0], axis=1, keepdims=True) * 0.125 + t5t_ref[:, 0:1]
        l_s[...] = jnp.ones_like(l_s)
        vn = vn_ref[0]
        acc_s[...] = per_head([vn[h:h + 1, :] for h in range(nh)])

    far = t5t_ref[:, N_BUCKETS - 1:N_BUCKETS]
    near = _bias_rows(bk_ref[...], t5t_ref)
    qcol = qcol_ref[0]
    m, l, acc = m_s[...], l_s[...], acc_s[...]
    for i in range(PAGES_PER_STEP):
        lg = _page_logits(qcol, kpages[i][0]) * 0.125
        if i == PAGES_PER_STEP - 1:
            lg = lg + jnp.where(s == n_steps - 1, near, far)
        else:
            lg = lg + far
        mn = jnp.maximum(m, jnp.max(lg, axis=1, keepdims=True))
        alpha = jnp.exp(m - mn)
        p = jnp.exp(lg - mn)
        l = alpha * l + jnp.sum(p, axis=1, keepdims=True)
        p_rows = jnp.where(own_head, _dot(p.astype(BF16), spread_ref[...]), 0.0).astype(BF16)
        acc = alpha * acc + _dot(p_rows, vpages[i][0].astype(BF16))
        m = mn
    m_s[...], l_s[...], acc_s[...] = m, l, acc

    @pl.when(s == n_steps - 1)
    def _():
        a = acc / l
        for h in range(nh):
            att = a[2 * h:2 * h + 1, :] - lam_ref[0] * a[2 * h + 1:2 * h + 2, :]
            o_ref[0, h:h + 1, :] = _rms(att, sub_ref[...], 1e-5) * (1.0 - lam_init)


def diff_step(q, k_new, v_new, cache_k, cache_v, page_table, t5_bias, lam, subln_w, lam_init):
    nb, c = q.shape
    nm = c // 64
    nh = nm // 2
    n_pages = page_table.shape[1]
    n_steps = n_pages // PAGES_PER_STEP
    bucket = jnp.asarray(_past_bucket_row((n_pages - 1) * PAGE, PAGE, n_pages * PAGE))
    qcol = jnp.broadcast_to(q.reshape(nb, nm, 64, 1), (nb, nm, 64, PAGE))

    def kpage_spec(i):
        return pl.BlockSpec((1, nm, 64, PAGE), lambda b, s, pt: (pt[b, s * PAGES_PER_STEP + i], 0, 0, 0))

    cache_v = cache_v.reshape(cache_v.shape[0], PAGE * nh, 128)
    spread = jnp.asarray(np.repeat(np.eye(PAGE, dtype=np.float32), nh, axis=1), BF16)

    def vpage_spec(i):
        return pl.BlockSpec((1, PAGE * nh, 128), lambda b, s, pt: (pt[b, s * PAGES_PER_STEP + i], 0, 0))

    maps = pl.BlockSpec((1, nm, 64), lambda b, s, pt: (b, 0, 0))
    heads = pl.BlockSpec((1, nh, 128), lambda b, s, pt: (b, 0, 0))
    out = pl.pallas_call(
        functools.partial(_diff_step_kernel, n_steps=n_steps, lam_init=lam_init),
        grid_spec=pltpu.PrefetchScalarGridSpec(
            num_scalar_prefetch=1,
            grid=(nb, n_steps),
            in_specs=[pl.BlockSpec(memory_space=pltpu.SMEM),
                      pl.BlockSpec((1, nm, 64, PAGE), lambda b, s, pt: (b, 0, 0, 0)), maps, maps, heads,
                      pl.BlockSpec((nm, N_BUCKETS), lambda b, s, pt: (0, 0)),
                      pl.BlockSpec((1, PAGE), lambda b, s, pt: (0, 0)),
                      pl.BlockSpec((1, 128), lambda b, s, pt: (0, 0)),
                      pl.BlockSpec((PAGE, PAGE * nh), lambda b, s, pt: (0, 0))]
            + [kpage_spec(i) for i in range(PAGES_PER_STEP)] + [vpage_spec(i) for i in range(PAGES_PER_STEP)],
            out_specs=heads,
            scratch_shapes=[pltpu.VMEM((nm, 1), F32), pltpu.VMEM((nm, 1), F32), pltpu.VMEM((nm, 128), F32)]),
        out_shape=jax.ShapeDtypeStruct((nb, nh, 128), F32),
        compiler_params=_cp("arbitrary", "arbitrary"),
        name="diff_step",
    )(page_table, lam.reshape(1), qcol, q.reshape(nb, nm, 64), k_new.reshape(nb, nm, 64), v_new.reshape(nb, nh, 128),
      t5_bias.T, bucket, subln_w.reshape(1, 128), spread, *([cache_k] * PAGES_PER_STEP),
      *([cache_v] * PAGES_PER_STEP))
    return out.reshape(nb, c)


def rwkv_mix(z, shift0, s0, mu, w0, w2, a0, a2, g2, k_k, k_a, r_k, ln_w, ln_b, batch, step):
    params = _rwkv_prep_params(mu, w0, w2, a0, a2, g2, k_k, k_a)
    if not step:
        outs = rwkv_prep(z, batch, params)
        r, e, kk, ab, k2, v, gate = outs
        return rwkv_scan(r, e, kk, ab, k2, v, gate, s0, ln_w, ln_b, r_k, batch, 64)
    outs = rwkv_prep(z, batch, params, shift=shift0)
    r, e, kk, ab, k2, v, gate = (jnp.pad(o[:, None, :], ((0, 0), (0, 7), (0, 0))).reshape(batch * 8, GROUP)
                                 for o in outs)
    y, s_last = rwkv_scan(r, e, kk, ab, k2, v, gate, s0, ln_w, ln_b, r_k, batch, 8)
    return y.reshape(batch, 8, GROUP)[:, 0], s_last


def _trunk(x, p, batch, step, st, W):
    depth = p.shape[0]
    outs = {k: [] for k in ("moba_k", "moba_v", "lru_conv", "lru_h", "diff_k", "diff_v",
                            "rwkv_shift", "rwkv_s", "ffn_conv")}
    t = x.shape[0] // batch
    for i in range(depth):
        j = i // 2
        g = GROUP
        n_pp = t // PAGE
        if i % 2 == 0:
            lru_w = (W["lru_conv_w"][j], W["lru_conv_b"][j], W["lru_w_a"][j], W["lru_b_a"][j],
                     W["lru_w_x"][j], W["lru_b_x"][j], W["lru_lambda"][j])
            cols = [("f32", c * g, g) for c in range(3)]
            if step:
                gate, xin, q, k, v = norm_linear(x, W["norm_mix_pre"][i], W["even_w_in"][j],
                                                 cols + [("f32", 3 * g, g), ("f32", 4 * g, g)])
                ya, conv, h_last = lru_step(gate, xin, st["lru_conv"][j], st["lru_h"][j], *lru_w)
                yb = moba_step(q, k, v, st["moba_k"][j], st["moba_v"][j], st["page_table"], W["t5_bias"])
                k_out, v_out = k.reshape(batch, t, 8, 64), v.reshape(batch, t, 8, 64)
            else:
                gate, xin, q, kb, vb, kt, vt, kmean = norm_linear(
                    x, W["norm_mix_pre"][i], W["even_w_in"][j],
                    cols + [("bf16", 3 * g, g), ("bf16", 4 * g, g), ("pages", 3 * g, g), ("pages", 4 * g, g),
                            ("blockmean", 3 * g, g)])
                ya, conv, h_last = lru_prompt(gate, xin, *lru_w, batch)
                blocks_per_tile = x.shape[0] // kmean.shape[0] // ATT_BLOCK
                kmean = kmean[:, :blocks_per_tile].reshape(-1, g)
                yb = attn_prompt(q, kb, vb, W["t5_bias"], batch, "moba", kmean=kmean)
                k_out = jnp.transpose(kt.reshape(batch, n_pp, 8, 64, PAGE), (0, 1, 4, 2, 3))
                v_out = jnp.transpose(vt.reshape(batch, n_pp, 8, 64, PAGE), (0, 1, 4, 2, 3))
            outs["moba_k"].append(k_out)
            outs["moba_v"].append(v_out)
            outs["lru_conv"].append(conv)
            outs["lru_h"].append(h_last)
            w_out = W["even_w_out"][j]
        else:
            rw = (W["rwkv_mu"][j], W["rwkv_w0"][j], W["rwkv_w2"][j], W["rwkv_a0"][j], W["rwkv_a2"][j],
                  W["rwkv_g2"][j], W["rwkv_k_k"][j], W["rwkv_k_a"][j], W["rwkv_r_k"][j],
                  W["rwkv_ln_w"][j], W["rwkv_ln_b"][j])
            lam_init = 0.8 - 0.6 * math.exp(-0.3 * i)
            lf = W["diff_lambda"][j]
            lam = jnp.exp(jnp.sum(lf[0] * lf[1])) - jnp.exp(jnp.sum(lf[2] * lf[3])) + lam_init
            o = RWKV_COLS
            cols = [("f32", 0, o), ("f32", o, g)]
            if step:
                z, q, k, v = norm_linear(x, W["norm_mix_pre"][i], W["odd_w_in"][j],
                                         cols + [("f32", o + g, g), ("f32", o + 2 * g, g)])
                ya, s_last = rwkv_mix(z, st["rwkv_shift"][j], st["rwkv_s"][j], *rw, batch, True)
                yb = diff_step(q, k, v, st["diff_k"][j], st["diff_v"][j], st["page_table"], W["t5_bias"],
                               lam, W["diff_subln_w"][j], lam_init)
                shift = z
                k_out = k.reshape(batch, t, 4, 2, 64)
            else:
                z, q, kb, vb, kt, v = norm_linear(
                    x, W["norm_mix_pre"][i], W["odd_w_in"][j],
                    cols + [("bf16", o + g, g), ("bf16", o + 2 * g, g), ("pages", o + g, g), ("f32", o + 2 * g, g)])
                s0 = jnp.zeros((batch, 8, 64, 64), F32)
                ya, s_last = rwkv_mix(z, None, s0, *rw, batch, False)
                yb = attn_prompt(q, kb, vb, W["t5_bias"], batch, "diff", lam=lam, subln_w=W["diff_subln_w"][j],
                                 lam_init=lam_init)
                shift = z.reshape(batch, t, RWKV_COLS)[:, t - 1]
                k_out = jnp.transpose(kt.reshape(batch, n_pp, 4, 2, 64, PAGE), (0, 1, 5, 2, 3, 4))
            outs["diff_k"].append(k_out)
            outs["diff_v"].append(v.reshape(k_out.shape[:-3] + (4, 128)))
            outs["rwkv_shift"].append(shift)
            outs["rwkv_s"].append(s_last)
            w_out = W["odd_w_out"][j]
        x = out_proj(ya, yb, w_out, x, W["norm_mix_post"][i])
        ffn_w = (W["norm_ffn_pre"][i], W["ffn_w_up"][i], W["ffn_conv_w"][i], W["ffn_conv_b"][i], W["ffn_w_down"][i],
                 W["norm_ffn_post"][i], W["ple_w_gate"][i], W["ple_w_proj"][i])
        if step:
            x, fbuf = ffn_step(x, p[i], *ffn_w, st["ffn_conv"][i])
        else:
            x, fbuf = ffn_prompt(x, p[i], *ffn_w, batch)
        outs["ffn_conv"].append(fbuf)
    return x, {k: jnp.stack(v) for k, v in outs.items()}


def kernel(x_prompt, x_sample, cache_moba_k, cache_moba_v, state_lru_conv, state_lru_h, cache_diff_k, cache_diff_v, state_rwkv_shift, state_rwkv, state_ffn_conv, page_table, p_prompt, p_sample, t5_bias, norm_mix_pre, norm_mix_post, norm_ffn_pre, norm_ffn_post, even_w_in, even_w_out, lru_conv_w, lru_conv_b, lru_w_a, lru_b_a, lru_w_x, lru_b_x, lru_lambda, odd_w_in, odd_w_out, rwkv_mu, rwkv_w0, rwkv_w2, rwkv_a0, rwkv_a2, rwkv_g2, rwkv_k_k, rwkv_k_a, rwkv_r_k, rwkv_ln_w, rwkv_ln_b, diff_lambda, diff_subln_w, ffn_w_up, ffn_conv_w, ffn_conv_b, ffn_w_down, ple_w_proj, ple_w_gate):
    W = dict(t5_bias=t5_bias, norm_mix_pre=norm_mix_pre, norm_mix_post=norm_mix_post,
             norm_ffn_pre=norm_ffn_pre, norm_ffn_post=norm_ffn_post,
             even_w_in=even_w_in, even_w_out=even_w_out, lru_conv_w=lru_conv_w, lru_conv_b=lru_conv_b,
             lru_w_a=lru_w_a, lru_b_a=lru_b_a, lru_w_x=lru_w_x, lru_b_x=lru_b_x, lru_lambda=lru_lambda,
             odd_w_in=odd_w_in, odd_w_out=odd_w_out, rwkv_mu=rwkv_mu, rwkv_w0=rwkv_w0, rwkv_w2=rwkv_w2,
             rwkv_a0=rwkv_a0, rwkv_a2=rwkv_a2, rwkv_g2=rwkv_g2, rwkv_k_k=rwkv_k_k, rwkv_k_a=rwkv_k_a,
             rwkv_r_k=rwkv_r_k, rwkv_ln_w=rwkv_ln_w, rwkv_ln_b=rwkv_ln_b,
             diff_lambda=diff_lambda, diff_subln_w=diff_subln_w,
             ffn_w_up=ffn_w_up, ffn_conv_w=ffn_conv_w, ffn_conv_b=ffn_conv_b, ffn_w_down=ffn_w_down,
             ple_w_proj=ple_w_proj, ple_w_gate=ple_w_gate)
    bp, tp, d = x_prompt.shape
    bs, ts, _ = x_sample.shape
    depth = p_prompt.shape[0]
    n_pp = tp // PAGE
    assert ts == 1, "the sample group is a single-token step"

    yp, P = _trunk(x_prompt.reshape(bp * tp, d), p_prompt.reshape(depth, bp * tp, -1), bp, False, None, W)

    pool = cache_moba_k.shape[1]
    st = dict(moba_k=jnp.transpose(cache_moba_k, (0, 1, 3, 4, 2)), moba_v=jnp.transpose(cache_moba_v, (0, 1, 3, 4, 2)),
              diff_k=jnp.transpose(cache_diff_k, (0, 1, 3, 4, 5, 2)).reshape(-1, pool, 8, 64, PAGE),
              diff_v=cache_diff_v,
              lru_conv=state_lru_conv, lru_h=state_lru_h, rwkv_shift=state_rwkv_shift, rwkv_s=state_rwkv,
              ffn_conv=state_ffn_conv, page_table=page_table)
    ys, S = _trunk(x_sample.reshape(bs * ts, d), p_sample.reshape(depth, bs * ts, -1), bs, True, st, W)

    return (yp.reshape(bp, tp, d), ys.reshape(bs, ts, d),
            P["moba_k"], P["moba_v"], S["moba_k"], S["moba_v"],
            P["lru_conv"], S["lru_conv"], P["lru_h"], S["lru_h"],
            P["diff_k"], P["diff_v"], S["diff_k"], S["diff_v"],
            P["rwkv_shift"], S["rwkv_shift"], P["rwkv_s"], S["rwkv_s"],
            P["ffn_conv"], S["ffn_conv"])
```
